```python
import math
import jax, jax.numpy as jnp
from jax import lax
import numpy as np

D_MODEL = 2048
BATCH = 4
SEQ = 2048
DEPTH = 4
DEC_BATCH = 128
DEC_SEQ = 4
PAST_LEN = 16384
PAGE_SIZE = 128

N_MIXERS = 4
N_A = (DEPTH + 3) // 4
N_B = (DEPTH + 2) // 4
N_C = (DEPTH + 1) // 4
N_D = DEPTH // 4
EPS = 1e-6
CONV_K = 4
CHUNK = 128
LRU_WIDTH = D_MODEL
LRU_BLOCKS = 8
LRU_BLOCK = LRU_WIDTH // LRU_BLOCKS
LRU_C = 8.0
RET_HEADS = 8
RET_DK = D_MODEL // RET_HEADS
RET_DV = 2 * RET_DK
RET_VW = RET_HEADS * RET_DV
ROPE_BASE = 10000.0
SSD_INNER = 2 * D_MODEL
SSD_HEADDIM = 64
SSD_HEADS = SSD_INNER // SSD_HEADDIM
SSD_GROUPS = 8
SSD_HPG = SSD_HEADS // SSD_GROUPS
SSD_STATE = 128
SSD_CONV_DIM = SSD_INNER + 2 * SSD_GROUPS * SSD_STATE
S5_WIDTH = D_MODEL
S5_GROUP = 16
S5_GROUPS = S5_WIDTH // S5_GROUP
S5_STATE = 64
MEM_LEN = 256
XA_HEADS = 4
XA_HD = D_MODEL // XA_HEADS

kernel_name = "hybrid_lru_ret_ssd_s5_memxattn_step"

F32 = jnp.float32


def rmsnorm(x, g):
    xf = x.astype(F32)
    y = xf * lax.rsqrt(jnp.mean(xf * xf, axis=-1, keepdims=True) + EPS)
    return (y * g.astype(F32)).astype(x.dtype)


def block_len(t):
    return CHUNK if t % CHUNK == 0 else t


def to_blocks(t, blk):
    bn, tl = t.shape[:2]
    return t.reshape((bn, tl // blk, blk) + t.shape[2:]).swapaxes(0, 1)


def from_blocks(t):
    nb, bn, blk = t.shape[:3]
    return t.swapaxes(0, 1).reshape((bn, nb * blk) + t.shape[3:])


def causal_conv(x, w, b, buf):
    xp = jnp.concatenate([buf.astype(x.dtype), x], axis=1)
    y = lax.conv_general_dilated(xp, w[:, None, :].astype(x.dtype), window_strides=(1,), padding='VALID',
                                 dimension_numbers=('NWC', 'WIO', 'NWC'), feature_group_count=x.shape[-1])
    return y + b, xp[:, -(CONV_K - 1):]


def _affine_combine(l, r):
    al, bl = l
    ar, br = r
    return al * ar, ar * bl + br


def affine_scan(a, b, h0):
    tl = b.shape[1]
    blk = block_len(tl)
    nb = tl // blk
    a_blocks = to_blocks(a, blk) if a.shape[:2] == b.shape[:2] else jnp.broadcast_to(a, (nb,) + a.shape)

    def step(h, inp):
        a_blk, b_blk = inp
        a_blk = jnp.broadcast_to(a_blk, b_blk.shape)
        a_cum, h_blk = lax.associative_scan(_affine_combine, (a_blk, b_blk), axis=1)
        h_blk = h_blk + a_cum * h[:, None]
        return h_blk[:, -1], h_blk

    h_last, hs = lax.scan(step, h0, (a_blocks, to_blocks(b, blk)))
    return from_blocks(hs), h_last


def rglru_mixer(u, w_in, conv_w, conv_b, wa, ba, wx, bx, lam, w_out, conv_buf, h0):
    bn, tl, _ = u.shape
    x, z = jnp.split(u @ w_in, 2, axis=-1)
    x, conv_new = causal_conv(x, conv_w, conv_b, conv_buf)
    xb = x.reshape(bn, tl, LRU_BLOCKS, LRU_BLOCK)
    r = jax.nn.sigmoid(jnp.einsum('btnc,ncd->btnd', xb, wa).reshape(bn, tl, LRU_WIDTH) + ba)
    gi = jax.nn.sigmoid(jnp.einsum('btnc,ncd->btnd', xb, wx).reshape(bn, tl, LRU_WIDTH) + bx)
    log_a = -LRU_C * r.astype(F32) * jax.nn.softplus(-lam.astype(F32))
    a = jnp.exp(log_a)
    b = jnp.sqrt(-jnp.expm1(2.0 * log_a)) * (gi * x).astype(F32)
    h, h_last = affine_scan(a, b, h0.astype(F32))
    y = h.astype(u.dtype) * jax.nn.silu(z)
    return y @ w_out, conv_new, h_last.astype(u.dtype)


def rope(x, pos):
    half = x.shape[-1] // 2
    inv = ROPE_BASE ** (-jnp.arange(half, dtype=F32) / half)
    ang = pos.astype(F32)[:, None] * inv
    cos = jnp.cos(ang)[None, :, None, :]
    sin = jnp.sin(ang)[None, :, None, :]
    xf = x.astype(F32)
    x1, x2 = xf[..., :half], xf[..., half:]
    return jnp.concatenate([x1 * cos - x2 * sin, x1 * sin + x2 * cos], axis=-1).astype(x.dtype)


def retention(q, k, v, s0):
    tl = q.shape[1]
    c = block_len(tl)
    log_g = jnp.log1p(-jnp.exp2(-5.0 - jnp.arange(RET_HEADS, dtype=F32)))
    idx = jnp.arange(c, dtype=F32)
    rel = idx[:, None] - idx[None, :]
    mask = rel >= 0
    decay_mask = jnp.where(mask, jnp.exp(log_g[:, None, None] * jnp.where(mask, rel, 0.0)), 0.0)
    q_decay = jnp.exp(log_g[:, None] * (idx + 1.0)).T[None, :, :, None]
    k_decay = jnp.exp(log_g[:, None] * (c - 1.0 - idx)).T[None, :, :, None]
    chunk_decay = jnp.exp(log_g * c)[None, :, None, None]

    def step(s, inp):
        qi, ki, vi = (t.astype(F32) for t in inp)
        scores = jnp.einsum('bthd,bshd->bhts', qi, ki) * decay_mask
        inner = jnp.einsum('bhts,bshv->bthv', scores, vi)
        cross = jnp.einsum('bthd,bhdv->bthv', qi, s) * q_decay
        s_new = s * chunk_decay + jnp.einsum('bshd,bshv->bhdv', ki * k_decay, vi)
        return s_new, inner + cross

    s_last, o = lax.scan(step, s0, (to_blocks(q, c), to_blocks(k, c), to_blocks(v, c)))
    return from_blocks(o), s_last


def retention_mixer(u, pos, w_in, gn, w_out, s0):
    bn, tl, _ = u.shape
    qk = RET_HEADS * RET_DK
    q, k, v, z = jnp.split(u @ w_in, [qk, 2 * qk, 2 * qk + RET_VW], axis=-1)
    q = rope(q.reshape(bn, tl, RET_HEADS, RET_DK), pos)
    k = rope(k.reshape(bn, tl, RET_HEADS, RET_DK), pos) * RET_DK ** -0.5
    v = v.reshape(bn, tl, RET_HEADS, RET_DV)
    o, s_last = retention(q, k, v, s0.astype(F32))
    mu = jnp.mean(o, axis=-1, keepdims=True)
    var = jnp.mean(jnp.square(o - mu), axis=-1, keepdims=True)
    o = ((o - mu) * lax.rsqrt(var + EPS)).reshape(bn, tl, RET_VW) * gn.astype(F32)
    y = o.astype(u.dtype) * jax.nn.silu(z)
    return y @ w_out, s_last.astype(u.dtype)


def ssd_scan(xdt, da, bm, cm, h0):
    c = block_len(xdt.shape[1])
    causal = jnp.tril(jnp.ones((c, c), dtype=bool))

    def step(h, inp):
        xc, ac, bc, cc = inp
        cum = jnp.moveaxis(jnp.cumsum(ac, axis=1), 1, -1)
        seg = cum[..., :, None] - cum[..., None, :]
        lmat = jnp.exp(jnp.where(causal, seg, -jnp.inf))
        cb = jnp.einsum('btgn,bsgn->bgts', cc, bc)
        y_diag = jnp.einsum('bgts,bgrts,bsgrp->btgrp', cb, lmat, xc)
        y_off = jnp.einsum('btgn,bgrpn,bgrt->btgrp', cc, h, jnp.exp(cum))
        to_end = jnp.exp(cum[..., -1:] - cum)
        h_new = h * jnp.exp(cum[..., -1])[..., None, None] + jnp.einsum('bsgn,bgrs,bsgrp->bgrpn', bc, to_end, xc)
        return h_new, y_diag + y_off

    h_last, y = lax.scan(step, h0, (to_blocks(xdt, c), to_blocks(da, c), to_blocks(bm, c), to_blocks(cm, c)))
    return from_blocks(y), h_last


def ssd_mixer(u, w_in, conv_w, conv_b, dt_bias, a_log, d_skip, norm_g, w_out, conv_buf, h0):
    bn, tl, _ = u.shape
    z, xbc, dt = jnp.split(u @ w_in, [SSD_INNER, SSD_INNER + SSD_CONV_DIM], axis=-1)
    xbc, conv_new = causal_conv(xbc, conv_w, conv_b, conv_buf)
    xbc = jax.nn.silu(xbc)
    gn = SSD_GROUPS * SSD_STATE
    x, bm, cm = jnp.split(xbc, [SSD_INNER, SSD_INNER + gn], axis=-1)
    x = x.astype(F32).reshape(bn, tl, SSD_GROUPS, SSD_HPG, SSD_HEADDIM)
    bm = bm.astype(F32).reshape(bn, tl, SSD_GROUPS, SSD_STATE)
    cm = cm.astype(F32).reshape(bn, tl, SSD_GROUPS, SSD_STATE)
    dt = jax.nn.softplus(dt.astype(F32) + dt_bias.astype(F32)).reshape(bn, tl, SSD_GROUPS, SSD_HPG)
    a = -jnp.exp(a_log.astype(F32)).reshape(SSD_GROUPS, SSD_HPG)
    h0g = h0.astype(F32).reshape(bn, SSD_GROUPS, SSD_HPG, SSD_HEADDIM, SSD_STATE)
    y, h_last = ssd_scan(x * dt[..., None], dt * a, bm, cm, h0g)
    y = y + x * d_skip.astype(F32).reshape(SSD_GROUPS, SSD_HPG)[..., None]
    yg = (y.reshape(bn, tl, SSD_INNER) * jax.nn.silu(z.astype(F32))).reshape(bn, tl, SSD_GROUPS, SSD_INNER // SSD_GROUPS)
    yg = yg * lax.rsqrt(jnp.mean(yg * yg, axis=-1, keepdims=True) + EPS)
    y = yg.reshape(bn, tl, SSD_INNER) * norm_g.astype(F32)
    h_last = h_last.reshape(bn, SSD_HEADS, SSD_HEADDIM, SSD_STATE)
    return y.astype(u.dtype) @ w_out, conv_new, h_last.astype(u.dtype)


def s5_mixer(u, w_in, lam_re, lam_im, b_re, b_im, c_re, c_im, d_skip, log_dt, glu_w, glu_b, w_out, h0_re, h0_im):
    bn, tl, _ = u.shape
    x, z = jnp.split(u @ w_in, 2, axis=-1)
    xg = x.astype(F32).reshape(bn, tl, S5_GROUPS, S5_GROUP)
    lam = lax.complex(lam_re.astype(F32), lam_im.astype(F32))
    dt = jnp.exp(log_dt.astype(F32))[:, None]
    a_bar = jnp.exp(lam * dt)
    b_bar = ((a_bar - 1.0) / lam)[..., None] * lax.complex(b_re.astype(F32), b_im.astype(F32))
    bu = jnp.einsum('gpc,btgc->btgp', b_bar, xg.astype(jnp.complex64))
    h0 = lax.complex(h0_re.astype(F32), h0_im.astype(F32))
    h, h_last = affine_scan(a_bar[None, None], bu, h0)
    cmat = lax.complex(c_re.astype(F32), c_im.astype(F32))
    s = jnp.einsum('gcp,btgp->btgc', cmat, h).real + d_skip.astype(F32).reshape(S5_GROUPS, S5_GROUP) * xg
    g = jax.nn.gelu(s.reshape(bn, tl, S5_WIDTH))
    o = g * jax.nn.sigmoid(g @ glu_w.astype(F32) + glu_b.astype(F32))
    y = o.astype(u.dtype) * jax.nn.silu(z)
    return y @ w_out, h_last.real.astype(u.dtype), h_last.imag.astype(u.dtype)


def mem_kv(mem, norm_g, w_kv):
    bn, m, _ = mem.shape
    k, v = jnp.split(rmsnorm(mem, norm_g) @ w_kv, 2, axis=-1)
    return k.reshape(bn, m, XA_HEADS, XA_HD), v.reshape(bn, m, XA_HEADS, XA_HD)


def cross_attend(u, k, v, w_q, w_o):
    bn, tl, _ = u.shape
    q = (u @ w_q).reshape(bn, tl, XA_HEADS, XA_HD)
    s = jnp.einsum('bthd,bmhd->bhtm', q.astype(F32), k.astype(F32)) * XA_HD ** -0.5
    p = jax.nn.softmax(s, axis=-1)
    o = jnp.einsum('bhtm,bmhd->bthd', p, v.astype(F32)).reshape(bn, tl, D_MODEL)
    return o.astype(u.dtype) @ w_o


def setup_inputs(seed: int = 0) -> dict:
    key = jax.random.key(seed)
    keys = jax.random.split(key, 96)
    cnt = [0]

    def nk():
        cnt[0] += 1
        return keys[cnt[0] - 1]

    def nrm(shape, scale=1.0):
        return scale * jax.random.normal(nk(), shape, F32)

    def unif(shape, lo, hi):
        return jax.random.uniform(nk(), shape, F32, lo, hi)

    def gain(shape):
        return 1.0 + nrm(shape, 0.01)

    d = D_MODEL
    lru_u = unif((N_A, LRU_WIDTH), 0.9, 0.999)
    lru_s = lru_u ** (1.0 / LRU_C)
    ssd_dt = jnp.exp(unif((N_C, SSD_HEADS), math.log(1e-3), math.log(1e-1)))
    inp = {
        "x_prompt": nrm((BATCH, SEQ, d)),
        "x_sample": nrm((DEC_BATCH, DEC_SEQ, d)),
        "state_lru_conv": nrm((N_A, DEC_BATCH, CONV_K - 1, LRU_WIDTH)),
        "state_lru_h": nrm((N_A, DEC_BATCH, LRU_WIDTH), 0.5),
        "state_ret": nrm((N_B, DEC_BATCH, RET_HEADS, RET_DK, RET_DV)),
        "state_ssd_conv": nrm((N_C, DEC_BATCH, CONV_K - 1, SSD_CONV_DIM)),
        "state_ssd": nrm((N_C, DEC_BATCH, SSD_HEADS, SSD_HEADDIM, SSD_STATE), 0.1),
        "state_s5_re": nrm((N_D, DEC_BATCH, S5_GROUPS, S5_STATE), 0.1),
        "state_s5_im": nrm((N_D, DEC_BATCH, S5_GROUPS, S5_STATE), 0.1),
        "cache_mem_k": nrm((DEPTH, DEC_BATCH, MEM_LEN, XA_HEADS, XA_HD)),
        "cache_mem_v": nrm((DEPTH, DEC_BATCH, MEM_LEN, XA_HEADS, XA_HD)),
        "mem_prompt": nrm((BATCH, MEM_LEN, d)),
        "mix_norm": gain((DEPTH, d)),
        "xa_norm": gain((DEPTH, d)),
        "xa_mem_norm": gain((DEPTH, d)),
        "xa_wq": nrm((DEPTH, d, d), d ** -0.5),
        "xa_wkv": nrm((DEPTH, d, 2 * d), d ** -0.5),
        "xa_wo": nrm((DEPTH, d, d), d ** -0.5),
        "final_norm": gain((d,)),
        "lru_w_in": nrm((N_A, d, 2 * LRU_WIDTH), d ** -0.5),
        "lru_conv_w": nrm((N_A, CONV_K, LRU_WIDTH), CONV_K ** -0.5),
        "lru_conv_b": nrm((N_A, LRU_WIDTH), 0.01),
        "lru_wa": nrm((N_A, LRU_BLOCKS, LRU_BLOCK, LRU_BLOCK), LRU_BLOCK ** -0.5),
        "lru_ba": nrm((N_A, LRU_WIDTH), 0.01),
        "lru_wx": nrm((N_A, LRU_BLOCKS, LRU_BLOCK, LRU_BLOCK), LRU_BLOCK ** -0.5),
        "lru_bx": nrm((N_A, LRU_WIDTH), 0.01),
        "lru_lambda": jnp.log(lru_s) - jnp.log1p(-lru_s),
        "lru_w_out": nrm((N_A, LRU_WIDTH, d), LRU_WIDTH ** -0.5),
        "ret_w_in": nrm((N_B, d, 2 * RET_HEADS * RET_DK + 2 * RET_VW), d ** -0.5),
        "ret_gn": gain((N_B, RET_VW)),
        "ret_w_out": nrm((N_B, RET_VW, d), RET_VW ** -0.5),
        "ssd_w_in": nrm((N_C, d, SSD_INNER + SSD_CONV_DIM + SSD_HEADS), d ** -0.5),
        "ssd_conv_w": nrm((N_C, CONV_K, SSD_CONV_DIM), CONV_K ** -0.5),
        "ssd_conv_b": nrm((N_C, SSD_CONV_DIM), 0.01),
        "ssd_dt_bias": ssd_dt + jnp.log(-jnp.expm1(-ssd_dt)),
        "ssd_a_log": jnp.log(unif((N_C, SSD_HEADS), 1.0, 16.0)),
        "ssd_d": 1.0 + nrm((N_C, SSD_HEADS), 0.1),
        "ssd_norm": gain((N_C, SSD_INNER)),
        "ssd_w_out": nrm((N_C, SSD_INNER, d), SSD_INNER ** -0.5),
        "s5_w_in": nrm((N_D, d, 2 * S5_WIDTH), d ** -0.5),
        "s5_lambda_re": -0.5 + nrm((N_D, S5_GROUPS, S5_STATE), 0.01),
        "s5_lambda_im": jnp.pi * jnp.arange(S5_STATE, dtype=F32) + nrm((N_D, S5_GROUPS, S5_STATE), 0.01),
        "s5_b_re": nrm((N_D, S5_GROUPS, S5_STATE, S5_GROUP), (2 * S5_GROUP) ** -0.5),
        "s5_b_im": nrm((N_D, S5_GROUPS, S5_STATE, S5_GROUP), (2 * S5_GROUP) ** -0.5),
        "s5_c_re": nrm((N_D, S5_GROUPS, S5_GROUP, S5_STATE), S5_STATE ** -0.5),
        "s5_c_im": nrm((N_D, S5_GROUPS, S5_GROUP, S5_STATE), S5_STATE ** -0.5),
        "s5_d": nrm((N_D, S5_WIDTH)),
        "s5_log_dt": unif((N_D, S5_GROUPS), math.log(1e-3), math.log(1e-1)),
        "s5_glu_w": nrm((N_D, S5_WIDTH, S5_WIDTH), S5_WIDTH ** -0.5),
        "s5_glu_b": nrm((N_D, S5_WIDTH), 0.01),
        "s5_w_out": nrm((N_D, S5_WIDTH, d), S5_WIDTH ** -0.5),
    }
    return inp


def reference(x_prompt, x_sample, state_lru_conv, state_lru_h, state_ret, state_ssd_conv, state_ssd,
              state_s5_re, state_s5_im, cache_mem_k, cache_mem_v, mem_prompt,
              mix_norm, xa_norm, xa_mem_norm, xa_wq, xa_wkv, xa_wo, final_norm,
              lru_w_in, lru_conv_w, lru_conv_b, lru_wa, lru_ba, lru_wx, lru_bx, lru_lambda, lru_w_out,
              ret_w_in, ret_gn, ret_w_out,
              ssd_w_in, ssd_conv_w, ssd_conv_b, ssd_dt_bias, ssd_a_log, ssd_d, ssd_norm, ssd_w_out,
              s5_w_in, s5_lambda_re, s5_lambda_im, s5_b_re, s5_b_im, s5_c_re, s5_c_im, s5_d, s5_log_dt,
              s5_glu_w, s5_glu_b, s5_w_out):
    bp, tp, _ = x_prompt.shape
    ts = x_sample.shape[1]
    dt_ = x_prompt.dtype
    pos_p = jnp.arange(tp)
    pos_s = PAST_LEN + jnp.arange(ts)
    hp, hs = x_prompt, x_sample
    lru_conv_p, lru_conv_s, lru_h_p, lru_h_s = [], [], [], []
    ret_p, ret_s = [], []
    ssd_conv_p, ssd_conv_s, ssd_p, ssd_s = [], [], [], []
    s5_re_p, s5_re_s, s5_im_p, s5_im_s = [], [], [], []
    mem_k_p, mem_v_p = [], []
    for i in range(DEPTH):
        kind, j = i % N_MIXERS, i // N_MIXERS
        up = rmsnorm(hp, mix_norm[i])
        us = rmsnorm(hs, mix_norm[i])
        if kind == 0:
            w = (lru_w_in[j], lru_conv_w[j], lru_conv_b[j], lru_wa[j], lru_ba[j], lru_wx[j], lru_bx[j],
                 lru_lambda[j], lru_w_out[j])
            dp, cp, hl_p = rglru_mixer(up, *w, jnp.zeros((bp, CONV_K - 1, LRU_WIDTH), dt_),
                                       jnp.zeros((bp, LRU_WIDTH), dt_))
            ds, cs, hl_s = rglru_mixer(us, *w, state_lru_conv[j], state_lru_h[j])
            lru_conv_p.append(cp); lru_conv_s.append(cs); lru_h_p.append(hl_p); lru_h_s.append(hl_s)
        elif kind == 1:
            w = (ret_w_in[j], ret_gn[j], ret_w_out[j])
            dp, sp = retention_mixer(up, pos_p, *w, jnp.zeros((bp, RET_HEADS, RET_DK, RET_DV), dt_))
            ds, ss = retention_mixer(us, pos_s, *w, state_ret[j])
            ret_p.append(sp); ret_s.append(ss)
        elif kind == 2:
            w = (ssd_w_in[j], ssd_conv_w[j], ssd_conv_b[j], ssd_dt_bias[j], ssd_a_log[j], ssd_d[j],
                 ssd_norm[j], ssd_w_out[j])
            dp, cp, sp = ssd_mixer(up, *w, jnp.zeros((bp, CONV_K - 1, SSD_CONV_DIM), dt_),
                                   jnp.zeros((bp, SSD_HEADS, SSD_HEADDIM, SSD_STATE), dt_))
            ds, cs, ss = ssd_mixer(us, *w, state_ssd_conv[j], state_ssd[j])
            ssd_conv_p.append(cp); ssd_conv_s.append(cs); ssd_p.append(sp); ssd_s.append(ss)
        else:
            w = (s5_w_in[j], s5_lambda_re[j], s5_lambda_im[j], s5_b_re[j], s5_b_im[j], s5_c_re[j], s5_c_im[j],
                 s5_d[j], s5_log_dt[j], s5_glu_w[j], s5_glu_b[j], s5_w_out[j])
            z0 = jnp.zeros((bp, S5_GROUPS, S5_STATE), dt_)
            dp, rp, ip = s5_mixer(up, *w, z0, z0)
            ds, rs, is_ = s5_mixer(us, *w, state_s5_re[j], state_s5_im[j])
            s5_re_p.append(rp); s5_re_s.append(rs); s5_im_p.append(ip); s5_im_s.append(is_)
        hp = hp + dp
        hs = hs + ds
        kp, vp = mem_kv(mem_prompt, xa_mem_norm[i], xa_wkv[i])
        hp = hp + cross_attend(rmsnorm(hp, xa_norm[i]), kp, vp, xa_wq[i], xa_wo[i])
        hs = hs + cross_attend(rmsnorm(hs, xa_norm[i]), cache_mem_k[i], cache_mem_v[i], xa_wq[i], xa_wo[i])
        mem_k_p.append(kp); mem_v_p.append(vp)
    y_prompt = rmsnorm(hp, final_norm)
    y_sample = rmsnorm(hs, final_norm)
    return (y_prompt, y_sample,
            jnp.stack(lru_conv_p), jnp.stack(lru_conv_s), jnp.stack(lru_h_p), jnp.stack(lru_h_s),
            jnp.stack(ret_p), jnp.stack(ret_s),
            jnp.stack(ssd_conv_p), jnp.stack(ssd_conv_s), jnp.stack(ssd_p), jnp.stack(ssd_s),
            jnp.stack(s5_re_p), jnp.stack(s5_re_s), jnp.stack(s5_im_p), jnp.stack(s5_im_s),
            jnp.stack(mem_k_p), jnp.stack(mem_v_p))
```

```python
import functools
import math

import jax
import jax.numpy as jnp
from jax import lax
from jax.experimental import pallas as pl
from jax.experimental.pallas import tpu as pltpu

F32 = jnp.float32
BF16 = jnp.bfloat16

D_MODEL = 2048
PAST_LEN = 16384
EPS = 1e-6
CONV_K = 4
CHUNK = 128
LRU_WIDTH = D_MODEL
LRU_BLOCK = 256
LRU_C = 8.0
RET_HEADS = 8
RET_DK = 256
RET_DV = 512
RET_VW = RET_HEADS * RET_DV
ROPE_BASE = 10000.0
SSD_INNER = 2 * D_MODEL
SSD_HEADDIM = 64
SSD_HEADS = 64
SSD_GROUPS = 8
SSD_HPG = 8
SSD_STATE = 128
SSD_GW = SSD_HPG * SSD_HEADDIM
SSD_CONV_DIM = SSD_INNER + 2 * SSD_GROUPS * SSD_STATE
S5_GROUP = 16
S5_GROUPS = 128
S5_STATE = 64
S5_G8 = 8
S5_NBLK = S5_GROUPS // S5_G8
S5_HW = S5_G8 * S5_STATE
MEM_LEN = 256
XA_HEADS = 4
XA_HD = 512
SAMPLE_T = 4
SAMPLE_NB = 4

VMEM_LIMIT_BYTES = 56 * 1024 * 1024
NT_DIMS = (((1,), (1,)), ((), ()))
TN_DIMS = (((0,), (0,)), ((), ()))


def _cparams(n_axes):
    return pltpu.CompilerParams(dimension_semantics=("arbitrary",) * n_axes,
                                vmem_limit_bytes=VMEM_LIMIT_BYTES)


def _dot(a, b):
    return jnp.dot(a, b, preferred_element_type=F32)


def _dot_nt(a, b):
    return lax.dot_general(a, b, NT_DIMS, preferred_element_type=F32)


def _dot_tn(a, b):
    return lax.dot_general(a, b, TN_DIMS, preferred_element_type=F32)


def _largest_divisor(n, cap, mult):
    for d in range(min(cap, n) // mult * mult, 0, -mult):
        if n % d == 0:
            return d
    raise ValueError(f"no divisor of {n} that is a multiple of {mult}")


def _expm1(x):
    return jnp.where(jnp.abs(x) < 0.5, jnp.tanh(0.5 * x) * (jnp.exp(x) + 1.0), jnp.exp(x) - 1.0)


def _expand_cols(v, width):
    rows, n = v.shape
    return jnp.concatenate([jnp.broadcast_to(v[:, r:r + 1], (rows, width)) for r in range(n)], axis=1)


def _mm_body(*refs, has_norm, has_res, stage_x, tm, row_chunk):
    it = iter(refs)
    x_ref, w_ref = next(it), next(it)
    g_ref = next(it) if has_norm else None
    r_ref = next(it) if has_res else None
    o_ref = next(it)
    if stage_x:
        xb_ref = next(it)

        @pl.when(pl.program_id(1) == 0)
        def _():
            def body(i, carry):
                r0 = pl.multiple_of(i * row_chunk, 16)
                x = x_ref[pl.ds(r0, row_chunk), :].astype(F32)
                if has_norm:
                    x = x * lax.rsqrt(jnp.mean(x * x, axis=-1, keepdims=True) + EPS) * g_ref[...]
                xb_ref[pl.ds(r0, row_chunk), :] = x.astype(BF16)
                return carry
            lax.fori_loop(0, tm // row_chunk, body, 0)
        xb = xb_ref[...]
    else:
        xb = x_ref[...]
    acc = _dot(xb, w_ref[...].astype(BF16))
    if has_res:
        acc = acc + r_ref[...]
    o_ref[...] = acc.astype(o_ref.dtype)


def _matmul(x, w, w_idx, *, norm_g=None, g_idx=0, residual=None, out_dtype=F32, tn=512, col_off=0, n_cols=None):
    m, k = x.shape
    n_cols = w.shape[2] if n_cols is None else n_cols
    tn = min(tn, n_cols)
    assert n_cols % tn == 0 and col_off % tn == 0
    tm = _largest_divisor(m, 1088, 16)
    stage_x = x.dtype != BF16 or norm_g is not None
    row_chunk = _largest_divisor(tm, 272, 16)
    cb = col_off // tn
    in_specs = [pl.BlockSpec((tm, k), lambda i, j: (i, 0)),
                pl.BlockSpec((None, k, tn), lambda i, j: (w_idx, 0, j + cb))]
    args = [x, w]
    if norm_g is not None:
        in_specs.append(pl.BlockSpec((None, 1, k), lambda i, j: (g_idx, 0, 0)))
        args.append(norm_g)
    if residual is not None:
        in_specs.append(pl.BlockSpec((tm, tn), lambda i, j: (i, j)))
        args.append(residual)
    body = functools.partial(_mm_body, has_norm=norm_g is not None, has_res=residual is not None,
                             stage_x=stage_x, tm=tm, row_chunk=row_chunk)
    return pl.pallas_call(
        body,
        out_shape=jax.ShapeDtypeStruct((m, n_cols), out_dtype),
        grid=(m // tm, n_cols // tn),
        in_specs=in_specs,
        out_specs=pl.BlockSpec((tm, tn), lambda i, j: (i, j)),
        scratch_shapes=[pltpu.VMEM((tm, k), BF16)] if stage_x else [],
        compiler_params=_cparams(2),
    )(*args)


def _rmsnorm_body(x_ref, g_ref, o_ref):
    x = x_ref[...]
    o_ref[...] = x * lax.rsqrt(jnp.mean(x * x, axis=-1, keepdims=True) + EPS) * g_ref[...]


def _rmsnorm(x, g):
    m, k = x.shape
    tm = _largest_divisor(m, 512, 8)
    return pl.pallas_call(
        _rmsnorm_body,
        out_shape=jax.ShapeDtypeStruct((m, k), F32),
        grid=(m // tm,),
        in_specs=[pl.BlockSpec((tm, k), lambda i: (i, 0)), pl.BlockSpec((1, k), lambda i: (0, 0))],
        out_specs=pl.BlockSpec((tm, k), lambda i: (i, 0)),
        compiler_params=_cparams(1),
    )(x, g.reshape(1, k))


def _softmax_rows(s):
    e = jnp.exp(s - jnp.max(s, axis=-1, keepdims=True))
    return e / jnp.sum(e, axis=-1, keepdims=True)


def _xattn_prompt_body(q_ref, k_ref, v_ref, o_ref):
    kb = k_ref[...].astype(BF16)
    vb = v_ref[...].astype(BF16)
    for h in range(XA_HEADS):
        sl = slice(h * XA_HD, (h + 1) * XA_HD)
        p = _softmax_rows(_dot_nt(q_ref[:, sl], kb[:, sl]) * XA_HD ** -0.5)
        o_ref[:, sl] = _dot(p.astype(BF16), vb[:, sl]).astype(o_ref.dtype)


def _xattn_prompt(q, kv, bp, tp):
    m = q.shape[0]
    tq = min(tp, 512)
    nq = tp // tq
    return pl.pallas_call(
        _xattn_prompt_body,
        out_shape=jax.ShapeDtypeStruct((m, D_MODEL), BF16),
        grid=(bp, nq),
        in_specs=[pl.BlockSpec((tq, D_MODEL), lambda b, i: (b * nq + i, 0)),
                  pl.BlockSpec((MEM_LEN, D_MODEL), lambda b, i: (b, 0)),
                  pl.BlockSpec((MEM_LEN, D_MODEL), lambda b, i: (b, 1))],
        out_specs=pl.BlockSpec((tq, D_MODEL), lambda b, i: (b * nq + i, 0)),
        compiler_params=_cparams(2),
    )(q, kv, kv)


def _log2(n):
    assert n & (n - 1) == 0, n
    return n.bit_length() - 1


def _row_segment(rows, seg_len, width):
    return lax.broadcasted_iota(jnp.int32, (rows, width), 0) >> _log2(seg_len)


def _row_pos(rows, seg_len, width):
    return lax.broadcasted_iota(jnp.int32, (rows, width), 0) & (seg_len - 1)


def _xattn_sample_body(q_ref, k_ref, v_ref, oin_ref, o_ref):
    del oin_ref
    rows = SAMPLE_NB * SAMPLE_T
    q = q_ref[...]
    seg = _row_segment(rows, SAMPLE_T, XA_HD)
    o = jnp.zeros((rows, XA_HD), F32)
    for i in range(SAMPLE_NB):
        p = _softmax_rows(_dot_nt(q, k_ref[i].astype(BF16)) * XA_HD ** -0.5)
        o = jnp.where(seg == i, _dot(p.astype(BF16), v_ref[i].astype(BF16)), o)
    o_ref[...] = o.astype(o_ref.dtype)


def _xattn_sample(q, cache_k, cache_v, layer, o_all, rows_p, bs):
    rows = SAMPLE_NB * SAMPLE_T
    rb0 = rows_p // rows
    kv_spec = pl.BlockSpec((None, SAMPLE_NB, MEM_LEN, XA_HD), lambda bb, h: (layer, bb, 0, h))
    return pl.pallas_call(
        _xattn_sample_body,
        out_shape=jax.ShapeDtypeStruct(o_all.shape, o_all.dtype),
        grid=(bs // SAMPLE_NB, XA_HEADS),
        in_specs=[pl.BlockSpec((rows, XA_HD), lambda bb, h: (rb0 + bb, h)), kv_spec, kv_spec,
                  pl.BlockSpec(memory_space=pl.ANY)],
        out_specs=pl.BlockSpec((rows, XA_HD), lambda bb, h: (rb0 + bb, h)),
        input_output_aliases={3: 0},
        compiler_params=_cparams(2),
    )(q, cache_k, cache_v, o_all)


LRU_CW = 512
LRU_TC = 256


def _lru_gate_scan(xc, wa_ref, wx_ref, ba, bx, lam, h0, seg_len):
    rows = xc.shape[0]
    rs, gis = [], []
    for n in range(LRU_CW // LRU_BLOCK):
        xb = xc[:, n * LRU_BLOCK:(n + 1) * LRU_BLOCK].astype(BF16)
        rs.append(_dot(xb, wa_ref[n].astype(BF16)))
        gis.append(_dot(xb, wx_ref[n].astype(BF16)))
    r = jax.nn.sigmoid(jnp.concatenate(rs, axis=1) + ba)
    gi = jax.nn.sigmoid(jnp.concatenate(gis, axis=1) + bx)
    log_a = -LRU_C * r * jax.nn.softplus(-lam)
    a = jnp.exp(log_a)
    b = jnp.sqrt(-_expm1(2.0 * log_a)) * (gi * xc)
    t = _row_pos(rows, seg_len, LRU_CW)
    shift = 1
    while shift < seg_len:
        keep = t >= shift
        a_prev = pltpu.roll(a, shift, 0)
        b_prev = pltpu.roll(b, shift, 0)
        b = jnp.where(keep, a * b_prev + b, b)
        a = jnp.where(keep, a * a_prev, a)
        shift *= 2
    return b + a * h0


def _lru_prompt_body(x_ref, z_ref, cw_ref, cb_ref, wa_ref, wx_ref, ba_ref, bx_ref, lam_ref,
                     y_ref, hl_ref, xext_ref, hc_ref):
    c = pl.program_id(2)

    @pl.when(c == 0)
    def _():
        xext_ref[0:8, :] = jnp.zeros((8, LRU_CW), F32)
        hc_ref[...] = jnp.zeros_like(hc_ref)

    x = x_ref[...]
    xext_ref[8:8 + LRU_TC, :] = x
    xc = cw_ref[3:4, :] * x + cb_ref[...]
    for k in range(CONV_K - 1):
        xc = xc + cw_ref[k:k + 1, :] * xext_ref[5 + k:5 + k + LRU_TC, :]
    xext_ref[0:8, :] = x[LRU_TC - 8:LRU_TC, :]
    h = _lru_gate_scan(xc, wa_ref, wx_ref, ba_ref[...], bx_ref[...], lam_ref[...], hc_ref[0:1, :], LRU_TC)
    hc_ref[0:1, :] = h[LRU_TC - 1:LRU_TC, :]
    y_ref[...] = (h * jax.nn.silu(z_ref[...])).astype(y_ref.dtype)

    @pl.when(c == pl.num_programs(2) - 1)
    def _():
        hl_ref[...] = h[LRU_TC - 1:LRU_TC, :]


def _lru_param_specs(n_grid):
    def cmap(block):
        if n_grid == 3:
            return lambda b, cb, c: block(cb)
        return lambda cb: block(cb)
    return [pl.BlockSpec((None, CONV_K, LRU_CW), cmap(lambda cb: (0, 0, cb))),
            pl.BlockSpec((None, 1, LRU_CW), cmap(lambda cb: (0, 0, cb))),
            pl.BlockSpec((None, LRU_CW // LRU_BLOCK, LRU_BLOCK, LRU_BLOCK), cmap(lambda cb: (0, cb, 0, 0))),
            pl.BlockSpec((None, LRU_CW // LRU_BLOCK, LRU_BLOCK, LRU_BLOCK), cmap(lambda cb: (0, cb, 0, 0))),
            pl.BlockSpec((None, 1, LRU_CW), cmap(lambda cb: (0, 0, cb))),
            pl.BlockSpec((None, 1, LRU_CW), cmap(lambda cb: (0, 0, cb))),
            pl.BlockSpec((None, 1, LRU_CW), cmap(lambda cb: (0, 0, cb)))]


def _lru_params(conv_w, conv_b, wa, wx, ba, bx, lam):
    w = LRU_WIDTH
    return [conv_w, conv_b.reshape(1, 1, w), wa, wx, ba.reshape(1, 1, w), bx.reshape(1, 1, w), lam.reshape(1, 1, w)]


def _lru_prompt(xz, params, bp, tp):
    m = xz.shape[0]
    nc = tp // LRU_TC
    ncb = LRU_WIDTH // LRU_CW
    return pl.pallas_call(
        _lru_prompt_body,
        out_shape=(jax.ShapeDtypeStruct((m, LRU_WIDTH), BF16), jax.ShapeDtypeStruct((bp, 1, LRU_WIDTH), F32)),
        grid=(bp, ncb, nc),
        in_specs=[pl.BlockSpec((LRU_TC, LRU_CW), lambda b, cb, c: (b * nc + c, cb)),
                  pl.BlockSpec((LRU_TC, LRU_CW), lambda b, cb, c: (b * nc + c, ncb + cb))] + _lru_param_specs(3),
        out_specs=(pl.BlockSpec((LRU_TC, LRU_CW), lambda b, cb, c: (b * nc + c, cb)),
                   pl.BlockSpec((None, 1, LRU_CW), lambda b, cb, c: (b, 0, cb))),
        scratch_shapes=[pltpu.VMEM((LRU_TC + 8, LRU_CW), F32), pltpu.VMEM((8, LRU_CW), F32)],
        compiler_params=_cparams(3),
    )(xz, xz, *params)


def _lru_sample_body(x_ref, z_ref, p1_ref, p2_ref, p3_ref, h0_ref, cw_ref, cb_ref, wa_ref, wx_ref,
                     ba_ref, bx_ref, lam_ref, yin_ref, y_ref, h_ref):
    del yin_ref
    x = x_ref[...]
    rows = x.shape[0]
    t = _row_pos(rows, SAMPLE_T, LRU_CW)
    xc = cw_ref[3:4, :] * x + cb_ref[...]
    for k, prev_ref in ((1, p1_ref), (2, p2_ref), (3, p3_ref)):
        xc = xc + cw_ref[3 - k:4 - k, :] * jnp.where(t >= k, pltpu.roll(x, k, 0), prev_ref[...])
    h = _lru_gate_scan(xc, wa_ref, wx_ref, ba_ref[...], bx_ref[...], lam_ref[...], h0_ref[...], SAMPLE_T)
    h_ref[...] = h
    y_ref[...] = (h * jax.nn.silu(z_ref[...])).astype(y_ref.dtype)


def _lru_sample(xz, prevs, h0_rows, params, y_all, rows_p):
    rows_s = prevs[0].shape[0]
    rb0 = rows_p // rows_s
    ncb = LRU_WIDTH // LRU_CW
    small = pl.BlockSpec((rows_s, LRU_CW), lambda cb: (0, cb))
    return pl.pallas_call(
        _lru_sample_body,
        out_shape=(jax.ShapeDtypeStruct(y_all.shape, y_all.dtype), jax.ShapeDtypeStruct((rows_s, LRU_WIDTH), F32)),
        grid=(ncb,),
        in_specs=[pl.BlockSpec((rows_s, LRU_CW), lambda cb: (rb0, cb)),
                  pl.BlockSpec((rows_s, LRU_CW), lambda cb: (rb0, ncb + cb)),
                  small, small, small, small] + _lru_param_specs(1) + [pl.BlockSpec(memory_space=pl.ANY)],
        out_specs=(pl.BlockSpec((rows_s, LRU_CW), lambda cb: (rb0, cb)), small),
        input_output_aliases={13: 0},
        compiler_params=_cparams(1),
    )(xz, xz, *prevs, h0_rows, *params, y_all)


def _conv_prev_rows(buf):
    b, _, c = buf.shape
    out = []
    for k in range(1, CONV_K):
        pad = jnp.zeros((b, SAMPLE_T - k, c), buf.dtype)
        out.append(jnp.concatenate([buf[:, CONV_K - 1 - k:], pad], axis=1).reshape(b * SAMPLE_T, c))
    return out


def _rope(x, cos, sin):
    half = RET_DK // 2
    x1, x2 = x[:, :half], x[:, half:]
    return jnp.concatenate([x1 * cos - x2 * sin, x1 * sin + x2 * cos], axis=1)


def _ret_chunk(q, k, v, z, cos, sin, dmask, qdec, kdec, cdec, gn, states, seg_len):
    rows = q.shape[0]
    qb = _rope(q, cos, sin).astype(BF16)
    kr = _rope(k, cos, sin) * RET_DK ** -0.5
    kb = kr.astype(BF16)
    vb = v.astype(BF16)
    kd = kr * kdec
    o = _dot((_dot_nt(qb, kb) * dmask).astype(BF16), vb)
    new_states = []
    single = len(states) == 1
    seg_v = None if single else _row_segment(rows, seg_len, RET_DV)
    seg_k = None if single else _row_segment(rows, seg_len, RET_DK)
    for i, s in enumerate(states):
        cross = _dot(qb, s.astype(BF16)) * qdec
        kdi = kd
        if not single:
            cross = jnp.where(seg_v == i, cross, 0.0)
            kdi = jnp.where(seg_k == i, kd, 0.0)
        o = o + cross
        new_states.append(s * cdec + _dot_tn(kdi.astype(BF16), vb))
    mu = jnp.mean(o, axis=-1, keepdims=True)
    var = jnp.mean(jnp.square(o - mu), axis=-1, keepdims=True)
    on = (o - mu) * lax.rsqrt(var + EPS) * gn
    return on * jax.nn.silu(z), new_states


def _ret_prompt_body(q_ref, k_ref, v_ref, z_ref, cos_ref, sin_ref, dm_ref, qd_ref, kd_ref, cd_ref, gn_ref,
                     y_ref, sl_ref, s_ref):
    c = pl.program_id(2)

    @pl.when(c == 0)
    def _():
        s_ref[...] = jnp.zeros_like(s_ref)

    y, (s_new,) = _ret_chunk(q_ref[...], k_ref[...], v_ref[...], z_ref[...], cos_ref[...], sin_ref[...],
                             dm_ref[...], qd_ref[...], kd_ref[...], cd_ref[...], gn_ref[...], [s_ref[...]], CHUNK)
    s_ref[...] = s_new
    y_ref[...] = y.astype(y_ref.dtype)

    @pl.when(c == pl.num_programs(2) - 1)
    def _():
        sl_ref[...] = s_new


def _ret_tables(seg_len, nseg, pos):
    rows = seg_len * nseg
    log_g = jnp.log1p(-jnp.exp2(-5.0 - jnp.arange(RET_HEADS, dtype=F32)))[:, None, None]
    t = (jnp.arange(rows) % seg_len).astype(F32)
    seg = jnp.arange(rows) // seg_len
    rel = t[:, None] - t[None, :]
    ok = (rel >= 0) & (seg[:, None] == seg[None, :])
    dmask = jnp.where(ok, jnp.exp(log_g * jnp.where(ok, rel, 0.0)), 0.0)
    qdec = jnp.broadcast_to(jnp.exp(log_g * (t + 1.0)[None, :, None]), (RET_HEADS, rows, RET_DV))
    kdec = jnp.broadcast_to(jnp.exp(log_g * (seg_len - 1.0 - t)[None, :, None]), (RET_HEADS, rows, RET_DK))
    cdec = jnp.broadcast_to(jnp.exp(log_g * seg_len), (RET_HEADS, 1, RET_DV))
    half = RET_DK // 2
    inv = ROPE_BASE ** (-jnp.arange(half, dtype=F32) / half)
    ang = pos.astype(F32)[:, None] * inv
    return jnp.cos(ang), jnp.sin(ang), dmask, qdec, kdec, cdec


def _ret_prompt(qkvz, gn, bp, tp):
    m = qkvz.shape[0]
    nc = tp // CHUNK
    cos, sin, dmask, qdec, kdec, cdec = _ret_tables(CHUNK, 1, jnp.arange(tp))
    kb0 = RET_HEADS
    vb0 = 2 * RET_HEADS * RET_DK // RET_DV
    zb0 = vb0 + RET_HEADS

    def rows(b, h, c):
        return b * nc + c
    return pl.pallas_call(
        _ret_prompt_body,
        out_shape=(jax.ShapeDtypeStruct((m, RET_VW), BF16),
                   jax.ShapeDtypeStruct((bp, RET_HEADS, RET_DK, RET_DV), F32)),
        grid=(bp, RET_HEADS, nc),
        in_specs=[pl.BlockSpec((CHUNK, RET_DK), lambda b, h, c: (rows(b, h, c), h)),
                  pl.BlockSpec((CHUNK, RET_DK), lambda b, h, c: (rows(b, h, c), kb0 + h)),
                  pl.BlockSpec((CHUNK, RET_DV), lambda b, h, c: (rows(b, h, c), vb0 + h)),
                  pl.BlockSpec((CHUNK, RET_DV), lambda b, h, c: (rows(b, h, c), zb0 + h)),
                  pl.BlockSpec((CHUNK, RET_DK // 2), lambda b, h, c: (c, 0)),
                  pl.BlockSpec((CHUNK, RET_DK // 2), lambda b, h, c: (c, 0)),
                  pl.BlockSpec((None, CHUNK, CHUNK), lambda b, h, c: (h, 0, 0)),
                  pl.BlockSpec((None, CHUNK, RET_DV), lambda b, h, c: (h, 0, 0)),
                  pl.BlockSpec((None, CHUNK, RET_DK), lambda b, h, c: (h, 0, 0)),
                  pl.BlockSpec((None, 1, RET_DV), lambda b, h, c: (h, 0, 0)),
                  pl.BlockSpec((1, RET_DV), lambda b, h, c: (0, h))],
        out_specs=(pl.BlockSpec((CHUNK, RET_DV), lambda b, h, c: (rows(b, h, c), h)),
                   pl.BlockSpec((None, None, RET_DK, RET_DV), lambda b, h, c: (b, h, 0, 0))),
        scratch_shapes=[pltpu.VMEM((RET_DK, RET_DV), F32)],
        compiler_params=_cparams(3),
    )(qkvz, qkvz, qkvz, qkvz, cos, sin, dmask, qdec, kdec, cdec, gn.reshape(1, RET_VW))


def _ret_sample_body(q_ref, k_ref, v_ref, z_ref, cos_ref, sin_ref, dm_ref, qd_ref, kd_ref, cd_ref, gn_ref,
                     s0_ref, yin_ref, y_ref, s_ref):
    del yin_ref
    y, s_new = _ret_chunk(q_ref[...], k_ref[...], v_ref[...], z_ref[...], cos_ref[...], sin_ref[...],
                          dm_ref[...], qd_ref[...], kd_ref[...], cd_ref[...], gn_ref[...],
                          [s0_ref[i] for i in range(SAMPLE_NB)], SAMPLE_T)
    for i in range(SAMPLE_NB):
        s_ref[i] = s_new[i]
    y_ref[...] = y.astype(y_ref.dtype)


def _ret_sample(qkvz, gn, state, y_all, rows_p, bs):
    rows = SAMPLE_NB * SAMPLE_T
    rb0 = rows_p // rows
    pos = PAST_LEN + jnp.arange(rows) % SAMPLE_T
    cos, sin, dmask, qdec, kdec, cdec = _ret_tables(SAMPLE_T, SAMPLE_NB, pos)
    kb0 = RET_HEADS
    vb0 = 2 * RET_HEADS * RET_DK // RET_DV
    zb0 = vb0 + RET_HEADS
    st_spec = pl.BlockSpec((None, SAMPLE_NB, None, RET_DK, RET_DV), lambda bb, h: (0, bb, h, 0, 0))
    return pl.pallas_call(
        _ret_sample_body,
        out_shape=(jax.ShapeDtypeStruct(y_all.shape, y_all.dtype), jax.ShapeDtypeStruct(state.shape, F32)),
        grid=(bs // SAMPLE_NB, RET_HEADS),
        in_specs=[pl.BlockSpec((rows, RET_DK), lambda bb, h: (rb0 + bb, h)),
                  pl.BlockSpec((rows, RET_DK), lambda bb, h: (rb0 + bb, kb0 + h)),
                  pl.BlockSpec((rows, RET_DV), lambda bb, h: (rb0 + bb, vb0 + h)),
                  pl.BlockSpec((rows, RET_DV), lambda bb, h: (rb0 + bb, zb0 + h)),
                  pl.BlockSpec((rows, RET_DK // 2), lambda bb, h: (0, 0)),
                  pl.BlockSpec((rows, RET_DK // 2), lambda bb, h: (0, 0)),
                  pl.BlockSpec((None, rows, rows), lambda bb, h: (h, 0, 0)),
                  pl.BlockSpec((None, rows, RET_DV), lambda bb, h: (h, 0, 0)),
                  pl.BlockSpec((None, rows, RET_DK), lambda bb, h: (h, 0, 0)),
                  pl.BlockSpec((None, 1, RET_DV), lambda bb, h: (h, 0, 0)),
                  pl.BlockSpec((1, RET_DV), lambda bb, h: (0, h)),
                  st_spec,
                  pl.BlockSpec(memory_space=pl.ANY)],
        out_specs=(pl.BlockSpec((rows, RET_DV), lambda bb, h: (rb0 + bb, h)), st_spec),
        input_output_aliases={12: 0},
        compiler_params=_cparams(2),
    )(qkvz, qkvz, qkvz, qkvz, cos, sin, dmask, qdec, kdec, cdec, gn.reshape(1, RET_VW), state, y_all)


def _ssd_chunk(xs, bs, cs, z, dt_raw, dt_bias, a_log, d_skip, norm_g, states, seg_len):
    rows = xs.shape[0]
    single = len(states) == 1
    hp = SSD_HPG
    pad = 128
    t8 = _row_pos(rows, seg_len, pad)
    seg8 = _row_segment(rows, seg_len, pad)
    tt = lax.broadcasted_iota(jnp.int32, (rows, rows), 0)
    ss = lax.broadcasted_iota(jnp.int32, (rows, rows), 1)
    causal = (tt >= ss) & ((tt >> _log2(seg_len)) == (ss >> _log2(seg_len)))

    def lane_pad(v):
        return jnp.concatenate([v, jnp.zeros((v.shape[0], pad - hp), F32)], axis=1)

    dt = jax.nn.softplus(lane_pad(dt_raw) + lane_pad(dt_bias))
    da = dt * (-jnp.exp(lane_pad(a_log)))
    cum = da
    shift = 1
    while shift < seg_len:
        cum = cum + jnp.where(t8 >= shift, pltpu.roll(cum, shift, 0), 0.0)
        shift *= 2
    cum_sq = cum if rows == pad else jnp.concatenate([cum, jnp.zeros((pad - rows, pad), F32)], axis=0)
    cum_t = cum_sq.T[0:hp, 0:rows]
    lasts = [cum[(i + 1) * seg_len - 1:(i + 1) * seg_len, :] for i in range(len(states))]
    last_row = lasts[0]
    if not single:
        last_row = jnp.zeros((rows, pad), F32)
        for i, l in enumerate(lasts):
            last_row = jnp.where(seg8 == i, l, last_row)
    to_end = jnp.exp(last_row - cum)
    dt, cum, to_end = dt[:, 0:hp], cum[:, 0:hp], to_end[:, 0:hp]

    csb = cs.astype(BF16)
    bsb = bs.astype(BF16)
    cb = _dot_nt(csb, bsb)
    xdt = xs * _expand_cols(dt, SSD_HEADDIM)
    ycols = []
    for r in range(hp):
        lmat = jnp.where(causal, jnp.exp(cum[:, r:r + 1] - cum_t[r:r + 1, :]), 0.0)
        ycols.append(_dot((cb * lmat).astype(BF16), xdt[:, r * SSD_HEADDIM:(r + 1) * SSD_HEADDIM].astype(BF16)))
    y = jnp.concatenate(ycols, axis=1)
    ecum = _expand_cols(jnp.exp(cum), SSD_HEADDIM)
    xte = xdt * _expand_cols(to_end, SSD_HEADDIM)
    segw = None if single else _row_segment(rows, seg_len, SSD_GW)
    new_states = []
    for i, h in enumerate(states):
        y_off = _dot_nt(csb, h.astype(BF16)) * ecum
        xi = xte
        if not single:
            y_off = jnp.where(segw == i, y_off, 0.0)
            xi = jnp.where(segw == i, xte, 0.0)
        y = y + y_off
        e_last = jnp.exp(lasts[i])
        dec = jnp.concatenate([jnp.broadcast_to(e_last[:, r:r + 1], (SSD_HEADDIM, SSD_STATE)) for r in range(hp)],
                              axis=0)
        new_states.append(h * dec + _dot_tn(xi.astype(BF16), bsb))
    y = y + xs * _expand_cols(d_skip, SSD_HEADDIM)
    yg = y * jax.nn.silu(z)
    yg = yg * lax.rsqrt(jnp.mean(yg * yg, axis=-1, keepdims=True) + EPS)
    return yg * norm_g, new_states


def _ssd_prompt_body(z_ref, x_ref, b_ref, c_ref, dt_ref, wx_ref, wb_ref, wc_ref, bx_ref, bb_ref, bc_ref,
                     dtb_ref, al_ref, ds_ref, ng_ref, y_ref, hl_ref, xe_ref, be_ref, ce_ref, h_ref):
    c = pl.program_id(2)

    @pl.when(c == 0)
    def _():
        xe_ref[0:8, :] = jnp.zeros((8, SSD_GW), F32)
        be_ref[0:8, :] = jnp.zeros((8, SSD_STATE), F32)
        ce_ref[0:8, :] = jnp.zeros((8, SSD_STATE), F32)
        h_ref[...] = jnp.zeros_like(h_ref)

    def conv(raw_ref, ext_ref, w_ref, bias_ref):
        raw = raw_ref[...]
        ext_ref[8:8 + CHUNK, :] = raw
        acc = w_ref[3:4, :] * raw + bias_ref[...]
        for k in range(CONV_K - 1):
            acc = acc + w_ref[k:k + 1, :] * ext_ref[5 + k:5 + k + CHUNK, :]
        ext_ref[0:8, :] = raw[CHUNK - 8:CHUNK, :]
        return jax.nn.silu(acc)

    xs = conv(x_ref, xe_ref, wx_ref, bx_ref)
    bs = conv(b_ref, be_ref, wb_ref, bb_ref)
    cs = conv(c_ref, ce_ref, wc_ref, bc_ref)
    y, (h_new,) = _ssd_chunk(xs, bs, cs, z_ref[...], dt_ref[...], dtb_ref[...], al_ref[...], ds_ref[...],
                             ng_ref[...], [h_ref[...]], CHUNK)
    h_ref[...] = h_new
    y_ref[...] = y.astype(y_ref.dtype)

    @pl.when(c == pl.num_programs(2) - 1)
    def _():
        hl_ref[...] = h_new


def _ssd_col_blocks():
    xb0 = SSD_INNER // SSD_GW
    bb0 = (2 * SSD_INNER) // SSD_STATE
    cb0 = bb0 + SSD_GROUPS
    return xb0, bb0, cb0


def _ssd_param_arrays(conv_w, conv_b, dt_bias, a_log, d_skip, norm_g):
    g, hp = SSD_GROUPS, SSD_HPG
    return [conv_w, conv_w, conv_w, conv_b.reshape(1, 1, -1), conv_b.reshape(1, 1, -1), conv_b.reshape(1, 1, -1),
            dt_bias.reshape(g, 1, hp), a_log.reshape(g, 1, hp), d_skip.reshape(g, 1, hp), norm_g.reshape(1, SSD_INNER)]


def _ssd_param_specs(gmap):
    wxb0 = 0
    wbb0 = SSD_INNER // SSD_STATE
    wcb0 = wbb0 + SSD_GROUPS
    return [pl.BlockSpec((None, CONV_K, SSD_GW), gmap(lambda g: (0, 0, wxb0 + g))),
            pl.BlockSpec((None, CONV_K, SSD_STATE), gmap(lambda g: (0, 0, wbb0 + g))),
            pl.BlockSpec((None, CONV_K, SSD_STATE), gmap(lambda g: (0, 0, wcb0 + g))),
            pl.BlockSpec((None, 1, SSD_GW), gmap(lambda g: (0, 0, wxb0 + g))),
            pl.BlockSpec((None, 1, SSD_STATE), gmap(lambda g: (0, 0, wbb0 + g))),
            pl.BlockSpec((None, 1, SSD_STATE), gmap(lambda g: (0, 0, wcb0 + g))),
            pl.BlockSpec((None, 1, SSD_HPG), gmap(lambda g: (g, 0, 0))),
            pl.BlockSpec((None, 1, SSD_HPG), gmap(lambda g: (g, 0, 0))),
            pl.BlockSpec((None, 1, SSD_HPG), gmap(lambda g: (g, 0, 0))),
            pl.BlockSpec((1, SSD_GW), gmap(lambda g: (0, g)))]


def _ssd_prompt(zx, dtg, params, bp, tp):
    m = zx.shape[0]
    nc = tp // CHUNK
    xb0, bb0, cb0 = _ssd_col_blocks()

    def gmap(f):
        return lambda b, g, c: f(g)
    return pl.pallas_call(
        _ssd_prompt_body,
        out_shape=(jax.ShapeDtypeStruct((m, SSD_INNER), BF16),
                   jax.ShapeDtypeStruct((bp, SSD_GROUPS, SSD_GW, SSD_STATE), F32)),
        grid=(bp, SSD_GROUPS, nc),
        in_specs=[pl.BlockSpec((CHUNK, SSD_GW), lambda b, g, c: (b * nc + c, g)),
                  pl.BlockSpec((CHUNK, SSD_GW), lambda b, g, c: (b * nc + c, xb0 + g)),
                  pl.BlockSpec((CHUNK, SSD_STATE), lambda b, g, c: (b * nc + c, bb0 + g)),
                  pl.BlockSpec((CHUNK, SSD_STATE), lambda b, g, c: (b * nc + c, cb0 + g)),
                  pl.BlockSpec((None, CHUNK, SSD_HPG), lambda b, g, c: (g, b * nc + c, 0))] + _ssd_param_specs(gmap),
        out_specs=(pl.BlockSpec((CHUNK, SSD_GW), lambda b, g, c: (b * nc + c, g)),
                   pl.BlockSpec((None, None, SSD_GW, SSD_STATE), lambda b, g, c: (b, g, 0, 0))),
        scratch_shapes=[pltpu.VMEM((CHUNK + 8, SSD_GW), F32), pltpu.VMEM((CHUNK + 8, SSD_STATE), F32),
                        pltpu.VMEM((CHUNK + 8, SSD_STATE), F32), pltpu.VMEM((SSD_GW, SSD_STATE), F32)],
        compiler_params=_cparams(3),
    )(zx, zx, zx, zx, dtg, *params)


def _ssd_sample_body(z_ref, x_ref, b_ref, c_ref, dt_ref, px_ref, pb_ref, pc_ref, wx_ref, wb_ref, wc_ref,
                     bx_ref, bb_ref, bc_ref, dtb_ref, al_ref, ds_ref, ng_ref, h0_ref, yin_ref, y_ref, h_ref):
    del yin_ref

    def conv(raw_ref, prev_ref, w_ref, bias_ref):
        raw = raw_ref[...]
        rows, width = raw.shape
        t = _row_pos(rows, SAMPLE_T, width)
        acc = w_ref[3:4, :] * raw + bias_ref[...]
        for k in range(1, CONV_K):
            acc = acc + w_ref[3 - k:4 - k, :] * jnp.where(t >= k, pltpu.roll(raw, k, 0), prev_ref[k - 1])
        return jax.nn.silu(acc)

    xs = conv(x_ref, px_ref, wx_ref, bx_ref)
    bs = conv(b_ref, pb_ref, wb_ref, bb_ref)
    cs = conv(c_ref, pc_ref, wc_ref, bc_ref)
    y, h_new = _ssd_chunk(xs, bs, cs, z_ref[...], dt_ref[...], dtb_ref[...], al_ref[...], ds_ref[...],
                          ng_ref[...], [h0_ref[i] for i in range(SAMPLE_NB)], SAMPLE_T)
    for i in range(SAMPLE_NB):
        h_ref[i] = h_new[i]
    y_ref[...] = y.astype(y_ref.dtype)


def _ssd_sample(zx, dtg, prevs, params, state, y_all, rows_p, bs):
    rows = SAMPLE_NB * SAMPLE_T
    rb0 = rows_p // rows
    xb0, bb0, cb0 = _ssd_col_blocks()
    wbb0 = SSD_INNER // SSD_STATE
    wcb0 = wbb0 + SSD_GROUPS

    def gmap(f):
        return lambda bb, g: f(g)
    st_spec = pl.BlockSpec((None, SAMPLE_NB, None, SSD_GW, SSD_STATE), lambda bb, g: (0, bb, g, 0, 0))
    return pl.pallas_call(
        _ssd_sample_body,
        out_shape=(jax.ShapeDtypeStruct(y_all.shape, y_all.dtype), jax.ShapeDtypeStruct(state.shape, F32)),
        grid=(bs // SAMPLE_NB, SSD_GROUPS),
        in_specs=[pl.BlockSpec((rows, SSD_GW), lambda bb, g: (rb0 + bb, g)),
                  pl.BlockSpec((rows, SSD_GW), lambda bb, g: (rb0 + bb, xb0 + g)),
                  pl.BlockSpec((rows, SSD_STATE), lambda bb, g: (rb0 + bb, bb0 + g)),
                  pl.BlockSpec((rows, SSD_STATE), lambda bb, g: (rb0 + bb, cb0 + g)),
                  pl.BlockSpec((None, rows, SSD_HPG), lambda bb, g: (g, rb0 + bb, 0)),
                  pl.BlockSpec((CONV_K - 1, rows, SSD_GW), lambda bb, g: (0, bb, g)),
                  pl.BlockSpec((CONV_K - 1, rows, SSD_STATE), lambda bb, g: (0, bb, wbb0 + g)),
                  pl.BlockSpec((CONV_K - 1, rows, SSD_STATE), lambda bb, g: (0, bb, wcb0 + g))]
                 + _ssd_param_specs(gmap) + [st_spec, pl.BlockSpec(memory_space=pl.ANY)],
        out_specs=(pl.BlockSpec((rows, SSD_GW), lambda bb, g: (rb0 + bb, g)), st_spec),
        input_output_aliases={19: 0},
        compiler_params=_cparams(2),
    )(zx, zx, zx, zx, dtg, prevs, prevs, prevs, *params, state, y_all)


def _split_bf16(x):
    hi = x.astype(BF16)
    lo = (x - hi.astype(F32)).astype(BF16)
    return hi, lo


def _dot3(x_hi, x_lo, w_hi, w_lo):
    return _dot(x_hi, w_hi) + (_dot(x_lo, w_hi) + _dot(x_hi, w_lo))


def _s5_body(*refs, sub, nsub, n_scan, has_h0, has_alias):
    it = iter(refs)
    x_ref = next(it)
    kdh_ref, kdl_ref, wdh_ref, wdl_ref, vth_ref, vtl_ref = (next(it) for _ in range(6))
    are_ref, aim_ref, sre_ref, sim_ref, dsk_ref = (next(it) for _ in range(5))
    h0_ref = next(it) if has_h0 else None
    if has_alias:
        next(it)
    s_ref, hl_ref = next(it), next(it)

    xs = [x_ref[pl.ds(s, nsub, stride=sub), :] for s in range(sub)]
    xsp = [_split_bf16(x) for x in xs]
    cb = None
    for s in range(sub):
        term = _dot3(xsp[s][0], xsp[s][1], wdh_ref[s], wdl_ref[s])
        cb = term if cb is None else cb + term

    def cmul(h, a_re, a_im):
        return h * a_re + pltpu.roll(h, S5_HW, 1) * a_im

    if has_h0:
        hprev = h0_ref[...]
        h = cb + cmul(hprev, are_ref[...], aim_ref[...])
    else:
        h = cb
        j = lax.broadcasted_iota(jnp.int32, h.shape, 0)
        for step in range(n_scan):
            shift = 1 << step
            prev = jnp.where(j >= shift, pltpu.roll(h, shift, 0), 0.0)
            h = h + cmul(prev, sre_ref[step], sim_ref[step])
        hprev = jnp.where(j >= 1, pltpu.roll(h, 1, 0), 0.0)
    hl_ref[...] = h[nsub - 1:nsub, :] if not has_h0 else h
    hp_hi, hp_lo = _split_bf16(hprev)
    for t in range(sub):
        y = _dot3(hp_hi, hp_lo, vth_ref[t], vtl_ref[t])
        for s in range(t + 1):
            y = y + _dot3(xsp[s][0], xsp[s][1], kdh_ref[t - s], kdl_ref[t - s])
        s_ref[pl.ds(t, nsub, stride=sub), :] = y + dsk_ref[...] * xs[t]


def _s5_tables(lam_re, lam_im, b_re, b_im, c_re, c_im, log_dt, sub, n_scan):
    g, p = S5_GROUPS, S5_STATE
    dt = jnp.exp(log_dt)[:, None]

    def apow(d):
        mag = jnp.exp(lam_re * dt * d)
        return mag * jnp.cos(lam_im * dt * d), mag * jnp.sin(lam_im * dt * d)

    a_re, a_im = apow(1.0)
    den = lam_re * lam_re + lam_im * lam_im
    f_re = ((a_re - 1.0) * lam_re + a_im * lam_im) / den
    f_im = (a_im * lam_re - (a_re - 1.0) * lam_im) / den
    bb_re = f_re[..., None] * b_re - f_im[..., None] * b_im
    bb_im = f_re[..., None] * b_im + f_im[..., None] * b_re
    eye = jnp.eye(S5_G8, dtype=F32)

    def blockdiag(m):
        i, j = m.shape[1:]
        m = m.reshape(S5_NBLK, S5_G8, i, j)
        return jnp.einsum('Ggij,gh->Ggihj', m, eye).reshape(S5_NBLK, S5_G8 * i, S5_G8 * j)

    kd, wd, vt = [], [], []
    for d in range(sub):
        p_re, p_im = apow(float(d))
        ab_re = p_re[..., None] * bb_re - p_im[..., None] * bb_im
        ab_im = p_re[..., None] * bb_im + p_im[..., None] * bb_re
        kd.append(blockdiag(jnp.einsum('gcp,gpk->gkc', c_re, ab_re) - jnp.einsum('gcp,gpk->gkc', c_im, ab_im)))
        w_re = blockdiag(jnp.swapaxes(ab_re, 1, 2))
        w_im = blockdiag(jnp.swapaxes(ab_im, 1, 2))
        wd.append(jnp.concatenate([w_re, w_im], axis=2))
        q_re, q_im = apow(float(d + 1))
        m_re = c_re * q_re[:, None, :] - c_im * q_im[:, None, :]
        m_im = c_re * q_im[:, None, :] + c_im * q_re[:, None, :]
        vt.append(jnp.concatenate([blockdiag(jnp.swapaxes(m_re, 1, 2)), -blockdiag(jnp.swapaxes(m_im, 1, 2))], axis=1))
    kd = jnp.stack(kd, axis=1)
    wd = jnp.stack(wd[::-1], axis=1)
    vt = jnp.stack(vt, axis=1)

    def lanes(re, im):
        re = re.reshape(S5_NBLK, 1, S5_HW)
        im = im.reshape(S5_NBLK, 1, S5_HW)
        return jnp.concatenate([re, re], axis=2), jnp.concatenate([-im, im], axis=2)

    are, aim = lanes(*apow(float(sub)))
    steps = [lanes(*apow(float(sub * (1 << k)))) for k in range(max(n_scan, 1))]
    sre = jnp.stack([s[0] for s in steps], axis=1)
    sim = jnp.stack([s[1] for s in steps], axis=1)

    def split(m):
        hi = m.astype(BF16)
        return hi, (m - hi.astype(F32)).astype(BF16)
    return (*split(kd), *split(wd), *split(vt), are, aim, sre, sim)


def _s5_core(xz, tables, d_skip, s_all, h0, *, sub, nsub, n_seq, row_block0, n_scan):
    has_h0 = h0 is not None
    rows = sub * nsub
    ns = tables[8].shape[1]
    hw2 = 2 * S5_HW

    def blk(shape, f):
        return pl.BlockSpec(shape, lambda gb, b: f(gb, b))
    in_specs = [blk((rows, 128), lambda gb, b: (row_block0 + b, gb)),
                blk((None, sub, 128, 128), lambda gb, b: (gb, 0, 0, 0)),
                blk((None, sub, 128, 128), lambda gb, b: (gb, 0, 0, 0)),
                blk((None, sub, 128, hw2), lambda gb, b: (gb, 0, 0, 0)),
                blk((None, sub, 128, hw2), lambda gb, b: (gb, 0, 0, 0)),
                blk((None, sub, hw2, 128), lambda gb, b: (gb, 0, 0, 0)),
                blk((None, sub, hw2, 128), lambda gb, b: (gb, 0, 0, 0)),
                blk((None, 1, hw2), lambda gb, b: (gb, 0, 0)),
                blk((None, 1, hw2), lambda gb, b: (gb, 0, 0)),
                blk((None, ns, 1, hw2), lambda gb, b: (gb, 0, 0, 0)),
                blk((None, ns, 1, hw2), lambda gb, b: (gb, 0, 0, 0)),
                blk((1, 128), lambda gb, b: (0, gb))]
    args = [xz, *tables, d_skip.reshape(1, -1)]
    if has_h0:
        in_specs.append(blk((None, nsub, hw2), lambda gb, b: (gb, 0, 0)))
        args.append(h0)
        hl_shape = (S5_NBLK, nsub, hw2)
        hl_spec = blk((None, nsub, hw2), lambda gb, b: (gb, 0, 0))
    else:
        hl_shape = (S5_NBLK, n_seq, 1, hw2)
        hl_spec = blk((None, None, 1, hw2), lambda gb, b: (gb, b, 0, 0))
    aliases = {}
    if s_all is not None:
        in_specs.append(pl.BlockSpec(memory_space=pl.ANY))
        args.append(s_all)
        aliases = {len(args) - 1: 0}
    body = functools.partial(_s5_body, sub=sub, nsub=nsub, n_scan=n_scan, has_h0=has_h0,
                             has_alias=s_all is not None)
    return pl.pallas_call(
        body,
        out_shape=(jax.ShapeDtypeStruct((xz.shape[0], D_MODEL), F32), jax.ShapeDtypeStruct(hl_shape, F32)),
        grid=(S5_NBLK, n_seq),
        in_specs=in_specs,
        out_specs=(blk((rows, 128), lambda gb, b: (row_block0 + b, gb)), hl_spec),
        input_output_aliases=aliases,
        compiler_params=_cparams(2),
    )(*args)


def _s5_glu_body(s_ref, st_ref, z_ref, w_ref, b_ref, o_ref, gb_ref, *, tm, row_chunk):
    @pl.when(pl.program_id(1) == 0)
    def _():
        def body(i, carry):
            r0 = pl.multiple_of(i * row_chunk, 16)
            gb_ref[pl.ds(r0, row_chunk), :] = jax.nn.gelu(s_ref[pl.ds(r0, row_chunk), :]).astype(BF16)
            return carry
        lax.fori_loop(0, tm // row_chunk, body, 0)
    g = jax.nn.gelu(st_ref[...])
    o = g * jax.nn.sigmoid(_dot(gb_ref[...], w_ref[...].astype(BF16)) + b_ref[...])
    o_ref[...] = (o * jax.nn.silu(z_ref[...])).astype(o_ref.dtype)


def _s5_glu(s_all, xz, glu_w, glu_b):
    m, k = s_all.shape
    tn = 512
    tm = _largest_divisor(m, 1088, 16)
    row_chunk = _largest_divisor(tm, 272, 16)
    zb0 = k // tn
    body = functools.partial(_s5_glu_body, tm=tm, row_chunk=row_chunk)
    return pl.pallas_call(
        body,
        out_shape=jax.ShapeDtypeStruct((m, k), BF16),
        grid=(m // tm, k // tn),
        in_specs=[pl.BlockSpec((tm, k), lambda i, j: (i, 0)),
                  pl.BlockSpec((tm, tn), lambda i, j: (i, j)),
                  pl.BlockSpec((tm, tn), lambda i, j: (i, zb0 + j)),
                  pl.BlockSpec((None, k, tn), lambda i, j: (0, 0, j)),
                  pl.BlockSpec((1, tn), lambda i, j: (0, j))],
        out_specs=pl.BlockSpec((tm, tn), lambda i, j: (i, j)),
        scratch_shapes=[pltpu.VMEM((tm, k), BF16)],
        compiler_params=_cparams(2),
    )(s_all, s_all, xz, glu_w, glu_b.reshape(1, k))


def _lru_layer(u_norm, h, lay, j, dims, state_conv, state_h, w_in, conv_w, conv_b, wa, ba, wx, bx, lam, w_out):
    bp, tp, bs = dims
    rows_p = bp * tp
    xz = _matmul(h, w_in, j, norm_g=u_norm, g_idx=lay)
    params = _lru_params(conv_w[j:j + 1], conv_b[j], wa[j:j + 1], wx[j:j + 1], ba[j], bx[j], lam[j])
    y, hl_p = _lru_prompt(xz, params, bp, tp)
    prevs = _conv_prev_rows(state_conv[j])
    h0_rows = jnp.repeat(state_h[j], SAMPLE_T, axis=0)
    y, h_rows = _lru_sample(xz, prevs, h0_rows, params, y, rows_p)
    x_p = xz[:rows_p, :LRU_WIDTH].reshape(bp, tp, LRU_WIDTH)
    x_s = xz[rows_p:, :LRU_WIDTH].reshape(bs, SAMPLE_T, LRU_WIDTH)
    outs = (x_p[:, tp - (CONV_K - 1):], x_s[:, SAMPLE_T - (CONV_K - 1):],
            hl_p.reshape(bp, LRU_WIDTH), h_rows.reshape(bs, SAMPLE_T, LRU_WIDTH)[:, SAMPLE_T - 1])
    return _matmul(y, w_out, j, residual=h), outs


def _ret_layer(u_norm, h, lay, j, dims, state, w_in, gn, w_out):
    bp, tp, bs = dims
    rows_p = bp * tp
    qkvz = _matmul(h, w_in, j, norm_g=u_norm, g_idx=lay)
    y, s_p = _ret_prompt(qkvz, gn[j], bp, tp)
    y, s_s = _ret_sample(qkvz, gn[j], state[j:j + 1], y, rows_p, bs)
    return _matmul(y, w_out, j, residual=h, tn=256), (s_p, s_s[0])


def _ssd_layer(u_norm, h, lay, j, dims, state_conv, state, w_in, conv_w, conv_b, dt_bias, a_log, d_skip,
               norm_g, w_out):
    bp, tp, bs = dims
    rows_p = bp * tp
    m = h.shape[0]
    n_main = SSD_INNER + SSD_CONV_DIM
    zx = _matmul(h, w_in, j, norm_g=u_norm, g_idx=lay, n_cols=n_main)
    dt = _matmul(h, w_in[j:j + 1, :, n_main:], 0, norm_g=u_norm, g_idx=lay)
    dtg = dt.reshape(m, SSD_GROUPS, SSD_HPG).transpose(1, 0, 2)
    params = _ssd_param_arrays(conv_w[j:j + 1], conv_b[j], dt_bias[j], a_log[j], d_skip[j], norm_g[j])
    y, hl_p = _ssd_prompt(zx, dtg, params, bp, tp)
    prevs = jnp.stack(_conv_prev_rows(state_conv[j]))
    st = state[j:j + 1].reshape(1, bs, SSD_GROUPS, SSD_GW, SSD_STATE)
    y, hl_s = _ssd_sample(zx, dtg, prevs, params, st, y, rows_p, bs)
    xbc_p = zx[:rows_p, SSD_INNER:].reshape(bp, tp, SSD_CONV_DIM)
    xbc_s = zx[rows_p:, SSD_INNER:].reshape(bs, SAMPLE_T, SSD_CONV_DIM)
    shape = (SSD_HEADS, SSD_HEADDIM, SSD_STATE)
    outs = (xbc_p[:, tp - (CONV_K - 1):], xbc_s[:, SAMPLE_T - (CONV_K - 1):],
            hl_p.reshape((bp,) + shape), hl_s.reshape((bs,) + shape))
    return _matmul(y, w_out, j, residual=h, tn=256), outs


S5_SUB = 8


def _s5_layer(u_norm, h, lay, j, dims, h0_re, h0_im, w_in, lam_re, lam_im, b_re, b_im, c_re, c_im, d_skip,
              log_dt, glu_w, glu_b, w_out):
    bp, tp, bs = dims
    rows_p = bp * tp
    m = h.shape[0]
    xz = _matmul(h, w_in, j, norm_g=u_norm, g_idx=lay)
    nsub_p = tp // S5_SUB
    n_scan = max(nsub_p - 1, 0).bit_length()
    par = (lam_re[j], lam_im[j], b_re[j], b_im[j], c_re[j], c_im[j], log_dt[j])
    s_all, hl_p = _s5_core(xz, _s5_tables(*par, S5_SUB, n_scan), d_skip[j], None, None,
                           sub=S5_SUB, nsub=nsub_p, n_seq=bp, row_block0=0, n_scan=n_scan)

    def to_lanes(v):
        return v.reshape(bs, S5_NBLK, S5_HW).transpose(1, 0, 2)
    h0 = jnp.concatenate([to_lanes(h0_re[j]), to_lanes(h0_im[j])], axis=2)
    s_all, hl_s = _s5_core(xz, _s5_tables(*par, SAMPLE_T, 0), d_skip[j], s_all, h0,
                           sub=SAMPLE_T, nsub=bs, n_seq=1, row_block0=rows_p // (bs * SAMPLE_T), n_scan=0)
    y = _s5_glu(s_all, xz, glu_w[j:j + 1], glu_b[j])

    def from_lanes(v, nb):
        return v.transpose(1, 0, 2).reshape(nb, S5_GROUPS, S5_STATE)
    hl_p = hl_p.reshape(S5_NBLK, bp, 2 * S5_HW)
    outs = (from_lanes(hl_p[..., :S5_HW], bp), from_lanes(hl_s[..., :S5_HW], bs),
            from_lanes(hl_p[..., S5_HW:], bp), from_lanes(hl_s[..., S5_HW:], bs))
    return _matmul(y, w_out, j, residual=h), outs


def kernel(x_prompt, x_sample, state_lru_conv, state_lru_h, state_ret, state_ssd_conv, state_ssd, state_s5_re, state_s5_im, cache_mem_k, cache_mem_v, mem_prompt, mix_norm, xa_norm, xa_mem_norm, xa_wq, xa_wkv, xa_wo, final_norm, lru_w_in, lru_conv_w, lru_conv_b, lru_wa, lru_ba, lru_wx, lru_bx, lru_lambda, lru_w_out, ret_w_in, ret_gn, ret_w_out, ssd_w_in, ssd_conv_w, ssd_conv_b, ssd_dt_bias, ssd_a_log, ssd_d, ssd_norm, ssd_w_out, s5_w_in, s5_lambda_re, s5_lambda_im, s5_b_re, s5_b_im, s5_c_re, s5_c_im, s5_d, s5_log_dt, s5_glu_w, s5_glu_b, s5_w_out):
    bp, tp, d = x_prompt.shape
    bs, ts, _ = x_sample.shape
    depth = mix_norm.shape[0]
    assert d == D_MODEL and ts == SAMPLE_T and tp % LRU_TC == 0 and bs % SAMPLE_NB == 0
    rows_p, rows_s = bp * tp, bs * ts
    assert rows_p % rows_s == 0
    dims = (bp, tp, bs)
    h = jnp.concatenate([x_prompt.reshape(rows_p, d), x_sample.reshape(rows_s, d)], axis=0)
    mem = mem_prompt.reshape(bp * MEM_LEN, d)
    cache_k = cache_mem_k.reshape(depth, bs, MEM_LEN, d)
    cache_v = cache_mem_v.reshape(depth, bs, MEM_LEN, d)
    mix_g = mix_norm.reshape(depth, 1, d)
    xa_g = xa_norm.reshape(depth, 1, d)
    mem_g = xa_mem_norm.reshape(depth, 1, d)

    outs = {k: [] for k in ("lru", "ret", "ssd", "s5")}
    mem_k, mem_v = [], []
    for i in range(depth):
        kind, j = i % 4, i // 4
        if kind == 0:
            h, o = _lru_layer(mix_g, h, i, j, dims, state_lru_conv, state_lru_h, lru_w_in,
                              lru_conv_w, lru_conv_b, lru_wa, lru_ba, lru_wx, lru_bx, lru_lambda, lru_w_out)
            outs["lru"].append(o)
        elif kind == 1:
            h, o = _ret_layer(mix_g, h, i, j, dims, state_ret, ret_w_in, ret_gn, ret_w_out)
            outs["ret"].append(o)
        elif kind == 2:
            h, o = _ssd_layer(mix_g, h, i, j, dims, state_ssd_conv, state_ssd, ssd_w_in,
                              ssd_conv_w, ssd_conv_b, ssd_dt_bias, ssd_a_log, ssd_d, ssd_norm, ssd_w_out)
            outs["ssd"].append(o)
        else:
            h, o = _s5_layer(mix_g, h, i, j, dims, state_s5_re, state_s5_im, s5_w_in,
                             s5_lambda_re, s5_lambda_im, s5_b_re, s5_b_im, s5_c_re, s5_c_im, s5_d, s5_log_dt,
                             s5_glu_w, s5_glu_b, s5_w_out)
            outs["s5"].append(o)
        kv = _matmul(mem, xa_wkv, i, norm_g=mem_g, g_idx=i)
        q = _matmul(h, xa_wq, i, norm_g=xa_g, g_idx=i, out_dtype=BF16)
        o_att = _xattn_prompt(q, kv, bp, tp)
        o_att = _xattn_sample(q, cache_k, cache_v, i, o_att, rows_p, bs)
        h = _matmul(o_att, xa_wo, i, residual=h)
        mem_k.append(kv[:, :d].reshape(bp, MEM_LEN, XA_HEADS, XA_HD))
        mem_v.append(kv[:, d:].reshape(bp, MEM_LEN, XA_HEADS, XA_HD))
    y = _rmsnorm(h, final_norm)

    def stack(kind, idx):
        return jnp.stack([o[idx] for o in outs[kind]])
    return (y[:rows_p].reshape(bp, tp, d), y[rows_p:].reshape(bs, ts, d),
            stack("lru", 0), stack("lru", 1), stack("lru", 2), stack("lru", 3),
            stack("ret", 0), stack("ret", 1),
            stack("ssd", 0), stack("ssd", 1), stack("ssd", 2), stack("ssd", 3),
            stack("s5", 0), stack("s5", 1), stack("s5", 2), stack("s5", 3),
            jnp.stack(mem_k), jnp.stack(mem_v))
```

```python
import functools
import math

import jax
import jax.numpy as jnp
from jax import lax
from jax.experimental import pallas as pl
from jax.experimental.pallas import tpu as pltpu

F32 = jnp.float32
BF16 = jnp.bfloat16

D_MODEL = 2048
PAST_LEN = 16384
EPS = 1e-6
CONV_K = 4
CHUNK = 128
LRU_WIDTH = D_MODEL
LRU_BLOCK = 256
LRU_C = 8.0
RET_HEADS = 8
RET_DK = 256
RET_DV = 512
RET_VW = RET_HEADS * RET_DV
ROPE_BASE = 10000.0
SSD_INNER = 2 * D_MODEL
SSD_HEADDIM = 64
SSD_HEADS = 64
SSD_GROUPS = 8
SSD_HPG = 8
SSD_STATE = 128
SSD_GW = SSD_HPG * SSD_HEADDIM
SSD_CONV_DIM = SSD_INNER + 2 * SSD_GROUPS * SSD_STATE
S5_GROUP = 16
S5_GROUPS = 128
S5_STATE = 64
S5_G8 = 8
S5_NBLK = S5_GROUPS // S5_G8
S5_HW = S5_G8 * S5_STATE
MEM_LEN = 256
XA_HEADS = 4
XA_HD = 512
SAMPLE_T = 4
SAMPLE_NB = 4

VMEM_LIMIT_BYTES = 56 * 1024 * 1024
MM_MAX_ROWS = 2176
NT_DIMS = (((1,), (1,)), ((), ()))
TN_DIMS = (((0,), (0,)), ((), ()))


def _cparams(n_axes):
    return pltpu.CompilerParams(dimension_semantics=("arbitrary",) * n_axes,
                                vmem_limit_bytes=VMEM_LIMIT_BYTES)


def _dot(a, b):
    return jnp.dot(a, b, preferred_element_type=F32)


def _dot_nt(a, b):
    return lax.dot_general(a, b, NT_DIMS, preferred_element_type=F32)


def _dot_tn(a, b):
    return lax.dot_general(a, b, TN_DIMS, preferred_element_type=F32)


def _largest_divisor(n, cap, mult):
    for d in range(min(cap, n) // mult * mult, 0, -mult):
        if n % d == 0:
            return d
    raise ValueError(f"no divisor of {n} that is a multiple of {mult}")


def _expm1(x):
    return jnp.where(jnp.abs(x) < 0.5, jnp.tanh(0.5 * x) * (jnp.exp(x) + 1.0), jnp.exp(x) - 1.0)


def _expand_cols(v, width):
    rows, n = v.shape
    return jnp.concatenate([jnp.broadcast_to(v[:, r:r + 1], (rows, width)) for r in range(n)], axis=1)


def _mm_body(*refs, has_norm, has_res, stage_x, tm, row_chunk):
    it = iter(refs)
    x_ref, w_ref = next(it), next(it)
    g_ref = next(it) if has_norm else None
    r_ref = next(it) if has_res else None
    o_ref = next(it)
    if stage_x:
        xb_ref = next(it)

        @pl.when(pl.program_id(1) == 0)
        def _():
            def body(i, carry):
                r0 = pl.multiple_of(i * row_chunk, 16)
                x = x_ref[pl.ds(r0, row_chunk), :].astype(F32)
                if has_norm:
                    x = x * lax.rsqrt(jnp.mean(x * x, axis=-1, keepdims=True) + EPS) * g_ref[...]
                xb_ref[pl.ds(r0, row_chunk), :] = x.astype(BF16)
                return carry
            lax.fori_loop(0, tm // row_chunk, body, 0)
        xb = xb_ref[...]
    else:
        xb = x_ref[...]
    acc = _dot(xb, w_ref[...].astype(BF16))
    if has_res:
        acc = acc + r_ref[...]
    o_ref[...] = acc.astype(o_ref.dtype)


def _matmul(x, w, w_idx, *, norm_g=None, g_idx=0, residual=None, out_dtype=F32, tn=512, col_off=0, n_cols=None):
    m, k = x.shape
    n_cols = w.shape[2] if n_cols is None else n_cols
    tn = min(tn, n_cols)
    assert n_cols % tn == 0 and col_off % tn == 0
    tm = _largest_divisor(m, MM_MAX_ROWS, 16)
    stage_x = x.dtype != BF16 or norm_g is not None
    row_chunk = _largest_divisor(tm, 272, 16)
    cb = col_off // tn
    in_specs = [pl.BlockSpec((tm, k), lambda i, j: (i, 0), pipeline_mode=pl.Buffered(1)),
                pl.BlockSpec((None, k, tn), lambda i, j: (w_idx, 0, j + cb))]
    args = [x, w]
    if norm_g is not None:
        in_specs.append(pl.BlockSpec((None, 1, k), lambda i, j: (g_idx, 0, 0)))
        args.append(norm_g)
    if residual is not None:
        in_specs.append(pl.BlockSpec((tm, tn), lambda i, j: (i, j)))
        args.append(residual)
    body = functools.partial(_mm_body, has_norm=norm_g is not None, has_res=residual is not None,
                             stage_x=stage_x, tm=tm, row_chunk=row_chunk)
    return pl.pallas_call(
        body,
        out_shape=jax.ShapeDtypeStruct((m, n_cols), out_dtype),
        grid=(m // tm, n_cols // tn),
        in_specs=in_specs,
        out_specs=pl.BlockSpec((tm, tn), lambda i, j: (i, j)),
        scratch_shapes=[pltpu.VMEM((tm, k), BF16)] if stage_x else [],
        compiler_params=_cparams(2),
    )(*args)


def _rmsnorm_body(x_ref, g_ref, o_ref):
    x = x_ref[...]
    o_ref[...] = x * lax.rsqrt(jnp.mean(x * x, axis=-1, keepdims=True) + EPS) * g_ref[...]


def _rmsnorm(x, g, row0, n_rows):
    k = x.shape[1]
    tm = _largest_divisor(math.gcd(n_rows, row0) if row0 else n_rows, 512, 8)
    rb0 = row0 // tm
    return pl.pallas_call(
        _rmsnorm_body,
        out_shape=jax.ShapeDtypeStruct((n_rows, k), F32),
        grid=(n_rows // tm,),
        in_specs=[pl.BlockSpec((tm, k), lambda i: (rb0 + i, 0)), pl.BlockSpec((1, k), lambda i: (0, 0))],
        out_specs=pl.BlockSpec((tm, k), lambda i: (i, 0)),
        compiler_params=_cparams(1),
    )(x, g.reshape(1, k))


def _softmax_rows(s):
    e = jnp.exp(s - jnp.max(s, axis=-1, keepdims=True))
    return e / jnp.sum(e, axis=-1, keepdims=True)


def _xattn_prompt_body(q_ref, k_ref, v_ref, o_ref):
    kb = k_ref[...].astype(BF16)
    vb = v_ref[...].astype(BF16)
    for h in range(XA_HEADS):
        sl = slice(h * XA_HD, (h + 1) * XA_HD)
        p = _softmax_rows(_dot_nt(q_ref[:, sl], kb[:, sl]) * XA_HD ** -0.5)
        o_ref[:, sl] = _dot(p.astype(BF16), vb[:, sl]).astype(o_ref.dtype)


def _xattn_prompt(q, kv, bp, tp):
    m = q.shape[0]
    tq = min(tp, 512)
    nq = tp // tq
    return pl.pallas_call(
        _xattn_prompt_body,
        out_shape=jax.ShapeDtypeStruct((m, D_MODEL), BF16),
        grid=(bp, nq),
        in_specs=[pl.BlockSpec((tq, D_MODEL), lambda b, i: (b * nq + i, 0)),
                  pl.BlockSpec((MEM_LEN, D_MODEL), lambda b, i: (b, 0)),
                  pl.BlockSpec((MEM_LEN, D_MODEL), lambda b, i: (b, 1))],
        out_specs=pl.BlockSpec((tq, D_MODEL), lambda b, i: (b * nq + i, 0)),
        compiler_params=_cparams(2),
    )(q, kv, kv)


def _log2(n):
    assert n & (n - 1) == 0, n
    return n.bit_length() - 1


def _row_segment(rows, seg_len, width):
    return lax.broadcasted_iota(jnp.int32, (rows, width), 0) >> _log2(seg_len)


def _row_pos(rows, seg_len, width):
    return lax.broadcasted_iota(jnp.int32, (rows, width), 0) & (seg_len - 1)


XA_SNB = 2


def _xattn_sample_body(q_ref, k_ref, v_ref, oin_ref, o_ref, acc_ref):
    del oin_ref
    part = pl.program_id(1)
    rows = SAMPLE_NB * SAMPLE_T

    @pl.when(part == 0)
    def _():
        acc_ref[...] = jnp.zeros_like(acc_ref)

    lane_blocks = XA_HD // 128

    def head(ref, i, h):
        return jnp.concatenate([ref[i, pl.ds(j * XA_HEADS + h, MEM_LEN, stride=lane_blocks * XA_HEADS), :]
                                for j in range(lane_blocks)], axis=1).astype(BF16)

    pairs = [(i, h) for i in range(XA_SNB) for h in range(XA_HEADS)]
    s = jnp.concatenate([_dot_nt(q_ref[:, h * XA_HD:(h + 1) * XA_HD], head(k_ref, i, h)) for i, h in pairs], axis=0)
    p = _softmax_rows(s * XA_HD ** -0.5).astype(BF16)
    seg = _row_segment(rows, SAMPLE_T, XA_HD)
    for h in range(XA_HEADS):
        sl = slice(h * XA_HD, (h + 1) * XA_HD)
        o = acc_ref[:, sl]
        for i in range(XA_SNB):
            n = i * XA_HEADS + h
            o = jnp.where(seg == part * XA_SNB + i, _dot(p[n * rows:(n + 1) * rows], head(v_ref, i, h)), o)
        acc_ref[:, sl] = o

    @pl.when(part == pl.num_programs(1) - 1)
    def _():
        o_ref[...] = acc_ref[...].astype(o_ref.dtype)


def _xattn_sample(q, cache_k, cache_v, layer, o_all, rows_p, bs):
    rows = SAMPLE_NB * SAMPLE_T
    rb0 = rows_p // rows
    nparts = SAMPLE_NB // XA_SNB
    depth = cache_k.shape[0]
    lane_blocks = XA_HD // 128
    kv_rows = MEM_LEN * lane_blocks * XA_HEADS

    def relayout(c):
        c = c.reshape(depth, bs, MEM_LEN, XA_HEADS, lane_blocks, 128).transpose(0, 1, 2, 4, 3, 5)
        return c.reshape(depth, bs, kv_rows, 128)
    cache_k, cache_v = relayout(cache_k), relayout(cache_v)
    kv_spec = pl.BlockSpec((None, XA_SNB, kv_rows, 128), lambda bb, s: (layer, bb * nparts + s, 0, 0))
    return pl.pallas_call(
        _xattn_sample_body,
        out_shape=jax.ShapeDtypeStruct(o_all.shape, o_all.dtype),
        grid=(bs // SAMPLE_NB, nparts),
        in_specs=[pl.BlockSpec((rows, D_MODEL), lambda bb, s: (rb0 + bb, 0)), kv_spec, kv_spec,
                  pl.BlockSpec(memory_space=pl.ANY)],
        out_specs=pl.BlockSpec((rows, D_MODEL), lambda bb, s: (rb0 + bb, 0)),
        scratch_shapes=[pltpu.VMEM((rows, D_MODEL), F32)],
        input_output_aliases={3: 0},
        compiler_params=_cparams(2),
    )(q, cache_k, cache_v, o_all)


LRU_CW = 512
LRU_TC = 256


def _lru_gate_scan(xc, wa_ref, wx_ref, ba, bx, lam, h0, seg_len):
    rows = xc.shape[0]
    rs, gis = [], []
    for n in range(LRU_CW // LRU_BLOCK):
        xb = xc[:, n * LRU_BLOCK:(n + 1) * LRU_BLOCK].astype(BF16)
        rs.append(_dot(xb, wa_ref[n].astype(BF16)))
        gis.append(_dot(xb, wx_ref[n].astype(BF16)))
    r = jax.nn.sigmoid(jnp.concatenate(rs, axis=1) + ba)
    gi = jax.nn.sigmoid(jnp.concatenate(gis, axis=1) + bx)
    log_a = -LRU_C * r * jax.nn.softplus(-lam)
    a = jnp.exp(log_a)
    b = jnp.sqrt(-_expm1(2.0 * log_a)) * (gi * xc)
    t = _row_pos(rows, seg_len, LRU_CW)
    shift = 1
    while shift < seg_len:
        keep = t >= shift
        a_prev = pltpu.roll(a, shift, 0)
        b_prev = pltpu.roll(b, shift, 0)
        b = jnp.where(keep, a * b_prev + b, b)
        a = jnp.where(keep, a * a_prev, a)
        shift *= 2
    return b + a * h0


def _lru_prompt_body(x_ref, z_ref, cw_ref, cb_ref, wa_ref, wx_ref, ba_ref, bx_ref, lam_ref,
                     y_ref, hl_ref, xext_ref, hc_ref):
    c = pl.program_id(2)

    @pl.when(c == 0)
    def _():
        xext_ref[0:8, :] = jnp.zeros((8, LRU_CW), F32)
        hc_ref[...] = jnp.zeros_like(hc_ref)

    x = x_ref[...]
    xext_ref[8:8 + LRU_TC, :] = x
    xc = cw_ref[3:4, :] * x + cb_ref[...]
    for k in range(CONV_K - 1):
        xc = xc + cw_ref[k:k + 1, :] * xext_ref[5 + k:5 + k + LRU_TC, :]
    xext_ref[0:8, :] = x[LRU_TC - 8:LRU_TC, :]
    h = _lru_gate_scan(xc, wa_ref, wx_ref, ba_ref[...], bx_ref[...], lam_ref[...], hc_ref[0:1, :], LRU_TC)
    hc_ref[0:1, :] = h[LRU_TC - 1:LRU_TC, :]
    y_ref[...] = (h * jax.nn.silu(z_ref[...])).astype(y_ref.dtype)

    @pl.when(c == pl.num_programs(2) - 1)
    def _():
        hl_ref[...] = h[LRU_TC - 1:LRU_TC, :]


def _lru_param_specs(n_grid):
    def cmap(block):
        if n_grid == 3:
            return lambda b, cb, c: block(cb)
        return lambda cb: block(cb)
    return [pl.BlockSpec((None, CONV_K, LRU_CW), cmap(lambda cb: (0, 0, cb))),
            pl.BlockSpec((None, 1, LRU_CW), cmap(lambda cb: (0, 0, cb))),
            pl.BlockSpec((None, LRU_CW // LRU_BLOCK, LRU_BLOCK, LRU_BLOCK), cmap(lambda cb: (0, cb, 0, 0))),
            pl.BlockSpec((None, LRU_CW // LRU_BLOCK, LRU_BLOCK, LRU_BLOCK), cmap(lambda cb: (0, cb, 0, 0))),
            pl.BlockSpec((None, 1, LRU_CW), cmap(lambda cb: (0, 0, cb))),
            pl.BlockSpec((None, 1, LRU_CW), cmap(lambda cb: (0, 0, cb))),
            pl.BlockSpec((None, 1, LRU_CW), cmap(lambda cb: (0, 0, cb)))]


def _lru_params(conv_w, conv_b, wa, wx, ba, bx, lam):
    w = LRU_WIDTH
    return [conv_w, conv_b.reshape(1, 1, w), wa, wx, ba.reshape(1, 1, w), bx.reshape(1, 1, w), lam.reshape(1, 1, w)]


def _lru_prompt(xz, params, bp, tp):
    m = xz.shape[0]
    nc = tp // LRU_TC
    ncb = LRU_WIDTH // LRU_CW
    return pl.pallas_call(
        _lru_prompt_body,
        out_shape=(jax.ShapeDtypeStruct((m, LRU_WIDTH), BF16), jax.ShapeDtypeStruct((bp, 1, LRU_WIDTH), F32)),
        grid=(bp, ncb, nc),
        in_specs=[pl.BlockSpec((LRU_TC, LRU_CW), lambda b, cb, c: (b * nc + c, cb)),
                  pl.BlockSpec((LRU_TC, LRU_CW), lambda b, cb, c: (b * nc + c, ncb + cb))] + _lru_param_specs(3),
        out_specs=(pl.BlockSpec((LRU_TC, LRU_CW), lambda b, cb, c: (b * nc + c, cb)),
                   pl.BlockSpec((None, 1, LRU_CW), lambda b, cb, c: (b, 0, cb))),
        scratch_shapes=[pltpu.VMEM((LRU_TC + 8, LRU_CW), F32), pltpu.VMEM((8, LRU_CW), F32)],
        compiler_params=_cparams(3),
    )(xz, xz, *params)


def _lru_sample_body(x_ref, z_ref, p1_ref, p2_ref, p3_ref, h0_ref, cw_ref, cb_ref, wa_ref, wx_ref,
                     ba_ref, bx_ref, lam_ref, yin_ref, y_ref, h_ref):
    del yin_ref
    x = x_ref[...]
    rows = x.shape[0]
    t = _row_pos(rows, SAMPLE_T, LRU_CW)
    xc = cw_ref[3:4, :] * x + cb_ref[...]
    for k, prev_ref in ((1, p1_ref), (2, p2_ref), (3, p3_ref)):
        xc = xc + cw_ref[3 - k:4 - k, :] * jnp.where(t >= k, pltpu.roll(x, k, 0), prev_ref[...])
    h = _lru_gate_scan(xc, wa_ref, wx_ref, ba_ref[...], bx_ref[...], lam_ref[...], h0_ref[...], SAMPLE_T)
    h_ref[...] = h
    y_ref[...] = (h * jax.nn.silu(z_ref[...])).astype(y_ref.dtype)


def _lru_sample(xz, prevs, h0_rows, params, y_all, rows_p):
    rows_s = prevs[0].shape[0]
    rb0 = rows_p // rows_s
    ncb = LRU_WIDTH // LRU_CW
    small = pl.BlockSpec((rows_s, LRU_CW), lambda cb: (0, cb))
    return pl.pallas_call(
        _lru_sample_body,
        out_shape=(jax.ShapeDtypeStruct(y_all.shape, y_all.dtype), jax.ShapeDtypeStruct((rows_s, LRU_WIDTH), F32)),
        grid=(ncb,),
        in_specs=[pl.BlockSpec((rows_s, LRU_CW), lambda cb: (rb0, cb)),
                  pl.BlockSpec((rows_s, LRU_CW), lambda cb: (rb0, ncb + cb)),
                  small, small, small, small] + _lru_param_specs(1) + [pl.BlockSpec(memory_space=pl.ANY)],
        out_specs=(pl.BlockSpec((rows_s, LRU_CW), lambda cb: (rb0, cb)), small),
        input_output_aliases={13: 0},
        compiler_params=_cparams(1),
    )(xz, xz, *prevs, h0_rows, *params, y_all)


def _conv_prev_rows(buf):
    b, _, c = buf.shape
    out = []
    for k in range(1, CONV_K):
        pad = jnp.zeros((b, SAMPLE_T - k, c), buf.dtype)
        out.append(jnp.concatenate([buf[:, CONV_K - 1 - k:], pad], axis=1).reshape(b * SAMPLE_T, c))
    return out


def _rope(x, cos, sin):
    half = RET_DK // 2
    x1, x2 = x[:, :half], x[:, half:]
    return jnp.concatenate([x1 * cos - x2 * sin, x1 * sin + x2 * cos], axis=1)


def _ret_chunk(q, k, v, z, cos, sin, dmask, qdec, kdec, cdec, gn, states, seg_len):
    rows = q.shape[0]
    qb = _rope(q, cos, sin).astype(BF16)
    kr = _rope(k, cos, sin) * RET_DK ** -0.5
    kb = kr.astype(BF16)
    vb = v.astype(BF16)
    kd = kr * kdec
    o = _dot((_dot_nt(qb, kb) * dmask).astype(BF16), vb)
    new_states = []
    single = len(states) == 1
    seg_v = None if single else _row_segment(rows, seg_len, RET_DV)
    seg_k = None if single else _row_segment(rows, seg_len, RET_DK)
    for i, s in enumerate(states):
        cross = _dot(qb, s.astype(BF16)) * qdec
        kdi = kd
        if not single:
            cross = jnp.where(seg_v == i, cross, 0.0)
            kdi = jnp.where(seg_k == i, kd, 0.0)
        o = o + cross
        new_states.append(s * cdec + _dot_tn(kdi.astype(BF16), vb))
    mu = jnp.mean(o, axis=-1, keepdims=True)
    var = jnp.mean(jnp.square(o - mu), axis=-1, keepdims=True)
    on = (o - mu) * lax.rsqrt(var + EPS) * gn
    return on * jax.nn.silu(z), new_states


def _ret_prompt_body(q_ref, k_ref, v_ref, z_ref, cos_ref, sin_ref, dm_ref, qd_ref, kd_ref, cd_ref, gn_ref,
                     y_ref, sl_ref, s_ref):
    c = pl.program_id(1)

    @pl.when(c == 0)
    def _():
        s_ref[...] = jnp.zeros_like(s_ref)

    cos, sin = cos_ref[...], sin_ref[...]
    for h in range(RET_HEADS):
        ksl = slice(h * RET_DK, (h + 1) * RET_DK)
        vsl = slice(h * RET_DV, (h + 1) * RET_DV)
        y, (s_new,) = _ret_chunk(q_ref[:, ksl], k_ref[:, ksl], v_ref[:, vsl], z_ref[:, vsl], cos, sin,
                                 dm_ref[h], qd_ref[h], kd_ref[h], cd_ref[h], gn_ref[:, vsl], [s_ref[h]], CHUNK)
        s_ref[h] = s_new
        y_ref[:, vsl] = y.astype(y_ref.dtype)

        @pl.when(c == pl.num_programs(1) - 1)
        def _():
            sl_ref[h] = s_new


def _ret_tables(seg_len, nseg, pos):
    rows = seg_len * nseg
    log_g = jnp.log1p(-jnp.exp2(-5.0 - jnp.arange(RET_HEADS, dtype=F32)))[:, None, None]
    t = (jnp.arange(rows) % seg_len).astype(F32)
    seg = jnp.arange(rows) // seg_len
    rel = t[:, None] - t[None, :]
    ok = (rel >= 0) & (seg[:, None] == seg[None, :])
    dmask = jnp.where(ok, jnp.exp(log_g * jnp.where(ok, rel, 0.0)), 0.0)
    qdec = jnp.broadcast_to(jnp.exp(log_g * (t + 1.0)[None, :, None]), (RET_HEADS, rows, RET_DV))
    kdec = jnp.broadcast_to(jnp.exp(log_g * (seg_len - 1.0 - t)[None, :, None]), (RET_HEADS, rows, RET_DK))
    cdec = jnp.broadcast_to(jnp.exp(log_g * seg_len), (RET_HEADS, 1, RET_DV))
    half = RET_DK // 2
    inv = ROPE_BASE ** (-jnp.arange(half, dtype=F32) / half)
    ang = pos.astype(F32)[:, None] * inv
    return jnp.cos(ang), jnp.sin(ang), dmask, qdec, kdec, cdec


def _ret_prompt(qkvz, gn, bp, tp):
    m = qkvz.shape[0]
    nc = tp // CHUNK
    cos, sin, dmask, qdec, kdec, cdec = _ret_tables(CHUNK, 1, jnp.arange(tp))
    qk_w = RET_HEADS * RET_DK
    full = lambda b, c: (0, 0, 0)
    return pl.pallas_call(
        _ret_prompt_body,
        out_shape=(jax.ShapeDtypeStruct((m, RET_VW), BF16),
                   jax.ShapeDtypeStruct((bp, RET_HEADS, RET_DK, RET_DV), F32)),
        grid=(bp, nc),
        in_specs=[pl.BlockSpec((CHUNK, qk_w), lambda b, c: (b * nc + c, 0)),
                  pl.BlockSpec((CHUNK, qk_w), lambda b, c: (b * nc + c, 1)),
                  pl.BlockSpec((CHUNK, RET_VW), lambda b, c: (b * nc + c, 1)),
                  pl.BlockSpec((CHUNK, RET_VW), lambda b, c: (b * nc + c, 2)),
                  pl.BlockSpec((CHUNK, RET_DK // 2), lambda b, c: (c, 0)),
                  pl.BlockSpec((CHUNK, RET_DK // 2), lambda b, c: (c, 0)),
                  pl.BlockSpec((RET_HEADS, CHUNK, CHUNK), full),
                  pl.BlockSpec((RET_HEADS, CHUNK, RET_DV), full),
                  pl.BlockSpec((RET_HEADS, CHUNK, RET_DK), full),
                  pl.BlockSpec((RET_HEADS, 1, RET_DV), full),
                  pl.BlockSpec((1, RET_VW), lambda b, c: (0, 0))],
        out_specs=(pl.BlockSpec((CHUNK, RET_VW), lambda b, c: (b * nc + c, 0)),
                   pl.BlockSpec((None, RET_HEADS, RET_DK, RET_DV), lambda b, c: (b, 0, 0, 0))),
        scratch_shapes=[pltpu.VMEM((RET_HEADS, RET_DK, RET_DV), F32)],
        compiler_params=_cparams(2),
    )(qkvz, qkvz, qkvz, qkvz, cos, sin, dmask, qdec, kdec, cdec, gn.reshape(1, RET_VW))


def _ret_sample_body(q_ref, k_ref, v_ref, z_ref, cos_ref, sin_ref, dm_ref, qd_ref, kd_ref, cd_ref, gn_ref,
                     s0_ref, yin_ref, y_ref, s_ref):
    del yin_ref
    y, s_new = _ret_chunk(q_ref[...], k_ref[...], v_ref[...], z_ref[...], cos_ref[...], sin_ref[...],
                          dm_ref[...], qd_ref[...], kd_ref[...], cd_ref[...], gn_ref[...],
                          [s0_ref[i] for i in range(SAMPLE_NB)], SAMPLE_T)
    for i in range(SAMPLE_NB):
        s_ref[i] = s_new[i]
    y_ref[...] = y.astype(y_ref.dtype)


def _ret_sample(qkvz, gn, state, y_all, rows_p, bs):
    rows = SAMPLE_NB * SAMPLE_T
    rb0 = rows_p // rows
    pos = PAST_LEN + jnp.arange(rows) % SAMPLE_T
    cos, sin, dmask, qdec, kdec, cdec = _ret_tables(SAMPLE_T, SAMPLE_NB, pos)
    kb0 = RET_HEADS
    vb0 = 2 * RET_HEADS * RET_DK // RET_DV
    zb0 = vb0 + RET_HEADS
    st_spec = pl.BlockSpec((None, SAMPLE_NB, None, RET_DK, RET_DV), lambda bb, h: (0, bb, h, 0, 0))
    return pl.pallas_call(
        _ret_sample_body,
        out_shape=(jax.ShapeDtypeStruct(y_all.shape, y_all.dtype), jax.ShapeDtypeStruct(state.shape, F32)),
        grid=(bs // SAMPLE_NB, RET_HEADS),
        in_specs=[pl.BlockSpec((rows, RET_DK), lambda bb, h: (rb0 + bb, h)),
                  pl.BlockSpec((rows, RET_DK), lambda bb, h: (rb0 + bb, kb0 + h)),
                  pl.BlockSpec((rows, RET_DV), lambda bb, h: (rb0 + bb, vb0 + h)),
                  pl.BlockSpec((rows, RET_DV), lambda bb, h: (rb0 + bb, zb0 + h)),
                  pl.BlockSpec((rows, RET_DK // 2), lambda bb, h: (0, 0)),
                  pl.BlockSpec((rows, RET_DK // 2), lambda bb, h: (0, 0)),
                  pl.BlockSpec((None, rows, rows), lambda bb, h: (h, 0, 0)),
                  pl.BlockSpec((None, rows, RET_DV), lambda bb, h: (h, 0, 0)),
                  pl.BlockSpec((None, rows, RET_DK), lambda bb, h: (h, 0, 0)),
                  pl.BlockSpec((None, 1, RET_DV), lambda bb, h: (h, 0, 0)),
                  pl.BlockSpec((1, RET_DV), lambda bb, h: (0, h)),
                  st_spec,
                  pl.BlockSpec(memory_space=pl.ANY)],
        out_specs=(pl.BlockSpec((rows, RET_DV), lambda bb, h: (rb0 + bb, h)), st_spec),
        input_output_aliases={12: 0},
        compiler_params=_cparams(2),
    )(qkvz, qkvz, qkvz, qkvz, cos, sin, dmask, qdec, kdec, cdec, gn.reshape(1, RET_VW), state, y_all)


def _head_expand_matrix():
    r = lax.broadcasted_iota(jnp.int32, (128, SSD_GW), 0)
    c = lax.broadcasted_iota(jnp.int32, (128, SSD_GW), 1)
    return jnp.where(r == (c >> _log2(SSD_HEADDIM)), 1.0, 0.0).astype(BF16)


def _expand_heads(v, expand_mat):
    v1 = v.astype(BF16)
    r1 = v - v1.astype(F32)
    v2 = r1.astype(BF16)
    v3 = (r1 - v2.astype(F32)).astype(BF16)
    return (_dot(v1, expand_mat) + _dot(v2, expand_mat)) + _dot(v3, expand_mat)


def _ssd_chunk(xs, bs, cs, z, dt_raw, dt_bias, a_log, d_skip, norm_g, states, seg_len, expand_mat):
    rows = xs.shape[0]
    single = len(states) == 1
    hp = SSD_HPG
    pad = 128
    t8 = _row_pos(rows, seg_len, pad)
    seg8 = _row_segment(rows, seg_len, pad)
    tt = lax.broadcasted_iota(jnp.int32, (rows, rows), 0)
    ss = lax.broadcasted_iota(jnp.int32, (rows, rows), 1)
    causal = (tt >= ss) & ((tt >> _log2(seg_len)) == (ss >> _log2(seg_len)))

    def lane_pad(v):
        return jnp.concatenate([v, jnp.zeros((v.shape[0], pad - hp), F32)], axis=1)

    dt = jax.nn.softplus(lane_pad(dt_raw) + lane_pad(dt_bias))
    da = dt * (-jnp.exp(lane_pad(a_log)))
    cum = da
    shift = 1
    while shift < seg_len:
        cum = cum + jnp.where(t8 >= shift, pltpu.roll(cum, shift, 0), 0.0)
        shift *= 2
    cum_sq = cum if rows == pad else jnp.concatenate([cum, jnp.zeros((pad - rows, pad), F32)], axis=0)
    cum_t = cum_sq.T[0:hp, 0:rows]
    lasts = [cum[(i + 1) * seg_len - 1:(i + 1) * seg_len, :] for i in range(len(states))]
    last_row = lasts[0]
    if not single:
        last_row = jnp.zeros((rows, pad), F32)
        for i, l in enumerate(lasts):
            last_row = jnp.where(seg8 == i, l, last_row)
    to_end = jnp.exp(last_row - cum)
    expanded = _expand_heads(jnp.concatenate([dt, jnp.exp(cum), to_end], axis=0), expand_mat)
    dt_x, ecum, to_end_x = expanded[0:rows], expanded[rows:2 * rows], expanded[2 * rows:3 * rows]

    csb = cs.astype(BF16)
    bsb = bs.astype(BF16)
    cb = _dot_nt(csb, bsb)
    xdt = xs * dt_x
    ycols = []
    for r in range(hp):
        lmat = jnp.where(causal, jnp.exp(cum[:, r:r + 1] - cum_t[r:r + 1, :]), 0.0)
        ycols.append(_dot((cb * lmat).astype(BF16), xdt[:, r * SSD_HEADDIM:(r + 1) * SSD_HEADDIM].astype(BF16)))
    y = jnp.concatenate(ycols, axis=1)
    xte = xdt * to_end_x
    segw = None if single else _row_segment(rows, seg_len, SSD_GW)
    new_states = []
    for i, h in enumerate(states):
        y_off = _dot_nt(csb, h.astype(BF16)) * ecum
        xi = xte
        if not single:
            y_off = jnp.where(segw == i, y_off, 0.0)
            xi = jnp.where(segw == i, xte, 0.0)
        y = y + y_off
        e_last = jnp.exp(lasts[i])
        dec = jnp.concatenate([jnp.broadcast_to(e_last[:, r:r + 1], (SSD_HEADDIM, SSD_STATE)) for r in range(hp)],
                              axis=0)
        new_states.append(h * dec + _dot_tn(xi.astype(BF16), bsb))
    y = y + xs * _expand_cols(d_skip, SSD_HEADDIM)
    yg = y * jax.nn.silu(z)
    yg = yg * lax.rsqrt(jnp.mean(yg * yg, axis=-1, keepdims=True) + EPS)
    return yg * norm_g, new_states


def _ssd_prompt_body(z_ref, x_ref, b_ref, c_ref, dt_ref, wx_ref, wb_ref, wc_ref, bx_ref, bb_ref, bc_ref,
                     dtb_ref, al_ref, ds_ref, ng_ref, y_ref, hl_ref, xe_ref, be_ref, ce_ref, h_ref):
    c = pl.program_id(1)

    @pl.when(c == 0)
    def _():
        xe_ref[0:8, :] = jnp.zeros((8, SSD_INNER), F32)
        be_ref[0:8, :] = jnp.zeros((8, SSD_GROUPS * SSD_STATE), F32)
        ce_ref[0:8, :] = jnp.zeros((8, SSD_GROUPS * SSD_STATE), F32)
        h_ref[...] = jnp.zeros_like(h_ref)

    def conv(raw_ref, ext_ref, w_ref, bias_ref, sl):
        raw = raw_ref[:, sl]
        ext_ref[8:8 + CHUNK, sl] = raw
        acc = w_ref[3:4, sl] * raw + bias_ref[:, sl]
        for k in range(CONV_K - 1):
            acc = acc + w_ref[k:k + 1, sl] * ext_ref[5 + k:5 + k + CHUNK, sl]
        ext_ref[0:8, sl] = raw[CHUNK - 8:CHUNK, :]
        return jax.nn.silu(acc)

    expand_mat = _head_expand_matrix()
    for g in range(SSD_GROUPS):
        xsl = slice(g * SSD_GW, (g + 1) * SSD_GW)
        nsl = slice(g * SSD_STATE, (g + 1) * SSD_STATE)
        xs = conv(x_ref, xe_ref, wx_ref, bx_ref, xsl)
        bs = conv(b_ref, be_ref, wb_ref, bb_ref, nsl)
        cs = conv(c_ref, ce_ref, wc_ref, bc_ref, nsl)
        y, (h_new,) = _ssd_chunk(xs, bs, cs, z_ref[:, xsl], dt_ref[g], dtb_ref[g], al_ref[g], ds_ref[g],
                                 ng_ref[:, xsl], [h_ref[g]], CHUNK, expand_mat)
        h_ref[g] = h_new
        y_ref[:, xsl] = y.astype(y_ref.dtype)

        @pl.when(c == pl.num_programs(1) - 1)
        def _():
            hl_ref[g] = h_new


def _ssd_col_blocks():
    xb0 = SSD_INNER // SSD_GW
    bb0 = (2 * SSD_INNER) // SSD_STATE
    cb0 = bb0 + SSD_GROUPS
    return xb0, bb0, cb0


def _ssd_param_arrays(conv_w, conv_b, dt_bias, a_log, d_skip, norm_g):
    g, hp = SSD_GROUPS, SSD_HPG
    return [conv_w, conv_w, conv_w, conv_b.reshape(1, 1, -1), conv_b.reshape(1, 1, -1), conv_b.reshape(1, 1, -1),
            dt_bias.reshape(g, 1, hp), a_log.reshape(g, 1, hp), d_skip.reshape(g, 1, hp), norm_g.reshape(1, SSD_INNER)]


def _ssd_param_specs(gmap):
    wxb0 = 0
    wbb0 = SSD_INNER // SSD_STATE
    wcb0 = wbb0 + SSD_GROUPS
    return [pl.BlockSpec((None, CONV_K, SSD_GW), gmap(lambda g: (0, 0, wxb0 + g))),
            pl.BlockSpec((None, CONV_K, SSD_STATE), gmap(lambda g: (0, 0, wbb0 + g))),
            pl.BlockSpec((None, CONV_K, SSD_STATE), gmap(lambda g: (0, 0, wcb0 + g))),
            pl.BlockSpec((None, 1, SSD_GW), gmap(lambda g: (0, 0, wxb0 + g))),
            pl.BlockSpec((None, 1, SSD_STATE), gmap(lambda g: (0, 0, wbb0 + g))),
            pl.BlockSpec((None, 1, SSD_STATE), gmap(lambda g: (0, 0, wcb0 + g))),
            pl.BlockSpec((None, 1, SSD_HPG), gmap(lambda g: (g, 0, 0))),
            pl.BlockSpec((None, 1, SSD_HPG), gmap(lambda g: (g, 0, 0))),
            pl.BlockSpec((None, 1, SSD_HPG), gmap(lambda g: (g, 0, 0))),
            pl.BlockSpec((1, SSD_GW), gmap(lambda g: (0, g)))]


def _ssd_prompt(zx, dtg, params, bp, tp):
    m = zx.shape[0]
    nc = tp // CHUNK
    gn = SSD_GROUPS * SSD_STATE
    b_blk = 2 * SSD_INNER // gn
    wb_blk = SSD_INNER // gn
    hp3 = (SSD_GROUPS, 1, SSD_HPG)
    zero3 = lambda b, c: (0, 0, 0)
    return pl.pallas_call(
        _ssd_prompt_body,
        out_shape=(jax.ShapeDtypeStruct((m, SSD_INNER), BF16),
                   jax.ShapeDtypeStruct((bp, SSD_GROUPS, SSD_GW, SSD_STATE), F32)),
        grid=(bp, nc),
        in_specs=[pl.BlockSpec((CHUNK, SSD_INNER), lambda b, c: (b * nc + c, 0)),
                  pl.BlockSpec((CHUNK, SSD_INNER), lambda b, c: (b * nc + c, 1)),
                  pl.BlockSpec((CHUNK, gn), lambda b, c: (b * nc + c, b_blk)),
                  pl.BlockSpec((CHUNK, gn), lambda b, c: (b * nc + c, b_blk + 1)),
                  pl.BlockSpec((SSD_GROUPS, CHUNK, SSD_HPG), lambda b, c: (0, b * nc + c, 0)),
                  pl.BlockSpec((None, CONV_K, SSD_INNER), zero3),
                  pl.BlockSpec((None, CONV_K, gn), lambda b, c: (0, 0, wb_blk)),
                  pl.BlockSpec((None, CONV_K, gn), lambda b, c: (0, 0, wb_blk + 1)),
                  pl.BlockSpec((None, 1, SSD_INNER), zero3),
                  pl.BlockSpec((None, 1, gn), lambda b, c: (0, 0, wb_blk)),
                  pl.BlockSpec((None, 1, gn), lambda b, c: (0, 0, wb_blk + 1)),
                  pl.BlockSpec(hp3, zero3), pl.BlockSpec(hp3, zero3), pl.BlockSpec(hp3, zero3),
                  pl.BlockSpec((1, SSD_INNER), lambda b, c: (0, 0))],
        out_specs=(pl.BlockSpec((CHUNK, SSD_INNER), lambda b, c: (b * nc + c, 0)),
                   pl.BlockSpec((None, SSD_GROUPS, SSD_GW, SSD_STATE), lambda b, c: (b, 0, 0, 0))),
        scratch_shapes=[pltpu.VMEM((CHUNK + 8, SSD_INNER), F32), pltpu.VMEM((CHUNK + 8, gn), F32),
                        pltpu.VMEM((CHUNK + 8, gn), F32), pltpu.VMEM((SSD_GROUPS, SSD_GW, SSD_STATE), F32)],
        compiler_params=_cparams(2),
    )(zx, zx, zx, zx, dtg, *params)


def _ssd_sample_body(z_ref, x_ref, b_ref, c_ref, dt_ref, px_ref, pb_ref, pc_ref, wx_ref, wb_ref, wc_ref,
                     bx_ref, bb_ref, bc_ref, dtb_ref, al_ref, ds_ref, ng_ref, h0_ref, yin_ref, y_ref, h_ref):
    del yin_ref

    def conv(raw_ref, prev_ref, w_ref, bias_ref):
        raw = raw_ref[...]
        rows, width = raw.shape
        t = _row_pos(rows, SAMPLE_T, width)
        acc = w_ref[3:4, :] * raw + bias_ref[...]
        for k in range(1, CONV_K):
            acc = acc + w_ref[3 - k:4 - k, :] * jnp.where(t >= k, pltpu.roll(raw, k, 0), prev_ref[k - 1])
        return jax.nn.silu(acc)

    xs = conv(x_ref, px_ref, wx_ref, bx_ref)
    bs = conv(b_ref, pb_ref, wb_ref, bb_ref)
    cs = conv(c_ref, pc_ref, wc_ref, bc_ref)
    y, h_new = _ssd_chunk(xs, bs, cs, z_ref[...], dt_ref[...], dtb_ref[...], al_ref[...], ds_ref[...],
                          ng_ref[...], [h0_ref[i] for i in range(SAMPLE_NB)], SAMPLE_T, _head_expand_matrix())
    for i in range(SAMPLE_NB):
        h_ref[i] = h_new[i]
    y_ref[...] = y.astype(y_ref.dtype)


def _ssd_sample(zx, dtg, prevs, params, state, y_all, rows_p, bs):
    rows = SAMPLE_NB * SAMPLE_T
    rb0 = rows_p // rows
    xb0, bb0, cb0 = _ssd_col_blocks()
    wbb0 = SSD_INNER // SSD_STATE
    wcb0 = wbb0 + SSD_GROUPS

    def gmap(f):
        return lambda bb, g: f(g)
    st_spec = pl.BlockSpec((None, SAMPLE_NB, None, SSD_GW, SSD_STATE), lambda bb, g: (0, bb, g, 0, 0))
    return pl.pallas_call(
        _ssd_sample_body,
        out_shape=(jax.ShapeDtypeStruct(y_all.shape, y_all.dtype), jax.ShapeDtypeStruct(state.shape, F32)),
        grid=(bs // SAMPLE_NB, SSD_GROUPS),
        in_specs=[pl.BlockSpec((rows, SSD_GW), lambda bb, g: (rb0 + bb, g)),
                  pl.BlockSpec((rows, SSD_GW), lambda bb, g: (rb0 + bb, xb0 + g)),
                  pl.BlockSpec((rows, SSD_STATE), lambda bb, g: (rb0 + bb, bb0 + g)),
                  pl.BlockSpec((rows, SSD_STATE), lambda bb, g: (rb0 + bb, cb0 + g)),
                  pl.BlockSpec((None, rows, SSD_HPG), lambda bb, g: (g, rb0 + bb, 0)),
                  pl.BlockSpec((CONV_K - 1, rows, SSD_GW), lambda bb, g: (0, bb, g)),
                  pl.BlockSpec((CONV_K - 1, rows, SSD_STATE), lambda bb, g: (0, bb, wbb0 + g)),
                  pl.BlockSpec((CONV_K - 1, rows, SSD_STATE), lambda bb, g: (0, bb, wcb0 + g))]
                 + _ssd_param_specs(gmap) + [st_spec, pl.BlockSpec(memory_space=pl.ANY)],
        out_specs=(pl.BlockSpec((rows, SSD_GW), lambda bb, g: (rb0 + bb, g)), st_spec),
        input_output_aliases={19: 0},
        compiler_params=_cparams(2),
    )(zx, zx, zx, zx, dtg, prevs, prevs, prevs, *params, state, y_all)


def _s5_body(*refs, sub, nsub, n_scan, has_h0, has_alias):
    it = iter(refs)
    x_ref, m1_ref, vt_ref = next(it), next(it), next(it)
    are_ref, aim_ref, sre_ref, sim_ref, dsk_ref = (next(it) for _ in range(5))
    h0_ref = next(it) if has_h0 else None
    if has_alias:
        next(it)
    s_ref, hl_ref = next(it), next(it)

    xs = [x_ref[pl.ds(s, nsub, stride=sub), :] for s in range(sub)]
    r = _dot(jnp.concatenate(xs, axis=1).astype(BF16), m1_ref[...])
    cb, y = r[:, :2 * S5_HW], r[:, 2 * S5_HW:]

    def cmul(h, a_re, a_im):
        return h * a_re + pltpu.roll(h, S5_HW, 1) * a_im

    if has_h0:
        hprev = h0_ref[...]
        h = cb + cmul(hprev, are_ref[...], aim_ref[...])
    else:
        h = cb
        j = lax.broadcasted_iota(jnp.int32, h.shape, 0)
        for step in range(n_scan):
            shift = 1 << step
            prev = jnp.where(j >= shift, pltpu.roll(h, shift, 0), 0.0)
            h = h + cmul(prev, sre_ref[step], sim_ref[step])
        hprev = jnp.where(j >= 1, pltpu.roll(h, 1, 0), 0.0)
    hl_ref[...] = h[nsub - 1:nsub, :] if not has_h0 else h
    y = y + _dot(hprev.astype(BF16), vt_ref[...])
    for t in range(sub):
        s_ref[pl.ds(t, nsub, stride=sub), :] = y[:, t * 128:(t + 1) * 128] + dsk_ref[...] * xs[t]


def _s5_tables(lam_re, lam_im, b_re, b_im, c_re, c_im, log_dt, sub, n_scan):
    g, p = S5_GROUPS, S5_STATE
    dt = jnp.exp(log_dt)[:, None]

    def apow(d):
        mag = jnp.exp(lam_re * dt * d)
        return mag * jnp.cos(lam_im * dt * d), mag * jnp.sin(lam_im * dt * d)

    a_re, a_im = apow(1.0)
    den = lam_re * lam_re + lam_im * lam_im
    f_re = ((a_re - 1.0) * lam_re + a_im * lam_im) / den
    f_im = (a_im * lam_re - (a_re - 1.0) * lam_im) / den
    bb_re = f_re[..., None] * b_re - f_im[..., None] * b_im
    bb_im = f_re[..., None] * b_im + f_im[..., None] * b_re
    eye = jnp.eye(S5_G8, dtype=F32)

    def blockdiag(m):
        i, j = m.shape[1:]
        m = m.reshape(S5_NBLK, S5_G8, i, j)
        return jnp.einsum('Ggij,gh->Ggihj', m, eye).reshape(S5_NBLK, S5_G8 * i, S5_G8 * j)

    kd, wd, vt = [], [], []
    for d in range(sub):
        p_re, p_im = apow(float(d))
        ab_re = p_re[..., None] * bb_re - p_im[..., None] * bb_im
        ab_im = p_re[..., None] * bb_im + p_im[..., None] * bb_re
        kd.append(blockdiag(jnp.einsum('gcp,gpk->gkc', c_re, ab_re) - jnp.einsum('gcp,gpk->gkc', c_im, ab_im)))
        w_re = blockdiag(jnp.swapaxes(ab_re, 1, 2))
        w_im = blockdiag(jnp.swapaxes(ab_im, 1, 2))
        wd.append(jnp.concatenate([w_re, w_im], axis=2))
        q_re, q_im = apow(float(d + 1))
        m_re = c_re * q_re[:, None, :] - c_im * q_im[:, None, :]
        m_im = c_re * q_im[:, None, :] + c_im * q_re[:, None, :]
        vt.append(jnp.concatenate([blockdiag(jnp.swapaxes(m_re, 1, 2)), -blockdiag(jnp.swapaxes(m_im, 1, 2))], axis=1))
    zero = jnp.zeros_like(kd[0])
    kd_all = jnp.concatenate([jnp.concatenate([kd[t - s] if t >= s else zero for t in range(sub)], axis=2)
                              for s in range(sub)], axis=1)
    wd_all = jnp.concatenate(wd[::-1], axis=1)
    m1 = jnp.concatenate([wd_all, kd_all], axis=2).astype(BF16)
    vt_all = jnp.concatenate(vt, axis=2).astype(BF16)

    def lanes(re, im):
        re = re.reshape(S5_NBLK, 1, S5_HW)
        im = im.reshape(S5_NBLK, 1, S5_HW)
        return jnp.concatenate([re, re], axis=2), jnp.concatenate([-im, im], axis=2)

    are, aim = lanes(*apow(float(sub)))
    steps = [lanes(*apow(float(sub * (1 << k)))) for k in range(max(n_scan, 1))]
    sre = jnp.stack([s[0] for s in steps], axis=1)
    sim = jnp.stack([s[1] for s in steps], axis=1)
    return m1, vt_all, are, aim, sre, sim


def _s5_core(xz, tables, d_skip, s_all, h0, *, sub, nsub, n_seq, row_block0, n_scan):
    has_h0 = h0 is not None
    rows = sub * nsub
    ns = tables[4].shape[1]
    hw2 = 2 * S5_HW
    sw = sub * 128

    def blk(shape, f):
        return pl.BlockSpec(shape, lambda gb, b: f(gb, b))
    in_specs = [blk((rows, 128), lambda gb, b: (row_block0 + b, gb)),
                blk((None, sw, hw2 + sw), lambda gb, b: (gb, 0, 0)),
                blk((None, hw2, sw), lambda gb, b: (gb, 0, 0)),
                blk((None, 1, hw2), lambda gb, b: (gb, 0, 0)),
                blk((None, 1, hw2), lambda gb, b: (gb, 0, 0)),
                blk((None, ns, 1, hw2), lambda gb, b: (gb, 0, 0, 0)),
                blk((None, ns, 1, hw2), lambda gb, b: (gb, 0, 0, 0)),
                blk((1, 128), lambda gb, b: (0, gb))]
    args = [xz, *tables, d_skip.reshape(1, -1)]
    if has_h0:
        in_specs.append(blk((None, nsub, hw2), lambda gb, b: (gb, 0, 0)))
        args.append(h0)
        hl_shape = (S5_NBLK, nsub, hw2)
        hl_spec = blk((None, nsub, hw2), lambda gb, b: (gb, 0, 0))
    else:
        hl_shape = (S5_NBLK, n_seq, 1, hw2)
        hl_spec = blk((None, None, 1, hw2), lambda gb, b: (gb, b, 0, 0))
    aliases = {}
    if s_all is not None:
        in_specs.append(pl.BlockSpec(memory_space=pl.ANY))
        args.append(s_all)
        aliases = {len(args) - 1: 0}
    body = functools.partial(_s5_body, sub=sub, nsub=nsub, n_scan=n_scan, has_h0=has_h0,
                             has_alias=s_all is not None)
    return pl.pallas_call(
        body,
        out_shape=(jax.ShapeDtypeStruct((xz.shape[0], D_MODEL), F32), jax.ShapeDtypeStruct(hl_shape, F32)),
        grid=(S5_NBLK, n_seq),
        in_specs=in_specs,
        out_specs=(blk((rows, 128), lambda gb, b: (row_block0 + b, gb)), hl_spec),
        input_output_aliases=aliases,
        compiler_params=_cparams(2),
    )(*args)


def _s5_glu_body(s_ref, st_ref, z_ref, w_ref, b_ref, o_ref, gb_ref, *, tm, row_chunk):
    @pl.when(pl.program_id(1) == 0)
    def _():
        def body(i, carry):
            r0 = pl.multiple_of(i * row_chunk, 16)
            gb_ref[pl.ds(r0, row_chunk), :] = jax.nn.gelu(s_ref[pl.ds(r0, row_chunk), :]).astype(BF16)
            return carry
        lax.fori_loop(0, tm // row_chunk, body, 0)
    g = jax.nn.gelu(st_ref[...])
    o = g * jax.nn.sigmoid(_dot(gb_ref[...], w_ref[...].astype(BF16)) + b_ref[...])
    o_ref[...] = (o * jax.nn.silu(z_ref[...])).astype(o_ref.dtype)


def _s5_glu(s_all, xz, glu_w, glu_b):
    m, k = s_all.shape
    tn = 512
    tm = _largest_divisor(m, 1088, 16)
    row_chunk = _largest_divisor(tm, 272, 16)
    zb0 = k // tn
    body = functools.partial(_s5_glu_body, tm=tm, row_chunk=row_chunk)
    return pl.pallas_call(
        body,
        out_shape=jax.ShapeDtypeStruct((m, k), BF16),
        grid=(m // tm, k // tn),
        in_specs=[pl.BlockSpec((tm, k), lambda i, j: (i, 0)),
                  pl.BlockSpec((tm, tn), lambda i, j: (i, j)),
                  pl.BlockSpec((tm, tn), lambda i, j: (i, zb0 + j)),
                  pl.BlockSpec((None, k, tn), lambda i, j: (0, 0, j)),
                  pl.BlockSpec((1, tn), lambda i, j: (0, j))],
        out_specs=pl.BlockSpec((tm, tn), lambda i, j: (i, j)),
        scratch_shapes=[pltpu.VMEM((tm, k), BF16)],
        compiler_params=_cparams(2),
    )(s_all, s_all, xz, glu_w, glu_b.reshape(1, k))


def _last_conv_inputs(a, bp, tp, col0, n_cols):
    return jnp.stack([a[(b + 1) * tp - (CONV_K - 1):(b + 1) * tp, col0:col0 + n_cols] for b in range(bp)])


def _lru_layer(u_norm, h, lay, j, dims, state_conv, state_h, w_in, conv_w, conv_b, wa, ba, wx, bx, lam, w_out):
    bp, tp, bs = dims
    rows_p = bp * tp
    xz = _matmul(h, w_in, j, norm_g=u_norm, g_idx=lay)
    params = _lru_params(conv_w[j:j + 1], conv_b[j], wa[j:j + 1], wx[j:j + 1], ba[j], bx[j], lam[j])
    y, hl_p = _lru_prompt(xz, params, bp, tp)
    prevs = _conv_prev_rows(state_conv[j])
    h0_rows = jnp.repeat(state_h[j], SAMPLE_T, axis=0)
    y, h_rows = _lru_sample(xz, prevs, h0_rows, params, y, rows_p)
    x_s = xz[rows_p:, :LRU_WIDTH].reshape(bs, SAMPLE_T, LRU_WIDTH)
    outs = (_last_conv_inputs(xz, bp, tp, 0, LRU_WIDTH), x_s[:, SAMPLE_T - (CONV_K - 1):],
            hl_p.reshape(bp, LRU_WIDTH), h_rows.reshape(bs, SAMPLE_T, LRU_WIDTH)[:, SAMPLE_T - 1])
    return _matmul(y, w_out, j, residual=h), outs


def _ret_layer(u_norm, h, lay, j, dims, state, w_in, gn, w_out):
    bp, tp, bs = dims
    rows_p = bp * tp
    qkvz = _matmul(h, w_in, j, norm_g=u_norm, g_idx=lay)
    y, s_p = _ret_prompt(qkvz, gn[j], bp, tp)
    y, s_s = _ret_sample(qkvz, gn[j], state[j:j + 1], y, rows_p, bs)
    return _matmul(y, w_out, j, residual=h, tn=256), (s_p, s_s[0])


def _ssd_layer(u_norm, h, lay, j, dims, state_conv, state, w_in, conv_w, conv_b, dt_bias, a_log, d_skip,
               norm_g, w_out):
    bp, tp, bs = dims
    rows_p = bp * tp
    m = h.shape[0]
    n_main = SSD_INNER + SSD_CONV_DIM
    zx = _matmul(h, w_in, j, norm_g=u_norm, g_idx=lay, n_cols=n_main)
    dt = _matmul(h, w_in[j:j + 1, :, n_main:], 0, norm_g=u_norm, g_idx=lay)
    dtg = dt.reshape(m, SSD_GROUPS, SSD_HPG).transpose(1, 0, 2)
    params = _ssd_param_arrays(conv_w[j:j + 1], conv_b[j], dt_bias[j], a_log[j], d_skip[j], norm_g[j])
    y, hl_p = _ssd_prompt(zx, dtg, params, bp, tp)
    prevs = jnp.stack(_conv_prev_rows(state_conv[j]))
    st = state[j:j + 1].reshape(1, bs, SSD_GROUPS, SSD_GW, SSD_STATE)
    y, hl_s = _ssd_sample(zx, dtg, prevs, params, st, y, rows_p, bs)
    xbc_s = zx[rows_p:, SSD_INNER:].reshape(bs, SAMPLE_T, SSD_CONV_DIM)
    shape = (SSD_HEADS, SSD_HEADDIM, SSD_STATE)
    outs = (_last_conv_inputs(zx, bp, tp, SSD_INNER, SSD_CONV_DIM), xbc_s[:, SAMPLE_T - (CONV_K - 1):],
            hl_p.reshape((bp,) + shape), hl_s.reshape((bs,) + shape))
    return _matmul(y, w_out, j, residual=h, tn=256), outs


S5_SUB = 8


def _s5_layer(u_norm, h, lay, j, dims, h0_re, h0_im, w_in, lam_re, lam_im, b_re, b_im, c_re, c_im, d_skip,
              log_dt, glu_w, glu_b, w_out):
    bp, tp, bs = dims
    rows_p = bp * tp
    m = h.shape[0]
    xz = _matmul(h, w_in, j, norm_g=u_norm, g_idx=lay)
    nsub_p = tp // S5_SUB
    n_scan = max(nsub_p - 1, 0).bit_length()
    par = (lam_re[j], lam_im[j], b_re[j], b_im[j], c_re[j], c_im[j], log_dt[j])
    s_all, hl_p = _s5_core(xz, _s5_tables(*par, S5_SUB, n_scan), d_skip[j], None, None,
                           sub=S5_SUB, nsub=nsub_p, n_seq=bp, row_block0=0, n_scan=n_scan)

    def to_lanes(v):
        return v.reshape(bs, S5_NBLK, S5_HW).transpose(1, 0, 2)
    h0 = jnp.concatenate([to_lanes(h0_re[j]), to_lanes(h0_im[j])], axis=2)
    s_all, hl_s = _s5_core(xz, _s5_tables(*par, SAMPLE_T, 0), d_skip[j], s_all, h0,
                           sub=SAMPLE_T, nsub=bs, n_seq=1, row_block0=rows_p // (bs * SAMPLE_T), n_scan=0)
    y = _s5_glu(s_all, xz, glu_w[j:j + 1], glu_b[j])

    def from_lanes(v, nb):
        return v.transpose(1, 0, 2).reshape(nb, S5_GROUPS, S5_STATE)
    hl_p = hl_p.reshape(S5_NBLK, bp, 2 * S5_HW)
    outs = (from_lanes(hl_p[..., :S5_HW], bp), from_lanes(hl_s[..., :S5_HW], bs),
            from_lanes(hl_p[..., S5_HW:], bp), from_lanes(hl_s[..., S5_HW:], bs))
    return _matmul(y, w_out, j, residual=h), outs


def kernel(x_prompt, x_sample, state_lru_conv, state_lru_h, state_ret, state_ssd_conv, state_ssd, state_s5_re, state_s5_im, cache_mem_k, cache_mem_v, mem_prompt, mix_norm, xa_norm, xa_mem_norm, xa_wq, xa_wkv, xa_wo, final_norm, lru_w_in, lru_conv_w, lru_conv_b, lru_wa, lru_ba, lru_wx, lru_bx, lru_lambda, lru_w_out, ret_w_in, ret_gn, ret_w_out, ssd_w_in, ssd_conv_w, ssd_conv_b, ssd_dt_bias, ssd_a_log, ssd_d, ssd_norm, ssd_w_out, s5_w_in, s5_lambda_re, s5_lambda_im, s5_b_re, s5_b_im, s5_c_re, s5_c_im, s5_d, s5_log_dt, s5_glu_w, s5_glu_b, s5_w_out):
    bp, tp, d = x_prompt.shape
    bs, ts, _ = x_sample.shape
    depth = mix_norm.shape[0]
    assert d == D_MODEL and ts == SAMPLE_T and tp % LRU_TC == 0 and bs % SAMPLE_NB == 0
    rows_p, rows_s = bp * tp, bs * ts
    assert rows_p % rows_s == 0
    dims = (bp, tp, bs)
    h = jnp.concatenate([x_prompt.reshape(rows_p, d), x_sample.reshape(rows_s, d)], axis=0)
    mem = mem_prompt.reshape(bp * MEM_LEN, d)
    mix_g = mix_norm.reshape(depth, 1, d)
    xa_g = xa_norm.reshape(depth, 1, d)
    mem_g = xa_mem_norm.reshape(depth, 1, d)

    outs = {k: [] for k in ("lru", "ret", "ssd", "s5")}
    mem_k, mem_v = [], []
    for i in range(depth):
        kind, j = i % 4, i // 4
        if kind == 0:
            h, o = _lru_layer(mix_g, h, i, j, dims, state_lru_conv, state_lru_h, lru_w_in,
                              lru_conv_w, lru_conv_b, lru_wa, lru_ba, lru_wx, lru_bx, lru_lambda, lru_w_out)
            outs["lru"].append(o)
        elif kind == 1:
            h, o = _ret_layer(mix_g, h, i, j, dims, state_ret, ret_w_in, ret_gn, ret_w_out)
            outs["ret"].append(o)
        elif kind == 2:
            h, o = _ssd_layer(mix_g, h, i, j, dims, state_ssd_conv, state_ssd, ssd_w_in,
                              ssd_conv_w, ssd_conv_b, ssd_dt_bias, ssd_a_log, ssd_d, ssd_norm, ssd_w_out)
            outs["ssd"].append(o)
        else:
            h, o = _s5_layer(mix_g, h, i, j, dims, state_s5_re, state_s5_im, s5_w_in,
                             s5_lambda_re, s5_lambda_im, s5_b_re, s5_b_im, s5_c_re, s5_c_im, s5_d, s5_log_dt,
                             s5_glu_w, s5_glu_b, s5_w_out)
            outs["s5"].append(o)
        kv = _matmul(mem, xa_wkv, i, norm_g=mem_g, g_idx=i)
        q = _matmul(h, xa_wq, i, norm_g=xa_g, g_idx=i, out_dtype=BF16)
        o_att = _xattn_prompt(q, kv, bp, tp)
        o_att = _xattn_sample(q, cache_mem_k, cache_mem_v, i, o_att, rows_p, bs)
        h = _matmul(o_att, xa_wo, i, residual=h)
        mem_k.append(kv[:, :d].reshape(bp, MEM_LEN, XA_HEADS, XA_HD))
        mem_v.append(kv[:, d:].reshape(bp, MEM_LEN, XA_HEADS, XA_HD))
    y_p = _rmsnorm(h, final_norm, 0, rows_p)
    y_s = _rmsnorm(h, final_norm, rows_p, rows_s)

    def stack(kind, idx):
        return jnp.stack([o[idx] for o in outs[kind]])
    return (y_p.reshape(bp, tp, d), y_s.reshape(bs, ts, d),
            stack("lru", 0), stack("lru", 1), stack("lru", 2), stack("lru", 3),
            stack("ret", 0), stack("ret", 1),
            stack("ssd", 0), stack("ssd", 1), stack("ssd", 2), stack("ssd", 3),
            stack("s5", 0), stack("s5", 1), stack("s5", 2), stack("s5", 3),
            jnp.stack(mem_k), jnp.stack(mem_v))
```

```python
import functools
import math

import jax
import jax.numpy as jnp
from jax import lax
from jax.experimental import pallas as pl
from jax.experimental.pallas import tpu as pltpu

F32 = jnp.float32
BF16 = jnp.bfloat16

D_MODEL = 2048
PAST_LEN = 16384
EPS = 1e-6
CONV_K = 4
CHUNK = 128
LRU_WIDTH = D_MODEL
LRU_BLOCK = 256
LRU_C = 8.0
RET_HEADS = 8
RET_DK = 256
RET_DV = 512
RET_VW = RET_HEADS * RET_DV
ROPE_BASE = 10000.0
SSD_INNER = 2 * D_MODEL
SSD_HEADDIM = 64
SSD_HEADS = 64
SSD_GROUPS = 8
SSD_HPG = 8
SSD_STATE = 128
SSD_GW = SSD_HPG * SSD_HEADDIM
SSD_CONV_DIM = SSD_INNER + 2 * SSD_GROUPS * SSD_STATE
S5_GROUP = 16
S5_GROUPS = 128
S5_STATE = 64
S5_G8 = 8
S5_NBLK = S5_GROUPS // S5_G8
S5_HW = S5_G8 * S5_STATE
MEM_LEN = 256
XA_HEADS = 4
XA_HD = 512
SAMPLE_T = 4
SAMPLE_NB = 4

VMEM_LIMIT_BYTES = 56 * 1024 * 1024
MM_MAX_ROWS = 2176
MM_X_DOUBLE_BUFFER_BYTES = 18 * 1024 * 1024
NT_DIMS = (((1,), (1,)), ((), ()))
TN_DIMS = (((0,), (0,)), ((), ()))


def _cparams(n_axes):
    return pltpu.CompilerParams(dimension_semantics=("arbitrary",) * n_axes,
                                vmem_limit_bytes=VMEM_LIMIT_BYTES)


def _dot(a, b):
    return jnp.dot(a, b, preferred_element_type=F32)


def _dot_nt(a, b):
    return lax.dot_general(a, b, NT_DIMS, preferred_element_type=F32)


def _dot_tn(a, b):
    return lax.dot_general(a, b, TN_DIMS, preferred_element_type=F32)


def _largest_divisor(n, cap, mult):
    for d in range(min(cap, n) // mult * mult, 0, -mult):
        if n % d == 0:
            return d
    raise ValueError(f"no divisor of {n} that is a multiple of {mult}")


def _expm1(x):
    return jnp.where(jnp.abs(x) < 0.5, jnp.tanh(0.5 * x) * (jnp.exp(x) + 1.0), jnp.exp(x) - 1.0)


def _expand_cols(v, width):
    rows, n = v.shape
    return jnp.concatenate([jnp.broadcast_to(v[:, r:r + 1], (rows, width)) for r in range(n)], axis=1)


def _mm_body(*refs, has_norm, has_res, stage_x, tm, row_chunk, w_transposed):
    it = iter(refs)
    x_ref, w_ref = next(it), next(it)
    g_ref = next(it) if has_norm else None
    r_ref = next(it) if has_res else None
    o_ref = next(it)
    if stage_x:
        xb_ref = next(it)

        @pl.when(pl.program_id(1) == 0)
        def _():
            def body(i, carry):
                r0 = pl.multiple_of(i * row_chunk, 16)
                x = x_ref[pl.ds(r0, row_chunk), :].astype(F32)
                if has_norm:
                    x = x * lax.rsqrt(jnp.mean(x * x, axis=-1, keepdims=True) + EPS) * g_ref[...]
                xb_ref[pl.ds(r0, row_chunk), :] = x.astype(BF16)
                return carry
            lax.fori_loop(0, tm // row_chunk, body, 0)
        xb = xb_ref[...]
    else:
        xb = x_ref[...]
    wb = w_ref[...].astype(BF16)
    acc = _dot_nt(xb, wb) if w_transposed else _dot(xb, wb)
    if has_res:
        acc = acc + r_ref[...]
    o_ref[...] = acc.astype(o_ref.dtype)


def _matmul(x, w, w_idx, *, norm_g=None, g_idx=0, residual=None, out_dtype=F32, tn=512, col_off=0, n_cols=None,
            w_transposed=False):
    m, k = x.shape
    n_total = w.shape[1] if w_transposed else w.shape[2]
    n_cols = n_total if n_cols is None else n_cols
    tn = min(tn, n_cols)
    assert n_cols % tn == 0 and col_off % tn == 0
    tm = _largest_divisor(m, MM_MAX_ROWS, 16)
    stage_x = x.dtype != BF16 or norm_g is not None
    row_chunk = _largest_divisor(tm, 272, 16)
    cb = col_off // tn
    x_bytes = tm * k * x.dtype.itemsize
    x_mode = {} if 2 * x_bytes <= MM_X_DOUBLE_BUFFER_BYTES else {"pipeline_mode": pl.Buffered(1)}
    w_spec = (pl.BlockSpec((None, tn, k), lambda i, j: (w_idx, j + cb, 0)) if w_transposed
              else pl.BlockSpec((None, k, tn), lambda i, j: (w_idx, 0, j + cb)))
    in_specs = [pl.BlockSpec((tm, k), lambda i, j: (i, 0), **x_mode), w_spec]
    args = [x, w]
    if norm_g is not None:
        in_specs.append(pl.BlockSpec((None, 1, k), lambda i, j: (g_idx, 0, 0)))
        args.append(norm_g)
    if residual is not None:
        in_specs.append(pl.BlockSpec((tm, tn), lambda i, j: (i, j)))
        args.append(residual)
    body = functools.partial(_mm_body, has_norm=norm_g is not None, has_res=residual is not None,
                             stage_x=stage_x, tm=tm, row_chunk=row_chunk, w_transposed=w_transposed)
    return pl.pallas_call(
        body,
        out_shape=jax.ShapeDtypeStruct((m, n_cols), out_dtype),
        grid=(m // tm, n_cols // tn),
        in_specs=in_specs,
        out_specs=pl.BlockSpec((tm, tn), lambda i, j: (i, j)),
        scratch_shapes=[pltpu.VMEM((tm, k), BF16)] if stage_x else [],
        compiler_params=_cparams(2),
    )(*args)


def _rmsnorm_body(x_ref, g_ref, o_ref):
    x = x_ref[...]
    o_ref[...] = x * lax.rsqrt(jnp.mean(x * x, axis=-1, keepdims=True) + EPS) * g_ref[...]


def _rmsnorm(x, g, row0, n_rows):
    k = x.shape[1]
    tm = _largest_divisor(math.gcd(n_rows, row0) if row0 else n_rows, 512, 8)
    rb0 = row0 // tm
    return pl.pallas_call(
        _rmsnorm_body,
        out_shape=jax.ShapeDtypeStruct((n_rows, k), F32),
        grid=(n_rows // tm,),
        in_specs=[pl.BlockSpec((tm, k), lambda i: (rb0 + i, 0)), pl.BlockSpec((1, k), lambda i: (0, 0))],
        out_specs=pl.BlockSpec((tm, k), lambda i: (i, 0)),
        compiler_params=_cparams(1),
    )(x, g.reshape(1, k))


def _softmax_rows(s):
    e = jnp.exp(s - jnp.max(s, axis=-1, keepdims=True))
    return e / jnp.sum(e, axis=-1, keepdims=True)


def _xattn_prompt_body(q_ref, k_ref, v_ref, o_ref):
    kb = k_ref[...].astype(BF16)
    vb = v_ref[...].astype(BF16)
    for h in range(XA_HEADS):
        sl = slice(h * XA_HD, (h + 1) * XA_HD)
        p = _softmax_rows(_dot_nt(q_ref[:, sl], kb[:, sl]) * XA_HD ** -0.5)
        o_ref[:, sl] = _dot(p.astype(BF16), vb[:, sl]).astype(o_ref.dtype)


def _xattn_prompt(q, kv, bp, tp):
    m = q.shape[0]
    tq = min(tp, 512)
    nq = tp // tq
    return pl.pallas_call(
        _xattn_prompt_body,
        out_shape=jax.ShapeDtypeStruct((m, D_MODEL), BF16),
        grid=(bp, nq),
        in_specs=[pl.BlockSpec((tq, D_MODEL), lambda b, i: (b * nq + i, 0)),
                  pl.BlockSpec((MEM_LEN, D_MODEL), lambda b, i: (b, 0)),
                  pl.BlockSpec((MEM_LEN, D_MODEL), lambda b, i: (b, 1))],
        out_specs=pl.BlockSpec((tq, D_MODEL), lambda b, i: (b * nq + i, 0)),
        compiler_params=_cparams(2),
    )(q, kv, kv)


def _log2(n):
    assert n & (n - 1) == 0, n
    return n.bit_length() - 1


def _row_segment(rows, seg_len, width):
    return lax.broadcasted_iota(jnp.int32, (rows, width), 0) >> _log2(seg_len)


def _row_pos(rows, seg_len, width):
    return lax.broadcasted_iota(jnp.int32, (rows, width), 0) & (seg_len - 1)


XA_SNB = 2


def _xattn_sample_body(q_ref, k_ref, v_ref, oin_ref, o_ref, acc_ref):
    del oin_ref
    part = pl.program_id(1)
    rows = SAMPLE_NB * SAMPLE_T

    @pl.when(part == 0)
    def _():
        acc_ref[...] = jnp.zeros_like(acc_ref)

    lane_blocks = XA_HD // 128

    def head(ref, i, h):
        return jnp.concatenate([ref[i, pl.ds(j * XA_HEADS + h, MEM_LEN, stride=lane_blocks * XA_HEADS), :]
                                for j in range(lane_blocks)], axis=1).astype(BF16)

    pairs = [(i, h) for i in range(XA_SNB) for h in range(XA_HEADS)]
    s = jnp.concatenate([_dot_nt(q_ref[:, h * XA_HD:(h + 1) * XA_HD], head(k_ref, i, h)) for i, h in pairs], axis=0)
    p = _softmax_rows(s * XA_HD ** -0.5).astype(BF16)
    seg = _row_segment(rows, SAMPLE_T, XA_HD)
    for h in range(XA_HEADS):
        sl = slice(h * XA_HD, (h + 1) * XA_HD)
        o = acc_ref[:, sl]
        for i in range(XA_SNB):
            n = i * XA_HEADS + h
            o = jnp.where(seg == part * XA_SNB + i, _dot(p[n * rows:(n + 1) * rows], head(v_ref, i, h)), o)
        acc_ref[:, sl] = o

    @pl.when(part == pl.num_programs(1) - 1)
    def _():
        o_ref[...] = acc_ref[...].astype(o_ref.dtype)


def _xattn_sample(q, cache_k, cache_v, layer, o_all, rows_p, bs):
    rows = SAMPLE_NB * SAMPLE_T
    rb0 = rows_p // rows
    nparts = SAMPLE_NB // XA_SNB
    depth = cache_k.shape[0]
    lane_blocks = XA_HD // 128
    kv_rows = MEM_LEN * lane_blocks * XA_HEADS

    def relayout(c):
        c = c.reshape(depth, bs, MEM_LEN, XA_HEADS, lane_blocks, 128).transpose(0, 1, 2, 4, 3, 5)
        return c.reshape(depth, bs, kv_rows, 128)
    cache_k, cache_v = relayout(cache_k), relayout(cache_v)
    kv_spec = pl.BlockSpec((None, XA_SNB, kv_rows, 128), lambda bb, s: (layer, bb * nparts + s, 0, 0))
    return pl.pallas_call(
        _xattn_sample_body,
        out_shape=jax.ShapeDtypeStruct(o_all.shape, o_all.dtype),
        grid=(bs // SAMPLE_NB, nparts),
        in_specs=[pl.BlockSpec((rows, D_MODEL), lambda bb, s: (rb0 + bb, 0)), kv_spec, kv_spec,
                  pl.BlockSpec(memory_space=pl.ANY)],
        out_specs=pl.BlockSpec((rows, D_MODEL), lambda bb, s: (rb0 + bb, 0)),
        scratch_shapes=[pltpu.VMEM((rows, D_MODEL), F32)],
        input_output_aliases={3: 0},
        compiler_params=_cparams(2),
    )(q, cache_k, cache_v, o_all)


LRU_CW = 512
LRU_TC = 256


def _lru_gate_scan(xc, wa_ref, wx_ref, ba, bx, lam, h0, seg_len):
    rows = xc.shape[0]
    rs, gis = [], []
    for n in range(LRU_CW // LRU_BLOCK):
        xb = xc[:, n * LRU_BLOCK:(n + 1) * LRU_BLOCK].astype(BF16)
        rs.append(_dot(xb, wa_ref[n].astype(BF16)))
        gis.append(_dot(xb, wx_ref[n].astype(BF16)))
    r = jax.nn.sigmoid(jnp.concatenate(rs, axis=1) + ba)
    gi = jax.nn.sigmoid(jnp.concatenate(gis, axis=1) + bx)
    log_a = -LRU_C * r * jax.nn.softplus(-lam)
    a = jnp.exp(log_a)
    b = jnp.sqrt(-_expm1(2.0 * log_a)) * (gi * xc)
    t = _row_pos(rows, seg_len, LRU_CW)
    shift = 1
    while shift < seg_len:
        keep = t >= shift
        a_prev = pltpu.roll(a, shift, 0)
        b_prev = pltpu.roll(b, shift, 0)
        b = jnp.where(keep, a * b_prev + b, b)
        a = jnp.where(keep, a * a_prev, a)
        shift *= 2
    return b + a * h0


def _lru_prompt_body(x_ref, z_ref, cw_ref, cb_ref, wa_ref, wx_ref, ba_ref, bx_ref, lam_ref,
                     y_ref, hl_ref, xext_ref, hc_ref):
    c = pl.program_id(2)

    @pl.when(c == 0)
    def _():
        xext_ref[0:8, :] = jnp.zeros((8, LRU_CW), F32)
        hc_ref[...] = jnp.zeros_like(hc_ref)

    x = x_ref[...]
    xext_ref[8:8 + LRU_TC, :] = x
    xc = cw_ref[3:4, :] * x + cb_ref[...]
    for k in range(CONV_K - 1):
        xc = xc + cw_ref[k:k + 1, :] * xext_ref[5 + k:5 + k + LRU_TC, :]
    xext_ref[0:8, :] = x[LRU_TC - 8:LRU_TC, :]
    h = _lru_gate_scan(xc, wa_ref, wx_ref, ba_ref[...], bx_ref[...], lam_ref[...], hc_ref[0:1, :], LRU_TC)
    hc_ref[0:1, :] = h[LRU_TC - 1:LRU_TC, :]
    y_ref[...] = (h * jax.nn.silu(z_ref[...])).astype(y_ref.dtype)

    @pl.when(c == pl.num_programs(2) - 1)
    def _():
        hl_ref[...] = h[LRU_TC - 1:LRU_TC, :]


def _lru_param_specs(n_grid):
    def cmap(block):
        if n_grid == 3:
            return lambda b, cb, c: block(cb)
        return lambda cb: block(cb)
    return [pl.BlockSpec((None, CONV_K, LRU_CW), cmap(lambda cb: (0, 0, cb))),
            pl.BlockSpec((None, 1, LRU_CW), cmap(lambda cb: (0, 0, cb))),
            pl.BlockSpec((None, LRU_CW // LRU_BLOCK, LRU_BLOCK, LRU_BLOCK), cmap(lambda cb: (0, cb, 0, 0))),
            pl.BlockSpec((None, LRU_CW // LRU_BLOCK, LRU_BLOCK, LRU_BLOCK), cmap(lambda cb: (0, cb, 0, 0))),
            pl.BlockSpec((None, 1, LRU_CW), cmap(lambda cb: (0, 0, cb))),
            pl.BlockSpec((None, 1, LRU_CW), cmap(lambda cb: (0, 0, cb))),
            pl.BlockSpec((None, 1, LRU_CW), cmap(lambda cb: (0, 0, cb)))]


def _lru_params(conv_w, conv_b, wa, wx, ba, bx, lam):
    w = LRU_WIDTH
    return [conv_w, conv_b.reshape(1, 1, w), wa, wx, ba.reshape(1, 1, w), bx.reshape(1, 1, w), lam.reshape(1, 1, w)]


def _lru_prompt(xz, params, bp, tp):
    m = xz.shape[0]
    nc = tp // LRU_TC
    ncb = LRU_WIDTH // LRU_CW
    return pl.pallas_call(
        _lru_prompt_body,
        out_shape=(jax.ShapeDtypeStruct((m, LRU_WIDTH), BF16), jax.ShapeDtypeStruct((bp, 1, LRU_WIDTH), F32)),
        grid=(bp, ncb, nc),
        in_specs=[pl.BlockSpec((LRU_TC, LRU_CW), lambda b, cb, c: (b * nc + c, cb)),
                  pl.BlockSpec((LRU_TC, LRU_CW), lambda b, cb, c: (b * nc + c, ncb + cb))] + _lru_param_specs(3),
        out_specs=(pl.BlockSpec((LRU_TC, LRU_CW), lambda b, cb, c: (b * nc + c, cb)),
                   pl.BlockSpec((None, 1, LRU_CW), lambda b, cb, c: (b, 0, cb))),
        scratch_shapes=[pltpu.VMEM((LRU_TC + 8, LRU_CW), F32), pltpu.VMEM((8, LRU_CW), F32)],
        compiler_params=_cparams(3),
    )(xz, xz, *params)


def _lru_sample_body(x_ref, z_ref, p1_ref, p2_ref, p3_ref, h0_ref, cw_ref, cb_ref, wa_ref, wx_ref,
                     ba_ref, bx_ref, lam_ref, yin_ref, y_ref, h_ref):
    del yin_ref
    x = x_ref[...]
    rows = x.shape[0]
    t = _row_pos(rows, SAMPLE_T, LRU_CW)
    xc = cw_ref[3:4, :] * x + cb_ref[...]
    for k, prev_ref in ((1, p1_ref), (2, p2_ref), (3, p3_ref)):
        xc = xc + cw_ref[3 - k:4 - k, :] * jnp.where(t >= k, pltpu.roll(x, k, 0), prev_ref[...])
    h = _lru_gate_scan(xc, wa_ref, wx_ref, ba_ref[...], bx_ref[...], lam_ref[...], h0_ref[...], SAMPLE_T)
    h_ref[...] = h
    y_ref[...] = (h * jax.nn.silu(z_ref[...])).astype(y_ref.dtype)


def _lru_sample(xz, prevs, h0_rows, params, y_all, rows_p):
    rows_s = prevs[0].shape[0]
    rb0 = rows_p // rows_s
    ncb = LRU_WIDTH // LRU_CW
    small = pl.BlockSpec((rows_s, LRU_CW), lambda cb: (0, cb))
    return pl.pallas_call(
        _lru_sample_body,
        out_shape=(jax.ShapeDtypeStruct(y_all.shape, y_all.dtype), jax.ShapeDtypeStruct((rows_s, LRU_WIDTH), F32)),
        grid=(ncb,),
        in_specs=[pl.BlockSpec((rows_s, LRU_CW), lambda cb: (rb0, cb)),
                  pl.BlockSpec((rows_s, LRU_CW), lambda cb: (rb0, ncb + cb)),
                  small, small, small, small] + _lru_param_specs(1) + [pl.BlockSpec(memory_space=pl.ANY)],
        out_specs=(pl.BlockSpec((rows_s, LRU_CW), lambda cb: (rb0, cb)), small),
        input_output_aliases={13: 0},
        compiler_params=_cparams(1),
    )(xz, xz, *prevs, h0_rows, *params, y_all)


def _conv_prev_rows(buf):
    b, _, c = buf.shape
    out = []
    for k in range(1, CONV_K):
        pad = jnp.zeros((b, SAMPLE_T - k, c), buf.dtype)
        out.append(jnp.concatenate([buf[:, CONV_K - 1 - k:], pad], axis=1).reshape(b * SAMPLE_T, c))
    return out


def _rope(x, cos, sin):
    half = RET_DK // 2
    x1, x2 = x[:, :half], x[:, half:]
    return jnp.concatenate([x1 * cos - x2 * sin, x1 * sin + x2 * cos], axis=1)


def _ret_chunk(q, k, v, z, cos, sin, dmask, qdec, kdec, cdec, gn, states, seg_len):
    rows = q.shape[0]
    qb = _rope(q, cos, sin).astype(BF16)
    kr = _rope(k, cos, sin) * RET_DK ** -0.5
    kb = kr.astype(BF16)
    vb = v.astype(BF16)
    kd = kr * kdec
    o = _dot((_dot_nt(qb, kb) * dmask).astype(BF16), vb)
    new_states = []
    single = len(states) == 1
    seg_v = None if single else _row_segment(rows, seg_len, RET_DV)
    seg_k = None if single else _row_segment(rows, seg_len, RET_DK)
    for i, s in enumerate(states):
        cross = _dot(qb, s.astype(BF16)) * qdec
        kdi = kd
        if not single:
            cross = jnp.where(seg_v == i, cross, 0.0)
            kdi = jnp.where(seg_k == i, kd, 0.0)
        o = o + cross
        new_states.append(s * cdec + _dot_tn(kdi.astype(BF16), vb))
    mu = jnp.mean(o, axis=-1, keepdims=True)
    var = jnp.mean(jnp.square(o - mu), axis=-1, keepdims=True)
    on = (o - mu) * lax.rsqrt(var + EPS) * gn
    return on * jax.nn.silu(z), new_states


def _ret_prompt_body(q_ref, k_ref, v_ref, z_ref, cos_ref, sin_ref, dm_ref, qd_ref, kd_ref, cd_ref, gn_ref,
                     y_ref, sl_ref, s_ref):
    c = pl.program_id(1)

    @pl.when(c == 0)
    def _():
        s_ref[...] = jnp.zeros_like(s_ref)

    cos, sin = cos_ref[...], sin_ref[...]
    for h in range(RET_HEADS):
        ksl = slice(h * RET_DK, (h + 1) * RET_DK)
        vsl = slice(h * RET_DV, (h + 1) * RET_DV)
        y, (s_new,) = _ret_chunk(q_ref[:, ksl], k_ref[:, ksl], v_ref[:, vsl], z_ref[:, vsl], cos, sin,
                                 dm_ref[h], qd_ref[h], kd_ref[h], cd_ref[h], gn_ref[:, vsl], [s_ref[h]], CHUNK)
        s_ref[h] = s_new
        y_ref[:, vsl] = y.astype(y_ref.dtype)

        @pl.when(c == pl.num_programs(1) - 1)
        def _():
            sl_ref[h] = s_new


def _ret_tables(seg_len, nseg, pos):
    rows = seg_len * nseg
    log_g = jnp.log1p(-jnp.exp2(-5.0 - jnp.arange(RET_HEADS, dtype=F32)))[:, None, None]
    t = (jnp.arange(rows) % seg_len).astype(F32)
    seg = jnp.arange(rows) // seg_len
    rel = t[:, None] - t[None, :]
    ok = (rel >= 0) & (seg[:, None] == seg[None, :])
    dmask = jnp.where(ok, jnp.exp(log_g * jnp.where(ok, rel, 0.0)), 0.0)
    qdec = jnp.broadcast_to(jnp.exp(log_g * (t + 1.0)[None, :, None]), (RET_HEADS, rows, RET_DV))
    kdec = jnp.broadcast_to(jnp.exp(log_g * (seg_len - 1.0 - t)[None, :, None]), (RET_HEADS, rows, RET_DK))
    cdec = jnp.broadcast_to(jnp.exp(log_g * seg_len), (RET_HEADS, 1, RET_DV))
    half = RET_DK // 2
    inv = ROPE_BASE ** (-jnp.arange(half, dtype=F32) / half)
    ang = pos.astype(F32)[:, None] * inv
    return jnp.cos(ang), jnp.sin(ang), dmask, qdec, kdec, cdec


def _ret_prompt(qkvz, gn, bp, tp):
    m = qkvz.shape[0]
    nc = tp // CHUNK
    cos, sin, dmask, qdec, kdec, cdec = _ret_tables(CHUNK, 1, jnp.arange(tp))
    qk_w = RET_HEADS * RET_DK
    full = lambda b, c: (0, 0, 0)
    return pl.pallas_call(
        _ret_prompt_body,
        out_shape=(jax.ShapeDtypeStruct((m, RET_VW), BF16),
                   jax.ShapeDtypeStruct((bp, RET_HEADS, RET_DK, RET_DV), F32)),
        grid=(bp, nc),
        in_specs=[pl.BlockSpec((CHUNK, qk_w), lambda b, c: (b * nc + c, 0)),
                  pl.BlockSpec((CHUNK, qk_w), lambda b, c: (b * nc + c, 1)),
                  pl.BlockSpec((CHUNK, RET_VW), lambda b, c: (b * nc + c, 1)),
                  pl.BlockSpec((CHUNK, RET_VW), lambda b, c: (b * nc + c, 2)),
                  pl.BlockSpec((CHUNK, RET_DK // 2), lambda b, c: (c, 0)),
                  pl.BlockSpec((CHUNK, RET_DK // 2), lambda b, c: (c, 0)),
                  pl.BlockSpec((RET_HEADS, CHUNK, CHUNK), full),
                  pl.BlockSpec((RET_HEADS, CHUNK, RET_DV), full),
                  pl.BlockSpec((RET_HEADS, CHUNK, RET_DK), full),
                  pl.BlockSpec((RET_HEADS, 1, RET_DV), full),
                  pl.BlockSpec((1, RET_VW), lambda b, c: (0, 0))],
        out_specs=(pl.BlockSpec((CHUNK, RET_VW), lambda b, c: (b * nc + c, 0)),
                   pl.BlockSpec((None, RET_HEADS, RET_DK, RET_DV), lambda b, c: (b, 0, 0, 0))),
        scratch_shapes=[pltpu.VMEM((RET_HEADS, RET_DK, RET_DV), F32)],
        compiler_params=_cparams(2),
    )(qkvz, qkvz, qkvz, qkvz, cos, sin, dmask, qdec, kdec, cdec, gn.reshape(1, RET_VW))


def _ret_sample_body(q_ref, k_ref, v_ref, z_ref, cos_ref, sin_ref, dm_ref, qd_ref, kd_ref, cd_ref, gn_ref,
                     s0_ref, yin_ref, y_ref, s_ref):
    del yin_ref
    y, s_new = _ret_chunk(q_ref[...], k_ref[...], v_ref[...], z_ref[...], cos_ref[...], sin_ref[...],
                          dm_ref[...], qd_ref[...], kd_ref[...], cd_ref[...], gn_ref[...],
                          [s0_ref[i] for i in range(SAMPLE_NB)], SAMPLE_T)
    for i in range(SAMPLE_NB):
        s_ref[i] = s_new[i]
    y_ref[...] = y.astype(y_ref.dtype)


def _ret_sample(qkvz, gn, state, y_all, rows_p, bs):
    rows = SAMPLE_NB * SAMPLE_T
    rb0 = rows_p // rows
    pos = PAST_LEN + jnp.arange(rows) % SAMPLE_T
    cos, sin, dmask, qdec, kdec, cdec = _ret_tables(SAMPLE_T, SAMPLE_NB, pos)
    kb0 = RET_HEADS
    vb0 = 2 * RET_HEADS * RET_DK // RET_DV
    zb0 = vb0 + RET_HEADS
    st_spec = pl.BlockSpec((None, SAMPLE_NB, None, RET_DK, RET_DV), lambda bb, h: (0, bb, h, 0, 0))
    return pl.pallas_call(
        _ret_sample_body,
        out_shape=(jax.ShapeDtypeStruct(y_all.shape, y_all.dtype), jax.ShapeDtypeStruct(state.shape, F32)),
        grid=(bs // SAMPLE_NB, RET_HEADS),
        in_specs=[pl.BlockSpec((rows, RET_DK), lambda bb, h: (rb0 + bb, h)),
                  pl.BlockSpec((rows, RET_DK), lambda bb, h: (rb0 + bb, kb0 + h)),
                  pl.BlockSpec((rows, RET_DV), lambda bb, h: (rb0 + bb, vb0 + h)),
                  pl.BlockSpec((rows, RET_DV), lambda bb, h: (rb0 + bb, zb0 + h)),
                  pl.BlockSpec((rows, RET_DK // 2), lambda bb, h: (0, 0)),
                  pl.BlockSpec((rows, RET_DK // 2), lambda bb, h: (0, 0)),
                  pl.BlockSpec((None, rows, rows), lambda bb, h: (h, 0, 0)),
                  pl.BlockSpec((None, rows, RET_DV), lambda bb, h: (h, 0, 0)),
                  pl.BlockSpec((None, rows, RET_DK), lambda bb, h: (h, 0, 0)),
                  pl.BlockSpec((None, 1, RET_DV), lambda bb, h: (h, 0, 0)),
                  pl.BlockSpec((1, RET_DV), lambda bb, h: (0, h)),
                  st_spec,
                  pl.BlockSpec(memory_space=pl.ANY)],
        out_specs=(pl.BlockSpec((rows, RET_DV), lambda bb, h: (rb0 + bb, h)), st_spec),
        input_output_aliases={12: 0},
        compiler_params=_cparams(2),
    )(qkvz, qkvz, qkvz, qkvz, cos, sin, dmask, qdec, kdec, cdec, gn.reshape(1, RET_VW), state, y_all)


def _head_expand_matrix():
    r = lax.broadcasted_iota(jnp.int32, (128, SSD_GW), 0)
    c = lax.broadcasted_iota(jnp.int32, (128, SSD_GW), 1)
    return jnp.where(r == (c >> _log2(SSD_HEADDIM)), 1.0, 0.0).astype(BF16)


def _expand_heads(v, expand_mat):
    v1 = v.astype(BF16)
    r1 = v - v1.astype(F32)
    v2 = r1.astype(BF16)
    v3 = (r1 - v2.astype(F32)).astype(BF16)
    return (_dot(v1, expand_mat) + _dot(v2, expand_mat)) + _dot(v3, expand_mat)


def _ssd_chunk(xs, bs, cs, z, dt_raw, dt_bias, a_log, d_skip, norm_g, states, seg_len, expand_mat):
    rows = xs.shape[0]
    single = len(states) == 1
    hp = SSD_HPG
    pad = 128
    t8 = _row_pos(rows, seg_len, pad)
    seg8 = _row_segment(rows, seg_len, pad)
    tt = lax.broadcasted_iota(jnp.int32, (rows, rows), 0)
    ss = lax.broadcasted_iota(jnp.int32, (rows, rows), 1)
    causal = (tt >= ss) & ((tt >> _log2(seg_len)) == (ss >> _log2(seg_len)))

    def lane_pad(v):
        return jnp.concatenate([v, jnp.zeros((v.shape[0], pad - hp), F32)], axis=1)

    dt = jax.nn.softplus(lane_pad(dt_raw) + lane_pad(dt_bias))
    da = dt * (-jnp.exp(lane_pad(a_log)))
    cum = da
    shift = 1
    while shift < seg_len:
        cum = cum + jnp.where(t8 >= shift, pltpu.roll(cum, shift, 0), 0.0)
        shift *= 2
    cum_sq = cum if rows == pad else jnp.concatenate([cum, jnp.zeros((pad - rows, pad), F32)], axis=0)
    cum_t = cum_sq.T[0:hp, 0:rows]
    lasts = [cum[(i + 1) * seg_len - 1:(i + 1) * seg_len, :] for i in range(len(states))]
    last_row = lasts[0]
    if not single:
        last_row = jnp.zeros((rows, pad), F32)
        for i, l in enumerate(lasts):
            last_row = jnp.where(seg8 == i, l, last_row)
    to_end = jnp.exp(last_row - cum)
    expanded = _expand_heads(jnp.concatenate([dt, jnp.exp(cum), to_end], axis=0), expand_mat)
    dt_x, ecum, to_end_x = expanded[0:rows], expanded[rows:2 * rows], expanded[2 * rows:3 * rows]

    csb = cs.astype(BF16)
    bsb = bs.astype(BF16)
    cb = _dot_nt(csb, bsb)
    xdt = xs * dt_x
    ycols = []
    for r in range(hp):
        lmat = jnp.where(causal, jnp.exp(cum[:, r:r + 1] - cum_t[r:r + 1, :]), 0.0)
        ycols.append(_dot((cb * lmat).astype(BF16), xdt[:, r * SSD_HEADDIM:(r + 1) * SSD_HEADDIM].astype(BF16)))
    y = jnp.concatenate(ycols, axis=1)
    xte = xdt * to_end_x
    segw = None if single else _row_segment(rows, seg_len, SSD_GW)
    new_states = []
    for i, h in enumerate(states):
        y_off = _dot_nt(csb, h.astype(BF16)) * ecum
        xi = xte
        if not single:
            y_off = jnp.where(segw == i, y_off, 0.0)
            xi = jnp.where(segw == i, xte, 0.0)
        y = y + y_off
        e_last = jnp.exp(lasts[i])
        dec = jnp.concatenate([jnp.broadcast_to(e_last[:, r:r + 1], (SSD_HEADDIM, SSD_STATE)) for r in range(hp)],
                              axis=0)
        new_states.append(h * dec + _dot_tn(xi.astype(BF16), bsb))
    y = y + xs * _expand_cols(d_skip, SSD_HEADDIM)
    yg = y * jax.nn.silu(z)
    yg = yg * lax.rsqrt(jnp.mean(yg * yg, axis=-1, keepdims=True) + EPS)
    return yg * norm_g, new_states


def _ssd_prompt_body(z_ref, x_ref, b_ref, c_ref, dt_ref, wx_ref, wb_ref, wc_ref, bx_ref, bb_ref, bc_ref,
                     dtb_ref, al_ref, ds_ref, ng_ref, y_ref, hl_ref, xe_ref, be_ref, ce_ref, h_ref):
    c = pl.program_id(1)

    @pl.when(c == 0)
    def _():
        xe_ref[0:8, :] = jnp.zeros((8, SSD_INNER), F32)
        be_ref[0:8, :] = jnp.zeros((8, SSD_GROUPS * SSD_STATE), F32)
        ce_ref[0:8, :] = jnp.zeros((8, SSD_GROUPS * SSD_STATE), F32)
        h_ref[...] = jnp.zeros_like(h_ref)

    def conv(raw_ref, ext_ref, w_ref, bias_ref, sl):
        raw = raw_ref[:, sl]
        ext_ref[8:8 + CHUNK, sl] = raw
        acc = w_ref[3:4, sl] * raw + bias_ref[:, sl]
        for k in range(CONV_K - 1):
            acc = acc + w_ref[k:k + 1, sl] * ext_ref[5 + k:5 + k + CHUNK, sl]
        ext_ref[0:8, sl] = raw[CHUNK - 8:CHUNK, :]
        return jax.nn.silu(acc)

    expand_mat = _head_expand_matrix()
    for g in range(SSD_GROUPS):
        xsl = slice(g * SSD_GW, (g + 1) * SSD_GW)
        nsl = slice(g * SSD_STATE, (g + 1) * SSD_STATE)
        xs = conv(x_ref, xe_ref, wx_ref, bx_ref, xsl)
        bs = conv(b_ref, be_ref, wb_ref, bb_ref, nsl)
        cs = conv(c_ref, ce_ref, wc_ref, bc_ref, nsl)
        y, (h_new,) = _ssd_chunk(xs, bs, cs, z_ref[:, xsl], dt_ref[g], dtb_ref[g], al_ref[g], ds_ref[g],
                                 ng_ref[:, xsl], [h_ref[g]], CHUNK, expand_mat)
        h_ref[g] = h_new
        y_ref[:, xsl] = y.astype(y_ref.dtype)

        @pl.when(c == pl.num_programs(1) - 1)
        def _():
            hl_ref[g] = h_new


def _ssd_col_blocks():
    xb0 = SSD_INNER // SSD_GW
    bb0 = (2 * SSD_INNER) // SSD_STATE
    cb0 = bb0 + SSD_GROUPS
    return xb0, bb0, cb0


def _ssd_param_arrays(conv_w, conv_b, dt_bias, a_log, d_skip, norm_g):
    g, hp = SSD_GROUPS, SSD_HPG
    return [conv_w, conv_w, conv_w, conv_b.reshape(1, 1, -1), conv_b.reshape(1, 1, -1), conv_b.reshape(1, 1, -1),
            dt_bias.reshape(g, 1, hp), a_log.reshape(g, 1, hp), d_skip.reshape(g, 1, hp), norm_g.reshape(1, SSD_INNER)]


def _ssd_param_specs(gmap):
    wxb0 = 0
    wbb0 = SSD_INNER // SSD_STATE
    wcb0 = wbb0 + SSD_GROUPS
    return [pl.BlockSpec((None, CONV_K, SSD_GW), gmap(lambda g: (0, 0, wxb0 + g))),
            pl.BlockSpec((None, CONV_K, SSD_STATE), gmap(lambda g: (0, 0, wbb0 + g))),
            pl.BlockSpec((None, CONV_K, SSD_STATE), gmap(lambda g: (0, 0, wcb0 + g))),
            pl.BlockSpec((None, 1, SSD_GW), gmap(lambda g: (0, 0, wxb0 + g))),
            pl.BlockSpec((None, 1, SSD_STATE), gmap(lambda g: (0, 0, wbb0 + g))),
            pl.BlockSpec((None, 1, SSD_STATE), gmap(lambda g: (0, 0, wcb0 + g))),
            pl.BlockSpec((None, 1, SSD_HPG), gmap(lambda g: (g, 0, 0))),
            pl.BlockSpec((None, 1, SSD_HPG), gmap(lambda g: (g, 0, 0))),
            pl.BlockSpec((None, 1, SSD_HPG), gmap(lambda g: (g, 0, 0))),
            pl.BlockSpec((1, SSD_GW), gmap(lambda g: (0, g)))]


def _ssd_prompt(zx, dtg, params, bp, tp):
    m = zx.shape[0]
    nc = tp // CHUNK
    gn = SSD_GROUPS * SSD_STATE
    b_blk = 2 * SSD_INNER // gn
    wb_blk = SSD_INNER // gn
    hp3 = (SSD_GROUPS, 1, SSD_HPG)
    zero3 = lambda b, c: (0, 0, 0)
    return pl.pallas_call(
        _ssd_prompt_body,
        out_shape=(jax.ShapeDtypeStruct((m, SSD_INNER), BF16),
                   jax.ShapeDtypeStruct((bp, SSD_GROUPS, SSD_GW, SSD_STATE), F32)),
        grid=(bp, nc),
        in_specs=[pl.BlockSpec((CHUNK, SSD_INNER), lambda b, c: (b * nc + c, 0)),
                  pl.BlockSpec((CHUNK, SSD_INNER), lambda b, c: (b * nc + c, 1)),
                  pl.BlockSpec((CHUNK, gn), lambda b, c: (b * nc + c, b_blk)),
                  pl.BlockSpec((CHUNK, gn), lambda b, c: (b * nc + c, b_blk + 1)),
                  pl.BlockSpec((SSD_GROUPS, CHUNK, SSD_HPG), lambda b, c: (0, b * nc + c, 0)),
                  pl.BlockSpec((None, CONV_K, SSD_INNER), zero3),
                  pl.BlockSpec((None, CONV_K, gn), lambda b, c: (0, 0, wb_blk)),
                  pl.BlockSpec((None, CONV_K, gn), lambda b, c: (0, 0, wb_blk + 1)),
                  pl.BlockSpec((None, 1, SSD_INNER), zero3),
                  pl.BlockSpec((None, 1, gn), lambda b, c: (0, 0, wb_blk)),
                  pl.BlockSpec((None, 1, gn), lambda b, c: (0, 0, wb_blk + 1)),
                  pl.BlockSpec(hp3, zero3), pl.BlockSpec(hp3, zero3), pl.BlockSpec(hp3, zero3),
                  pl.BlockSpec((1, SSD_INNER), lambda b, c: (0, 0))],
        out_specs=(pl.BlockSpec((CHUNK, SSD_INNER), lambda b, c: (b * nc + c, 0)),
                   pl.BlockSpec((None, SSD_GROUPS, SSD_GW, SSD_STATE), lambda b, c: (b, 0, 0, 0))),
        scratch_shapes=[pltpu.VMEM((CHUNK + 8, SSD_INNER), F32), pltpu.VMEM((CHUNK + 8, gn), F32),
                        pltpu.VMEM((CHUNK + 8, gn), F32), pltpu.VMEM((SSD_GROUPS, SSD_GW, SSD_STATE), F32)],
        compiler_params=_cparams(2),
    )(zx, zx, zx, zx, dtg, *params)


def _ssd_sample_body(z_ref, x_ref, b_ref, c_ref, dt_ref, px_ref, pb_ref, pc_ref, wx_ref, wb_ref, wc_ref,
                     bx_ref, bb_ref, bc_ref, dtb_ref, al_ref, ds_ref, ng_ref, h0_ref, yin_ref,
                     y_ref, h_ref, nx_ref, nb_ref, nc_ref):
    del yin_ref

    def conv(raw_ref, buf_ref, new_ref, w_ref, bias_ref):
        raw = raw_ref[...]
        rows, width = raw.shape
        t = _row_pos(rows, SAMPLE_T, width)
        acc = w_ref[3:4, :] * raw + bias_ref[...]
        for k in range(1, CONV_K):
            prev = jnp.concatenate([piece for i in range(SAMPLE_NB) for piece in
                                    (buf_ref[i, CONV_K - 1 - k:CONV_K - 1, :], jnp.zeros((SAMPLE_T - k, width), F32))],
                                   axis=0)
            acc = acc + w_ref[3 - k:4 - k, :] * jnp.where(t >= k, pltpu.roll(raw, k, 0), prev)
        for i in range(SAMPLE_NB):
            new_ref[i] = raw[i * SAMPLE_T + SAMPLE_T - (CONV_K - 1):(i + 1) * SAMPLE_T, :]
        return jax.nn.silu(acc)

    xs = conv(x_ref, px_ref, nx_ref, wx_ref, bx_ref)
    bs = conv(b_ref, pb_ref, nb_ref, wb_ref, bb_ref)
    cs = conv(c_ref, pc_ref, nc_ref, wc_ref, bc_ref)
    y, h_new = _ssd_chunk(xs, bs, cs, z_ref[...], dt_ref[...], dtb_ref[...], al_ref[...], ds_ref[...],
                          ng_ref[...], [h0_ref[i] for i in range(SAMPLE_NB)], SAMPLE_T, _head_expand_matrix())
    for i in range(SAMPLE_NB):
        h_ref[i] = h_new[i]
    y_ref[...] = y.astype(y_ref.dtype)


def _ssd_sample(zx, dtg, conv_state, params, state, y_all, rows_p, bs):
    rows = SAMPLE_NB * SAMPLE_T
    rb0 = rows_p // rows
    xb0, bb0, cb0 = _ssd_col_blocks()
    wbb0 = SSD_INNER // SSD_STATE
    wcb0 = wbb0 + SSD_GROUPS
    gn = SSD_GROUPS * SSD_STATE
    nk = CONV_K - 1

    def gmap(f):
        return lambda bb, g: f(g)
    st_spec = pl.BlockSpec((None, SAMPLE_NB, None, SSD_GW, SSD_STATE), lambda bb, g: (0, bb, g, 0, 0))
    new_x_spec = pl.BlockSpec((SAMPLE_NB, nk, SSD_GW), lambda bb, g: (bb, 0, g))
    new_n_spec = pl.BlockSpec((SAMPLE_NB, nk, SSD_STATE), lambda bb, g: (bb, 0, g))
    return pl.pallas_call(
        _ssd_sample_body,
        out_shape=(jax.ShapeDtypeStruct(y_all.shape, y_all.dtype), jax.ShapeDtypeStruct(state.shape, F32),
                   jax.ShapeDtypeStruct((bs, nk, SSD_INNER), F32), jax.ShapeDtypeStruct((bs, nk, gn), F32),
                   jax.ShapeDtypeStruct((bs, nk, gn), F32)),
        grid=(bs // SAMPLE_NB, SSD_GROUPS),
        in_specs=[pl.BlockSpec((rows, SSD_GW), lambda bb, g: (rb0 + bb, g)),
                  pl.BlockSpec((rows, SSD_GW), lambda bb, g: (rb0 + bb, xb0 + g)),
                  pl.BlockSpec((rows, SSD_STATE), lambda bb, g: (rb0 + bb, bb0 + g)),
                  pl.BlockSpec((rows, SSD_STATE), lambda bb, g: (rb0 + bb, cb0 + g)),
                  pl.BlockSpec((None, rows, SSD_HPG), lambda bb, g: (g, rb0 + bb, 0)),
                  pl.BlockSpec((None, SAMPLE_NB, nk, SSD_GW), lambda bb, g: (0, bb, 0, g)),
                  pl.BlockSpec((None, SAMPLE_NB, nk, SSD_STATE), lambda bb, g: (0, bb, 0, wbb0 + g)),
                  pl.BlockSpec((None, SAMPLE_NB, nk, SSD_STATE), lambda bb, g: (0, bb, 0, wcb0 + g))]
                 + _ssd_param_specs(gmap) + [st_spec, pl.BlockSpec(memory_space=pl.ANY)],
        out_specs=(pl.BlockSpec((rows, SSD_GW), lambda bb, g: (rb0 + bb, g)), st_spec,
                   new_x_spec, new_n_spec, new_n_spec),
        input_output_aliases={19: 0},
        compiler_params=_cparams(2),
    )(zx, zx, zx, zx, dtg, conv_state, conv_state, conv_state, *params, state, y_all)


def _s5_body(*refs, sub, nsub, n_scan, has_h0, has_alias):
    it = iter(refs)
    x_ref, wdc_ref, kdc_ref, vtc_ref = next(it), next(it), next(it), next(it)
    are_ref, aim_ref, sre_ref, sim_ref, dsk_ref = (next(it) for _ in range(5))
    h0_ref = next(it) if has_h0 else None
    if has_alias:
        next(it)
    s_ref, hl_ref = next(it), next(it)
    m1_ref, vt_ref = next(it), next(it)
    hw2 = 2 * S5_HW

    @pl.when(pl.program_id(1) == 0)
    def _():
        def iota(shape, axis):
            return lax.broadcasted_iota(jnp.int32, shape, axis)

        def expand(compact, spread, keep):
            return jnp.where(keep, _dot(compact.astype(BF16), spread), 0.0).astype(BF16)

        q, col = iota((128, hw2), 0), iota((128, hw2), 1)
        spread = jnp.where(((q >> 6) == (col >> 9)) & ((q & 63) == (col & 63)), 1.0, 0.0).astype(BF16)
        row, col = iota((sub * 128, hw2), 0), iota((sub * 128, hw2), 1)
        m1_ref[:, 0:hw2] = expand(wdc_ref[...], spread, ((row >> 4) & 7) == ((col >> 6) & 7))
        q, col = iota((128, 128), 0), iota((128, 128), 1)
        spread = jnp.where((q < S5_GROUP) & (q == (col & 15)), 1.0, 0.0).astype(BF16)
        keep = (q >> 4) == (col >> 4)
        kd = [expand(kdc_ref[d], spread, keep) for d in range(sub)]
        for s in range(sub):
            for t in range(sub):
                blk = kd[t - s] if t >= s else jnp.zeros((128, 128), BF16)
                m1_ref[s * 128:(s + 1) * 128, hw2 + t * 128:hw2 + (t + 1) * 128] = blk
        q, col = iota((128, sub * 128), 0), iota((128, sub * 128), 1)
        spread = jnp.where(((q >> 4) == (col >> 7)) & ((q & 15) == (col & 15)), 1.0, 0.0).astype(BF16)
        row, col = iota((hw2, sub * 128), 0), iota((hw2, sub * 128), 1)
        vt_ref[...] = expand(vtc_ref[...], spread, ((row >> 6) & 7) == ((col >> 4) & 7))

    xs = [x_ref[pl.ds(s, nsub, stride=sub), :] for s in range(sub)]
    r = _dot(jnp.concatenate(xs, axis=1).astype(BF16), m1_ref[...])
    cb, y = r[:, :2 * S5_HW], r[:, 2 * S5_HW:]

    def cmul(h, a_re, a_im):
        return h * a_re + pltpu.roll(h, S5_HW, 1) * a_im

    if has_h0:
        hprev = h0_ref[...]
        h = cb + cmul(hprev, are_ref[...], aim_ref[...])
    else:
        h = cb
        j = lax.broadcasted_iota(jnp.int32, h.shape, 0)
        for step in range(n_scan):
            shift = 1 << step
            prev = jnp.where(j >= shift, pltpu.roll(h, shift, 0), 0.0)
            h = h + cmul(prev, sre_ref[step], sim_ref[step])
        hprev = jnp.where(j >= 1, pltpu.roll(h, 1, 0), 0.0)
    hl_ref[...] = h[nsub - 1:nsub, :] if not has_h0 else h
    y = y + _dot(hprev.astype(BF16), vt_ref[...])
    for t in range(sub):
        s_ref[pl.ds(t, nsub, stride=sub), :] = y[:, t * 128:(t + 1) * 128] + dsk_ref[...] * xs[t]


def _s5_tables(lam_re, lam_im, b_re, b_im, c_re, c_im, log_dt, sub, n_scan):
    nb, g8, gc, p = S5_NBLK, S5_G8, S5_GROUP, S5_STATE
    dt = jnp.exp(log_dt)[:, None]

    def apow(ds):
        d = jnp.asarray(ds, F32).reshape(-1, 1, 1)
        mag = jnp.exp(lam_re * dt * d)
        return mag * jnp.cos(lam_im * dt * d), mag * jnp.sin(lam_im * dt * d)

    a_re, a_im = (v[0] for v in apow([1.0]))
    den = lam_re * lam_re + lam_im * lam_im
    f_re = ((a_re - 1.0) * lam_re + a_im * lam_im) / den
    f_im = (a_im * lam_re - (a_re - 1.0) * lam_im) / den
    bb_re = f_re[..., None] * b_re - f_im[..., None] * b_im
    bb_im = f_re[..., None] * b_im + f_im[..., None] * b_re

    def lane_pad(m):
        return jnp.pad(m, [(0, 0)] * (m.ndim - 1) + [(0, 128 - m.shape[-1])])

    p_re, p_im = apow(range(sub))
    ab_re = p_re[..., None] * bb_re - p_im[..., None] * bb_im
    ab_im = p_re[..., None] * bb_im + p_im[..., None] * bb_re
    kd = jnp.einsum('gcp,dgpk->dgkc', c_re, ab_re) - jnp.einsum('gcp,dgpk->dgkc', c_im, ab_im)
    kdc = lane_pad(kd.reshape(sub, nb, g8 * gc, gc).transpose(1, 0, 2, 3))
    w = jnp.concatenate([jnp.swapaxes(ab_re, 2, 3), jnp.swapaxes(ab_im, 2, 3)], axis=3)[::-1]
    wdc = w.reshape(sub, nb, g8 * gc, 2 * p).transpose(1, 0, 2, 3).reshape(nb, sub * g8 * gc, 2 * p)
    q_re, q_im = apow(range(1, sub + 1))
    m_re = c_re[None] * q_re[:, :, None, :] - c_im[None] * q_im[:, :, None, :]
    m_im = c_re[None] * q_im[:, :, None, :] + c_im[None] * q_re[:, :, None, :]

    def rows(m):
        return m.reshape(sub, nb, g8, gc, p).transpose(1, 2, 4, 0, 3).reshape(nb, g8 * p, sub * gc)
    vtc = lane_pad(jnp.concatenate([rows(m_re), -rows(m_im)], axis=1))

    def lanes(re, im):
        n = re.shape[0]
        re = re.reshape(n, nb, 1, S5_HW).transpose(1, 0, 2, 3)
        im = im.reshape(n, nb, 1, S5_HW).transpose(1, 0, 2, 3)
        return jnp.concatenate([re, re], axis=3), jnp.concatenate([-im, im], axis=3)

    are, aim = (v[:, 0] for v in lanes(*apow([float(sub)])))
    sre, sim = lanes(*apow([float(sub * (1 << k)) for k in range(max(n_scan, 1))]))
    return wdc, kdc, vtc, are, aim, sre, sim


def _s5_core(xz, tables, d_skip, s_all, h0, *, sub, nsub, n_seq, row_block0, n_scan):
    has_h0 = h0 is not None
    rows = sub * nsub
    ns = tables[5].shape[1]
    hw2 = 2 * S5_HW
    sw = sub * 128

    def blk(shape, f):
        return pl.BlockSpec(shape, lambda gb, b: f(gb, b))
    in_specs = [blk((rows, 128), lambda gb, b: (row_block0 + b, gb)),
                blk((None, sw, 128), lambda gb, b: (gb, 0, 0)),
                blk((None, sub, 128, 128), lambda gb, b: (gb, 0, 0, 0)),
                blk((None, hw2, 128), lambda gb, b: (gb, 0, 0)),
                blk((None, 1, hw2), lambda gb, b: (gb, 0, 0)),
                blk((None, 1, hw2), lambda gb, b: (gb, 0, 0)),
                blk((None, ns, 1, hw2), lambda gb, b: (gb, 0, 0, 0)),
                blk((None, ns, 1, hw2), lambda gb, b: (gb, 0, 0, 0)),
                blk((1, 128), lambda gb, b: (0, gb))]
    args = [xz, *tables, d_skip.reshape(1, -1)]
    if has_h0:
        in_specs.append(blk((None, nsub, hw2), lambda gb, b: (gb, 0, 0)))
        args.append(h0)
        hl_shape = (S5_NBLK, nsub, hw2)
        hl_spec = blk((None, nsub, hw2), lambda gb, b: (gb, 0, 0))
    else:
        hl_shape = (S5_NBLK, n_seq, 1, hw2)
        hl_spec = blk((None, None, 1, hw2), lambda gb, b: (gb, b, 0, 0))
    aliases = {}
    if s_all is not None:
        in_specs.append(pl.BlockSpec(memory_space=pl.ANY))
        args.append(s_all)
        aliases = {len(args) - 1: 0}
    body = functools.partial(_s5_body, sub=sub, nsub=nsub, n_scan=n_scan, has_h0=has_h0,
                             has_alias=s_all is not None)
    return pl.pallas_call(
        body,
        out_shape=(jax.ShapeDtypeStruct((xz.shape[0], D_MODEL), F32), jax.ShapeDtypeStruct(hl_shape, F32)),
        grid=(S5_NBLK, n_seq),
        in_specs=in_specs,
        out_specs=(blk((rows, 128), lambda gb, b: (row_block0 + b, gb)), hl_spec),
        scratch_shapes=[pltpu.VMEM((sw, hw2 + sw), BF16), pltpu.VMEM((hw2, sw), BF16)],
        input_output_aliases=aliases,
        compiler_params=_cparams(2),
    )(*args)


def _s5_glu_body(s_ref, st_ref, z_ref, w_ref, b_ref, o_ref, gb_ref, *, tm, row_chunk):
    @pl.when(pl.program_id(1) == 0)
    def _():
        def body(i, carry):
            r0 = pl.multiple_of(i * row_chunk, 16)
            gb_ref[pl.ds(r0, row_chunk), :] = jax.nn.gelu(s_ref[pl.ds(r0, row_chunk), :]).astype(BF16)
            return carry
        lax.fori_loop(0, tm // row_chunk, body, 0)
    g = jax.nn.gelu(st_ref[...])
    o = g * jax.nn.sigmoid(_dot(gb_ref[...], w_ref[...].astype(BF16)) + b_ref[...])
    o_ref[...] = (o * jax.nn.silu(z_ref[...])).astype(o_ref.dtype)


def _s5_glu(s_all, xz, glu_w, glu_b):
    m, k = s_all.shape
    tn = 512
    tm = _largest_divisor(m, 1088, 16)
    row_chunk = _largest_divisor(tm, 272, 16)
    zb0 = k // tn
    body = functools.partial(_s5_glu_body, tm=tm, row_chunk=row_chunk)
    return pl.pallas_call(
        body,
        out_shape=jax.ShapeDtypeStruct((m, k), BF16),
        grid=(m // tm, k // tn),
        in_specs=[pl.BlockSpec((tm, k), lambda i, j: (i, 0)),
                  pl.BlockSpec((tm, tn), lambda i, j: (i, j)),
                  pl.BlockSpec((tm, tn), lambda i, j: (i, zb0 + j)),
                  pl.BlockSpec((None, k, tn), lambda i, j: (0, 0, j)),
                  pl.BlockSpec((1, tn), lambda i, j: (0, j))],
        out_specs=pl.BlockSpec((tm, tn), lambda i, j: (i, j)),
        scratch_shapes=[pltpu.VMEM((tm, k), BF16)],
        compiler_params=_cparams(2),
    )(s_all, s_all, xz, glu_w, glu_b.reshape(1, k))


def _last_conv_inputs(a, bp, tp, col0, n_cols):
    return jnp.stack([a[(b + 1) * tp - (CONV_K - 1):(b + 1) * tp, col0:col0 + n_cols] for b in range(bp)])


def _lru_layer(u_norm, h, lay, j, dims, state_conv, state_h, w_in, conv_w, conv_b, wa, ba, wx, bx, lam, w_out):
    bp, tp, bs = dims
    rows_p = bp * tp
    xz = _matmul(h, w_in, j, norm_g=u_norm, g_idx=lay)
    params = _lru_params(conv_w[j:j + 1], conv_b[j], wa[j:j + 1], wx[j:j + 1], ba[j], bx[j], lam[j])
    y, hl_p = _lru_prompt(xz, params, bp, tp)
    prevs = _conv_prev_rows(state_conv[j])
    h0_rows = jnp.repeat(state_h[j], SAMPLE_T, axis=0)
    y, h_rows = _lru_sample(xz, prevs, h0_rows, params, y, rows_p)
    x_s = xz[rows_p:, :LRU_WIDTH].reshape(bs, SAMPLE_T, LRU_WIDTH)
    outs = (_last_conv_inputs(xz, bp, tp, 0, LRU_WIDTH), x_s[:, SAMPLE_T - (CONV_K - 1):],
            hl_p.reshape(bp, LRU_WIDTH), h_rows.reshape(bs, SAMPLE_T, LRU_WIDTH)[:, SAMPLE_T - 1])
    return _matmul(y, w_out, j, residual=h), outs


def _ret_layer(u_norm, h, lay, j, dims, state, w_in, gn, w_out):
    bp, tp, bs = dims
    rows_p = bp * tp
    qkvz = _matmul(h, w_in, j, norm_g=u_norm, g_idx=lay)
    y, s_p = _ret_prompt(qkvz, gn[j], bp, tp)
    y, s_s = _ret_sample(qkvz, gn[j], state[j:j + 1], y, rows_p, bs)
    return _matmul(y, w_out, j, residual=h, tn=256), (s_p, s_s[0])


def _ssd_layer(u_norm, h, lay, j, dims, state_conv, state, w_in, conv_w, conv_b, dt_bias, a_log, d_skip,
               norm_g, w_out):
    bp, tp, bs = dims
    rows_p = bp * tp
    m = h.shape[0]
    n_main = SSD_INNER + SSD_CONV_DIM
    w_t = jnp.swapaxes(w_in, 1, 2)
    zx = _matmul(h, w_t, j, norm_g=u_norm, g_idx=lay, n_cols=n_main, w_transposed=True)
    dt = _matmul(h, w_t[j:j + 1, n_main:, :], 0, norm_g=u_norm, g_idx=lay, w_transposed=True)
    dtg = dt.reshape(m, SSD_GROUPS, SSD_HPG).transpose(1, 0, 2)
    params = _ssd_param_arrays(conv_w[j:j + 1], conv_b[j], dt_bias[j], a_log[j], d_skip[j], norm_g[j])
    y, hl_p = _ssd_prompt(zx, dtg, params, bp, tp)
    st = state[j:j + 1].reshape(1, bs, SSD_GROUPS, SSD_GW, SSD_STATE)
    y, hl_s, *new_conv = _ssd_sample(zx, dtg, state_conv[j:j + 1], params, st, y, rows_p, bs)
    shape = (SSD_HEADS, SSD_HEADDIM, SSD_STATE)
    outs = (_last_conv_inputs(zx, bp, tp, SSD_INNER, SSD_CONV_DIM), jnp.concatenate(new_conv, axis=2),
            hl_p.reshape((bp,) + shape), hl_s.reshape((bs,) + shape))
    return _matmul(y, w_out, j, residual=h, tn=256), outs


S5_SUB = 8


def _s5_layer(u_norm, h, lay, j, dims, h0_re, h0_im, w_in, lam_re, lam_im, b_re, b_im, c_re, c_im, d_skip,
              log_dt, glu_w, glu_b, w_out):
    bp, tp, bs = dims
    rows_p = bp * tp
    m = h.shape[0]
    xz = _matmul(h, w_in, j, norm_g=u_norm, g_idx=lay)
    nsub_p = tp // S5_SUB
    n_scan = max(nsub_p - 1, 0).bit_length()
    par = (lam_re[j], lam_im[j], b_re[j], b_im[j], c_re[j], c_im[j], log_dt[j])
    s_all, hl_p = _s5_core(xz, _s5_tables(*par, S5_SUB, n_scan), d_skip[j], None, None,
                           sub=S5_SUB, nsub=nsub_p, n_seq=bp, row_block0=0, n_scan=n_scan)

    def to_lanes(v):
        return v.reshape(bs, S5_NBLK, S5_HW).transpose(1, 0, 2)
    h0 = jnp.concatenate([to_lanes(h0_re[j]), to_lanes(h0_im[j])], axis=2)
    s_all, hl_s = _s5_core(xz, _s5_tables(*par, SAMPLE_T, 0), d_skip[j], s_all, h0,
                           sub=SAMPLE_T, nsub=bs, n_seq=1, row_block0=rows_p // (bs * SAMPLE_T), n_scan=0)
    y = _s5_glu(s_all, xz, glu_w[j:j + 1], glu_b[j])

    def from_lanes(v, nb):
        return v.transpose(1, 0, 2).reshape(nb, S5_GROUPS, S5_STATE)
    hl_p = hl_p.reshape(S5_NBLK, bp, 2 * S5_HW)
    outs = (from_lanes(hl_p[..., :S5_HW], bp), from_lanes(hl_s[..., :S5_HW], bs),
            from_lanes(hl_p[..., S5_HW:], bp), from_lanes(hl_s[..., S5_HW:], bs))
    return _matmul(y, w_out, j, residual=h), outs


def kernel(x_prompt, x_sample, state_lru_conv, state_lru_h, state_ret, state_ssd_conv, state_ssd, state_s5_re, state_s5_im, cache_mem_k, cache_mem_v, mem_prompt, mix_norm, xa_norm, xa_mem_norm, xa_wq, xa_wkv, xa_wo, final_norm, lru_w_in, lru_conv_w, lru_conv_b, lru_wa, lru_ba, lru_wx, lru_bx, lru_lambda, lru_w_out, ret_w_in, ret_gn, ret_w_out, ssd_w_in, ssd_conv_w, ssd_conv_b, ssd_dt_bias, ssd_a_log, ssd_d, ssd_norm, ssd_w_out, s5_w_in, s5_lambda_re, s5_lambda_im, s5_b_re, s5_b_im, s5_c_re, s5_c_im, s5_d, s5_log_dt, s5_glu_w, s5_glu_b, s5_w_out):
    bp, tp, d = x_prompt.shape
    bs, ts, _ = x_sample.shape
    depth = mix_norm.shape[0]
    assert d == D_MODEL and ts == SAMPLE_T and tp % LRU_TC == 0 and bs % SAMPLE_NB == 0
    rows_p, rows_s = bp * tp, bs * ts
    assert rows_p % rows_s == 0
    dims = (bp, tp, bs)
    h = jnp.concatenate([x_prompt.reshape(rows_p, d), x_sample.reshape(rows_s, d)], axis=0)
    mem = mem_prompt.reshape(bp * MEM_LEN, d)
    mix_g = mix_norm.reshape(depth, 1, d)
    xa_g = xa_norm.reshape(depth, 1, d)
    mem_g = xa_mem_norm.reshape(depth, 1, d)

    outs = {k: [] for k in ("lru", "ret", "ssd", "s5")}
    mem_k, mem_v = [], []
    for i in range(depth):
        kind, j = i % 4, i // 4
        if kind == 0:
            h, o = _lru_layer(mix_g, h, i, j, dims, state_lru_conv, state_lru_h, lru_w_in,
                              lru_conv_w, lru_conv_b, lru_wa, lru_ba, lru_wx, lru_bx, lru_lambda, lru_w_out)
            outs["lru"].append(o)
        elif kind == 1:
            h, o = _ret_layer(mix_g, h, i, j, dims, state_ret, ret_w_in, ret_gn, ret_w_out)
            outs["ret"].append(o)
        elif kind == 2:
            h, o = _ssd_layer(mix_g, h, i, j, dims, state_ssd_conv, state_ssd, ssd_w_in,
                              ssd_conv_w, ssd_conv_b, ssd_dt_bias, ssd_a_log, ssd_d, ssd_norm, ssd_w_out)
            outs["ssd"].append(o)
        else:
            h, o = _s5_layer(mix_g, h, i, j, dims, state_s5_re, state_s5_im, s5_w_in,
                             s5_lambda_re, s5_lambda_im, s5_b_re, s5_b_im, s5_c_re, s5_c_im, s5_d, s5_log_dt,
                             s5_glu_w, s5_glu_b, s5_w_out)
            outs["s5"].append(o)
        kv = _matmul(mem, xa_wkv, i, norm_g=mem_g, g_idx=i)
        q = _matmul(h, xa_wq, i, norm_g=xa_g, g_idx=i, out_dtype=BF16)
        o_att = _xattn_prompt(q, kv, bp, tp)
        o_att = _xattn_sample(q, cache_mem_k, cache_mem_v, i, o_att, rows_p, bs)
        h = _matmul(o_att, xa_wo, i, residual=h)
        mem_k.append(kv[:, :d].reshape(bp, MEM_LEN, XA_HEADS, XA_HD))
        mem_v.append(kv[:, d:].reshape(bp, MEM_LEN, XA_HEADS, XA_HD))
    y_p = _rmsnorm(h, final_norm, 0, rows_p)
    y_s = _rmsnorm(h, final_norm, rows_p, rows_s)

    def stack(kind, idx):
        return jnp.stack([o[idx] for o in outs[kind]])
    return (y_p.reshape(bp, tp, d), y_s.reshape(bs, ts, d),
            stack("lru", 0), stack("lru", 1), stack("lru", 2), stack("lru", 3),
            stack("ret", 0), stack("ret", 1),
            stack("ssd", 0), stack("ssd", 1), stack("ssd", 2), stack("ssd", 3),
            stack("s5", 0), stack("s5", 1), stack("s5", 2), stack("s5", 3),
            jnp.stack(mem_k), jnp.stack(mem_v))
```

```python
import functools
import math

import jax
import jax.numpy as jnp
from jax import lax
from jax.experimental import pallas as pl
from jax.experimental.pallas import tpu as pltpu

F32 = jnp.float32
BF16 = jnp.bfloat16

D_MODEL = 2048
PAST_LEN = 16384
EPS = 1e-6
CONV_K = 4
CHUNK = 128
LRU_WIDTH = D_MODEL
LRU_BLOCK = 256
LRU_C = 8.0
RET_HEADS = 8
RET_DK = 256
RET_DV = 512
RET_VW = RET_HEADS * RET_DV
ROPE_BASE = 10000.0
SSD_INNER = 2 * D_MODEL
SSD_HEADDIM = 64
SSD_HEADS = 64
SSD_GROUPS = 8
SSD_HPG = 8
SSD_STATE = 128
SSD_GW = SSD_HPG * SSD_HEADDIM
SSD_CONV_DIM = SSD_INNER + 2 * SSD_GROUPS * SSD_STATE
S5_GROUP = 16
S5_GROUPS = 128
S5_STATE = 64
S5_G8 = 8
S5_NBLK = S5_GROUPS // S5_G8
S5_HW = S5_G8 * S5_STATE
MEM_LEN = 256
XA_HEADS = 4
XA_HD = 512
SAMPLE_T = 4
SAMPLE_NB = 4

VMEM_LIMIT_BYTES = 56 * 1024 * 1024
MM_MAX_ROWS = 2176
MM_X_DOUBLE_BUFFER_BYTES = 18 * 1024 * 1024
NT_DIMS = (((1,), (1,)), ((), ()))
TN_DIMS = (((0,), (0,)), ((), ()))


def _cparams(n_axes):
    return pltpu.CompilerParams(dimension_semantics=("arbitrary",) * n_axes,
                                vmem_limit_bytes=VMEM_LIMIT_BYTES)


def _dot(a, b):
    return jnp.dot(a, b, preferred_element_type=F32)


def _dot_nt(a, b):
    return lax.dot_general(a, b, NT_DIMS, preferred_element_type=F32)


def _dot_tn(a, b):
    return lax.dot_general(a, b, TN_DIMS, preferred_element_type=F32)


def _largest_divisor(n, cap, mult):
    for d in range(min(cap, n) // mult * mult, 0, -mult):
        if n % d == 0:
            return d
    raise ValueError(f"no divisor of {n} that is a multiple of {mult}")


def _expm1(x):
    return jnp.where(jnp.abs(x) < 0.5, jnp.tanh(0.5 * x) * (jnp.exp(x) + 1.0), jnp.exp(x) - 1.0)


def _expand_cols(v, width):
    rows, n = v.shape
    return jnp.concatenate([jnp.broadcast_to(v[:, r:r + 1], (rows, width)) for r in range(n)], axis=1)


def _mm_body(*refs, has_norm, has_res, stage_x, tm, row_chunk, w_transposed):
    it = iter(refs)
    x_ref, w_ref = next(it), next(it)
    g_ref = next(it) if has_norm else None
    r_ref = next(it) if has_res else None
    o_ref = next(it)
    if stage_x:
        xb_ref = next(it)

        @pl.when(pl.program_id(1) == 0)
        def _():
            def body(i, carry):
                r0 = pl.multiple_of(i * row_chunk, 16)
                x = x_ref[pl.ds(r0, row_chunk), :].astype(F32)
                if has_norm:
                    x = x * lax.rsqrt(jnp.mean(x * x, axis=-1, keepdims=True) + EPS) * g_ref[...]
                xb_ref[pl.ds(r0, row_chunk), :] = x.astype(BF16)
                return carry
            lax.fori_loop(0, tm // row_chunk, body, 0)
        xb = xb_ref[...]
    else:
        xb = x_ref[...]
    wb = w_ref[...].astype(BF16)
    acc = _dot_nt(xb, wb) if w_transposed else _dot(xb, wb)
    if has_res:
        acc = acc + r_ref[...]
    o_ref[...] = acc.astype(o_ref.dtype)


def _matmul(x, w, w_idx, *, norm_g=None, g_idx=0, residual=None, out_dtype=F32, tn=512, col_off=0, n_cols=None,
            w_transposed=False):
    m, k = x.shape
    n_total = w.shape[1] if w_transposed else w.shape[2]
    n_cols = n_total if n_cols is None else n_cols
    tn = min(tn, n_cols)
    assert n_cols % tn == 0 and col_off % tn == 0
    tm = _largest_divisor(m, MM_MAX_ROWS, 16)
    stage_x = x.dtype != BF16 or norm_g is not None
    row_chunk = _largest_divisor(tm, 272, 16)
    cb = col_off // tn
    x_bytes = tm * k * x.dtype.itemsize
    x_mode = {} if 2 * x_bytes <= MM_X_DOUBLE_BUFFER_BYTES else {"pipeline_mode": pl.Buffered(1)}
    w_spec = (pl.BlockSpec((None, tn, k), lambda i, j: (w_idx, j + cb, 0)) if w_transposed
              else pl.BlockSpec((None, k, tn), lambda i, j: (w_idx, 0, j + cb)))
    in_specs = [pl.BlockSpec((tm, k), lambda i, j: (i, 0), **x_mode), w_spec]
    args = [x, w]
    if norm_g is not None:
        in_specs.append(pl.BlockSpec((None, 1, k), lambda i, j: (g_idx, 0, 0)))
        args.append(norm_g)
    if residual is not None:
        in_specs.append(pl.BlockSpec((tm, tn), lambda i, j: (i, j)))
        args.append(residual)
    body = functools.partial(_mm_body, has_norm=norm_g is not None, has_res=residual is not None,
                             stage_x=stage_x, tm=tm, row_chunk=row_chunk, w_transposed=w_transposed)
    return pl.pallas_call(
        body,
        out_shape=jax.ShapeDtypeStruct((m, n_cols), out_dtype),
        grid=(m // tm, n_cols // tn),
        in_specs=in_specs,
        out_specs=pl.BlockSpec((tm, tn), lambda i, j: (i, j)),
        scratch_shapes=[pltpu.VMEM((tm, k), BF16)] if stage_x else [],
        compiler_params=_cparams(2),
    )(*args)


def _mm_res_norm_body(x_ref, w_ref, r_ref, g_ref, h_ref, xn_ref, *, tm, tn, n_tiles, row_chunk):
    j = pl.program_id(1)
    acc = _dot(x_ref[...], w_ref[...].astype(BF16)) + r_ref[...]
    for jj in range(n_tiles):
        @pl.when(j == jj)
        def _():
            h_ref[:, jj * tn:(jj + 1) * tn] = acc

    @pl.when(j == n_tiles - 1)
    def _():
        def body(i, carry):
            r0 = pl.multiple_of(i * row_chunk, 16)
            h = h_ref[pl.ds(r0, row_chunk), :]
            hn = h * lax.rsqrt(jnp.mean(h * h, axis=-1, keepdims=True) + EPS) * g_ref[...]
            xn_ref[pl.ds(r0, row_chunk), :] = hn.astype(BF16)
            return carry
        lax.fori_loop(0, tm // row_chunk, body, 0)


MM_NORM_OUT_ROWS = 1088


def _matmul_res_norm(x, w, w_idx, residual, next_g, g_idx, *, tn=512):
    m, k = x.shape
    n = w.shape[2]
    tm = _largest_divisor(m, MM_NORM_OUT_ROWS, 16)
    row_chunk = _largest_divisor(tm, 272, 16)
    n_tiles = n // tn
    body = functools.partial(_mm_res_norm_body, tm=tm, tn=tn, n_tiles=n_tiles, row_chunk=row_chunk)
    return pl.pallas_call(
        body,
        out_shape=(jax.ShapeDtypeStruct((m, n), F32), jax.ShapeDtypeStruct((m, n), BF16)),
        grid=(m // tm, n_tiles),
        in_specs=[pl.BlockSpec((tm, k), lambda i, j: (i, 0)),
                  pl.BlockSpec((None, k, tn), lambda i, j: (w_idx, 0, j)),
                  pl.BlockSpec((tm, tn), lambda i, j: (i, j)),
                  pl.BlockSpec((None, 1, n), lambda i, j: (g_idx, 0, 0))],
        out_specs=(pl.BlockSpec((tm, n), lambda i, j: (i, 0)), pl.BlockSpec((tm, n), lambda i, j: (i, 0))),
        compiler_params=_cparams(2),
    )(x, w, residual, next_g)


def _rmsnorm_body(x_ref, g_ref, o_ref):
    x = x_ref[...]
    o_ref[...] = x * lax.rsqrt(jnp.mean(x * x, axis=-1, keepdims=True) + EPS) * g_ref[...]


def _rmsnorm(x, g, row0, n_rows):
    k = x.shape[1]
    tm = _largest_divisor(math.gcd(n_rows, row0) if row0 else n_rows, 512, 8)
    rb0 = row0 // tm
    return pl.pallas_call(
        _rmsnorm_body,
        out_shape=jax.ShapeDtypeStruct((n_rows, k), F32),
        grid=(n_rows // tm,),
        in_specs=[pl.BlockSpec((tm, k), lambda i: (rb0 + i, 0)), pl.BlockSpec((1, k), lambda i: (0, 0))],
        out_specs=pl.BlockSpec((tm, k), lambda i: (i, 0)),
        compiler_params=_cparams(1),
    )(x, g.reshape(1, k))


def _softmax_rows(s):
    e = jnp.exp(s - jnp.max(s, axis=-1, keepdims=True))
    return e / jnp.sum(e, axis=-1, keepdims=True)


def _xattn_prompt_body(q_ref, k_ref, v_ref, o_ref):
    kb = k_ref[...].astype(BF16)
    vb = v_ref[...].astype(BF16)
    for h in range(XA_HEADS):
        sl = slice(h * XA_HD, (h + 1) * XA_HD)
        p = _softmax_rows(_dot_nt(q_ref[:, sl], kb[:, sl]) * XA_HD ** -0.5)
        o_ref[:, sl] = _dot(p.astype(BF16), vb[:, sl]).astype(o_ref.dtype)


def _xattn_prompt(q, kv, bp, tp):
    m = q.shape[0]
    tq = min(tp, 1024)
    nq = tp // tq
    return pl.pallas_call(
        _xattn_prompt_body,
        out_shape=jax.ShapeDtypeStruct((m, D_MODEL), BF16),
        grid=(bp, nq),
        in_specs=[pl.BlockSpec((tq, D_MODEL), lambda b, i: (b * nq + i, 0)),
                  pl.BlockSpec((MEM_LEN, D_MODEL), lambda b, i: (b, 0)),
                  pl.BlockSpec((MEM_LEN, D_MODEL), lambda b, i: (b, 1))],
        out_specs=pl.BlockSpec((tq, D_MODEL), lambda b, i: (b * nq + i, 0)),
        compiler_params=_cparams(2),
    )(q, kv, kv)


def _log2(n):
    assert n & (n - 1) == 0, n
    return n.bit_length() - 1


def _row_segment(rows, seg_len, width):
    return lax.broadcasted_iota(jnp.int32, (rows, width), 0) >> _log2(seg_len)


def _row_pos(rows, seg_len, width):
    return lax.broadcasted_iota(jnp.int32, (rows, width), 0) & (seg_len - 1)


XA_SNB = 2


def _xattn_sample_body(q_ref, k_ref, v_ref, oin_ref, o_ref, acc_ref):
    del oin_ref
    part = pl.program_id(1)
    rows = SAMPLE_NB * SAMPLE_T

    @pl.when(part == 0)
    def _():
        acc_ref[...] = jnp.zeros_like(acc_ref)

    lane_blocks = XA_HD // 128

    def head(ref, i, h):
        return jnp.concatenate([ref[i, pl.ds(j * XA_HEADS + h, MEM_LEN, stride=lane_blocks * XA_HEADS), :]
                                for j in range(lane_blocks)], axis=1).astype(BF16)

    pairs = [(i, h) for i in range(XA_SNB) for h in range(XA_HEADS)]
    s = jnp.concatenate([_dot_nt(q_ref[:, h * XA_HD:(h + 1) * XA_HD], head(k_ref, i, h)) for i, h in pairs], axis=0)
    p = _softmax_rows(s * XA_HD ** -0.5).astype(BF16)
    seg = _row_segment(rows, SAMPLE_T, XA_HD)
    for h in range(XA_HEADS):
        sl = slice(h * XA_HD, (h + 1) * XA_HD)
        o = acc_ref[:, sl]
        for i in range(XA_SNB):
            n = i * XA_HEADS + h
            o = jnp.where(seg == part * XA_SNB + i, _dot(p[n * rows:(n + 1) * rows], head(v_ref, i, h)), o)
        acc_ref[:, sl] = o

    @pl.when(part == pl.num_programs(1) - 1)
    def _():
        o_ref[...] = acc_ref[...].astype(o_ref.dtype)


def _xattn_sample(q, cache_k, cache_v, layer, o_all, rows_p, bs):
    rows = SAMPLE_NB * SAMPLE_T
    rb0 = rows_p // rows
    nparts = SAMPLE_NB // XA_SNB
    depth = cache_k.shape[0]
    lane_blocks = XA_HD // 128
    kv_rows = MEM_LEN * lane_blocks * XA_HEADS

    def relayout(c):
        c = c.reshape(depth, bs, MEM_LEN, XA_HEADS, lane_blocks, 128).transpose(0, 1, 2, 4, 3, 5)
        return c.reshape(depth, bs, kv_rows, 128)
    cache_k, cache_v = relayout(cache_k), relayout(cache_v)
    kv_spec = pl.BlockSpec((None, XA_SNB, kv_rows, 128), lambda bb, s: (layer, bb * nparts + s, 0, 0))
    return pl.pallas_call(
        _xattn_sample_body,
        out_shape=jax.ShapeDtypeStruct(o_all.shape, o_all.dtype),
        grid=(bs // SAMPLE_NB, nparts),
        in_specs=[pl.BlockSpec((rows, D_MODEL), lambda bb, s: (rb0 + bb, 0)), kv_spec, kv_spec,
                  pl.BlockSpec(memory_space=pl.ANY)],
        out_specs=pl.BlockSpec((rows, D_MODEL), lambda bb, s: (rb0 + bb, 0)),
        scratch_shapes=[pltpu.VMEM((rows, D_MODEL), F32)],
        input_output_aliases={3: 0},
        compiler_params=_cparams(2),
    )(q, cache_k, cache_v, o_all)


LRU_CW = 512
LRU_TC = 256


def _lru_gate_scan(xc, wa_ref, wx_ref, ba, bx, lam, h0, seg_len):
    rows = xc.shape[0]
    rs, gis = [], []
    for n in range(LRU_CW // LRU_BLOCK):
        xb = xc[:, n * LRU_BLOCK:(n + 1) * LRU_BLOCK].astype(BF16)
        rs.append(_dot(xb, wa_ref[n].astype(BF16)))
        gis.append(_dot(xb, wx_ref[n].astype(BF16)))
    r = jax.nn.sigmoid(jnp.concatenate(rs, axis=1) + ba)
    gi = jax.nn.sigmoid(jnp.concatenate(gis, axis=1) + bx)
    log_a = -LRU_C * r * jax.nn.softplus(-lam)
    a = jnp.exp(log_a)
    b = jnp.sqrt(-_expm1(2.0 * log_a)) * (gi * xc)
    t = _row_pos(rows, seg_len, LRU_CW)
    shift = 1
    while shift < seg_len:
        keep = t >= shift
        a_prev = pltpu.roll(a, shift, 0)
        b_prev = pltpu.roll(b, shift, 0)
        b = jnp.where(keep, a * b_prev + b, b)
        a = jnp.where(keep, a * a_prev, a)
        shift *= 2
    return b + a * h0


def _lru_prompt_body(x_ref, z_ref, cw_ref, cb_ref, wa_ref, wx_ref, ba_ref, bx_ref, lam_ref,
                     y_ref, hl_ref, xext_ref, hc_ref):
    c = pl.program_id(2)

    @pl.when(c == 0)
    def _():
        xext_ref[0:8, :] = jnp.zeros((8, LRU_CW), F32)
        hc_ref[...] = jnp.zeros_like(hc_ref)

    x = x_ref[...]
    xext_ref[8:8 + LRU_TC, :] = x
    xc = cw_ref[3:4, :] * x + cb_ref[...]
    for k in range(CONV_K - 1):
        xc = xc + cw_ref[k:k + 1, :] * xext_ref[5 + k:5 + k + LRU_TC, :]
    xext_ref[0:8, :] = x[LRU_TC - 8:LRU_TC, :]
    h = _lru_gate_scan(xc, wa_ref, wx_ref, ba_ref[...], bx_ref[...], lam_ref[...], hc_ref[0:1, :], LRU_TC)
    hc_ref[0:1, :] = h[LRU_TC - 1:LRU_TC, :]
    y_ref[...] = (h * jax.nn.silu(z_ref[...])).astype(y_ref.dtype)

    @pl.when(c == pl.num_programs(2) - 1)
    def _():
        hl_ref[...] = h[LRU_TC - 1:LRU_TC, :]


def _lru_param_specs(n_grid):
    def cmap(block):
        if n_grid == 3:
            return lambda b, cb, c: block(cb)
        return lambda cb: block(cb)
    return [pl.BlockSpec((None, CONV_K, LRU_CW), cmap(lambda cb: (0, 0, cb))),
            pl.BlockSpec((None, 1, LRU_CW), cmap(lambda cb: (0, 0, cb))),
            pl.BlockSpec((None, LRU_CW // LRU_BLOCK, LRU_BLOCK, LRU_BLOCK), cmap(lambda cb: (0, cb, 0, 0))),
            pl.BlockSpec((None, LRU_CW // LRU_BLOCK, LRU_BLOCK, LRU_BLOCK), cmap(lambda cb: (0, cb, 0, 0))),
            pl.BlockSpec((None, 1, LRU_CW), cmap(lambda cb: (0, 0, cb))),
            pl.BlockSpec((None, 1, LRU_CW), cmap(lambda cb: (0, 0, cb))),
            pl.BlockSpec((None, 1, LRU_CW), cmap(lambda cb: (0, 0, cb)))]


def _lru_params(conv_w, conv_b, wa, wx, ba, bx, lam):
    w = LRU_WIDTH
    return [conv_w, conv_b.reshape(1, 1, w), wa, wx, ba.reshape(1, 1, w), bx.reshape(1, 1, w), lam.reshape(1, 1, w)]


def _lru_prompt(xz, params, bp, tp):
    m = xz.shape[0]
    nc = tp // LRU_TC
    ncb = LRU_WIDTH // LRU_CW
    return pl.pallas_call(
        _lru_prompt_body,
        out_shape=(jax.ShapeDtypeStruct((m, LRU_WIDTH), BF16), jax.ShapeDtypeStruct((bp, 1, LRU_WIDTH), F32)),
        grid=(bp, ncb, nc),
        in_specs=[pl.BlockSpec((LRU_TC, LRU_CW), lambda b, cb, c: (b * nc + c, cb)),
                  pl.BlockSpec((LRU_TC, LRU_CW), lambda b, cb, c: (b * nc + c, ncb + cb))] + _lru_param_specs(3),
        out_specs=(pl.BlockSpec((LRU_TC, LRU_CW), lambda b, cb, c: (b * nc + c, cb)),
                   pl.BlockSpec((None, 1, LRU_CW), lambda b, cb, c: (b, 0, cb))),
        scratch_shapes=[pltpu.VMEM((LRU_TC + 8, LRU_CW), F32), pltpu.VMEM((8, LRU_CW), F32)],
        compiler_params=_cparams(3),
    )(xz, xz, *params)


def _lru_sample_body(x_ref, z_ref, p1_ref, p2_ref, p3_ref, h0_ref, cw_ref, cb_ref, wa_ref, wx_ref,
                     ba_ref, bx_ref, lam_ref, yin_ref, y_ref, h_ref):
    del yin_ref
    x = x_ref[...]
    rows = x.shape[0]
    t = _row_pos(rows, SAMPLE_T, LRU_CW)
    xc = cw_ref[3:4, :] * x + cb_ref[...]
    for k, prev_ref in ((1, p1_ref), (2, p2_ref), (3, p3_ref)):
        xc = xc + cw_ref[3 - k:4 - k, :] * jnp.where(t >= k, pltpu.roll(x, k, 0), prev_ref[...])
    h = _lru_gate_scan(xc, wa_ref, wx_ref, ba_ref[...], bx_ref[...], lam_ref[...], h0_ref[...], SAMPLE_T)
    h_ref[...] = h
    y_ref[...] = (h * jax.nn.silu(z_ref[...])).astype(y_ref.dtype)


def _lru_sample(xz, prevs, h0_rows, params, y_all, rows_p):
    rows_s = prevs[0].shape[0]
    rb0 = rows_p // rows_s
    ncb = LRU_WIDTH // LRU_CW
    small = pl.BlockSpec((rows_s, LRU_CW), lambda cb: (0, cb))
    return pl.pallas_call(
        _lru_sample_body,
        out_shape=(jax.ShapeDtypeStruct(y_all.shape, y_all.dtype), jax.ShapeDtypeStruct((rows_s, LRU_WIDTH), F32)),
        grid=(ncb,),
        in_specs=[pl.BlockSpec((rows_s, LRU_CW), lambda cb: (rb0, cb)),
                  pl.BlockSpec((rows_s, LRU_CW), lambda cb: (rb0, ncb + cb)),
                  small, small, small, small] + _lru_param_specs(1) + [pl.BlockSpec(memory_space=pl.ANY)],
        out_specs=(pl.BlockSpec((rows_s, LRU_CW), lambda cb: (rb0, cb)), small),
        input_output_aliases={13: 0},
        compiler_params=_cparams(1),
    )(xz, xz, *prevs, h0_rows, *params, y_all)


def _conv_prev_rows(buf):
    b, _, c = buf.shape
    out = []
    for k in range(1, CONV_K):
        pad = jnp.zeros((b, SAMPLE_T - k, c), buf.dtype)
        out.append(jnp.concatenate([buf[:, CONV_K - 1 - k:], pad], axis=1).reshape(b * SAMPLE_T, c))
    return out


def _rope(x, cos, sin):
    half = RET_DK // 2
    x1, x2 = x[:, :half], x[:, half:]
    return jnp.concatenate([x1 * cos - x2 * sin, x1 * sin + x2 * cos], axis=1)


def _ret_chunk(q, k, v, z, cos, sin, dmask, qdec, kdec, cdec, gn, states, seg_len):
    rows = q.shape[0]
    qb = _rope(q, cos, sin).astype(BF16)
    kr = _rope(k, cos, sin) * RET_DK ** -0.5
    kb = kr.astype(BF16)
    vb = v.astype(BF16)
    kd = kr * kdec
    o = _dot((_dot_nt(qb, kb) * dmask).astype(BF16), vb)
    new_states = []
    single = len(states) == 1
    seg_v = None if single else _row_segment(rows, seg_len, RET_DV)
    seg_k = None if single else _row_segment(rows, seg_len, RET_DK)
    for i, s in enumerate(states):
        cross = _dot(qb, s.astype(BF16)) * qdec
        kdi = kd
        if not single:
            cross = jnp.where(seg_v == i, cross, 0.0)
            kdi = jnp.where(seg_k == i, kd, 0.0)
        o = o + cross
        new_states.append(s * cdec + _dot_tn(kdi.astype(BF16), vb))
    mu = jnp.mean(o, axis=-1, keepdims=True)
    var = jnp.mean(jnp.square(o - mu), axis=-1, keepdims=True)
    on = (o - mu) * lax.rsqrt(var + EPS) * gn
    return on * jax.nn.silu(z), new_states


def _ret_prompt_body(q_ref, k_ref, v_ref, z_ref, cos_ref, sin_ref, dm_ref, qd_ref, kd_ref, cd_ref, gn_ref,
                     y_ref, sl_ref, s_ref):
    c = pl.program_id(1)

    @pl.when(c == 0)
    def _():
        s_ref[...] = jnp.zeros_like(s_ref)

    cos, sin = cos_ref[...], sin_ref[...]
    for h in range(RET_HEADS):
        ksl = slice(h * RET_DK, (h + 1) * RET_DK)
        vsl = slice(h * RET_DV, (h + 1) * RET_DV)
        y, (s_new,) = _ret_chunk(q_ref[:, ksl], k_ref[:, ksl], v_ref[:, vsl], z_ref[:, vsl], cos, sin,
                                 dm_ref[h], qd_ref[h], kd_ref[h], cd_ref[h], gn_ref[:, vsl], [s_ref[h]], CHUNK)
        s_ref[h] = s_new
        y_ref[:, vsl] = y.astype(y_ref.dtype)

        @pl.when(c == pl.num_programs(1) - 1)
        def _():
            sl_ref[h] = s_new


def _ret_tables(seg_len, nseg, pos):
    rows = seg_len * nseg
    log_g = jnp.log1p(-jnp.exp2(-5.0 - jnp.arange(RET_HEADS, dtype=F32)))[:, None, None]
    t = (jnp.arange(rows) % seg_len).astype(F32)
    seg = jnp.arange(rows) // seg_len
    rel = t[:, None] - t[None, :]
    ok = (rel >= 0) & (seg[:, None] == seg[None, :])
    dmask = jnp.where(ok, jnp.exp(log_g * jnp.where(ok, rel, 0.0)), 0.0)
    qdec = jnp.broadcast_to(jnp.exp(log_g * (t + 1.0)[None, :, None]), (RET_HEADS, rows, RET_DV))
    kdec = jnp.broadcast_to(jnp.exp(log_g * (seg_len - 1.0 - t)[None, :, None]), (RET_HEADS, rows, RET_DK))
    cdec = jnp.broadcast_to(jnp.exp(log_g * seg_len), (RET_HEADS, 1, RET_DV))
    half = RET_DK // 2
    inv = ROPE_BASE ** (-jnp.arange(half, dtype=F32) / half)
    ang = pos.astype(F32)[:, None] * inv
    return jnp.cos(ang), jnp.sin(ang), dmask, qdec, kdec, cdec


def _ret_prompt(qkvz, gn, bp, tp):
    m = qkvz.shape[0]
    nc = tp // CHUNK
    cos, sin, dmask, qdec, kdec, cdec = _ret_tables(CHUNK, 1, jnp.arange(tp))
    qk_w = RET_HEADS * RET_DK
    full = lambda b, c: (0, 0, 0)
    return pl.pallas_call(
        _ret_prompt_body,
        out_shape=(jax.ShapeDtypeStruct((m, RET_VW), BF16),
                   jax.ShapeDtypeStruct((bp, RET_HEADS, RET_DK, RET_DV), F32)),
        grid=(bp, nc),
        in_specs=[pl.BlockSpec((CHUNK, qk_w), lambda b, c: (b * nc + c, 0)),
                  pl.BlockSpec((CHUNK, qk_w), lambda b, c: (b * nc + c, 1)),
                  pl.BlockSpec((CHUNK, RET_VW), lambda b, c: (b * nc + c, 1)),
                  pl.BlockSpec((CHUNK, RET_VW), lambda b, c: (b * nc + c, 2)),
                  pl.BlockSpec((CHUNK, RET_DK // 2), lambda b, c: (c, 0)),
                  pl.BlockSpec((CHUNK, RET_DK // 2), lambda b, c: (c, 0)),
                  pl.BlockSpec((RET_HEADS, CHUNK, CHUNK), full),
                  pl.BlockSpec((RET_HEADS, CHUNK, RET_DV), full),
                  pl.BlockSpec((RET_HEADS, CHUNK, RET_DK), full),
                  pl.BlockSpec((RET_HEADS, 1, RET_DV), full),
                  pl.BlockSpec((1, RET_VW), lambda b, c: (0, 0))],
        out_specs=(pl.BlockSpec((CHUNK, RET_VW), lambda b, c: (b * nc + c, 0)),
                   pl.BlockSpec((None, RET_HEADS, RET_DK, RET_DV), lambda b, c: (b, 0, 0, 0))),
        scratch_shapes=[pltpu.VMEM((RET_HEADS, RET_DK, RET_DV), F32)],
        compiler_params=_cparams(2),
    )(qkvz, qkvz, qkvz, qkvz, cos, sin, dmask, qdec, kdec, cdec, gn.reshape(1, RET_VW))


def _ret_sample_body(q_ref, k_ref, v_ref, z_ref, cos_ref, sin_ref, dm_ref, qd_ref, kd_ref, cd_ref, gn_ref,
                     s0_ref, yin_ref, y_ref, s_ref):
    del yin_ref
    y, s_new = _ret_chunk(q_ref[...], k_ref[...], v_ref[...], z_ref[...], cos_ref[...], sin_ref[...],
                          dm_ref[...], qd_ref[...], kd_ref[...], cd_ref[...], gn_ref[...],
                          [s0_ref[i] for i in range(SAMPLE_NB)], SAMPLE_T)
    for i in range(SAMPLE_NB):
        s_ref[i] = s_new[i]
    y_ref[...] = y.astype(y_ref.dtype)


def _ret_sample(qkvz, gn, state, y_all, rows_p, bs):
    rows = SAMPLE_NB * SAMPLE_T
    rb0 = rows_p // rows
    pos = PAST_LEN + jnp.arange(rows) % SAMPLE_T
    cos, sin, dmask, qdec, kdec, cdec = _ret_tables(SAMPLE_T, SAMPLE_NB, pos)
    kb0 = RET_HEADS
    vb0 = 2 * RET_HEADS * RET_DK // RET_DV
    zb0 = vb0 + RET_HEADS
    st_spec = pl.BlockSpec((None, SAMPLE_NB, None, RET_DK, RET_DV), lambda bb, h: (0, bb, h, 0, 0))
    return pl.pallas_call(
        _ret_sample_body,
        out_shape=(jax.ShapeDtypeStruct(y_all.shape, y_all.dtype), jax.ShapeDtypeStruct(state.shape, F32)),
        grid=(bs // SAMPLE_NB, RET_HEADS),
        in_specs=[pl.BlockSpec((rows, RET_DK), lambda bb, h: (rb0 + bb, h)),
                  pl.BlockSpec((rows, RET_DK), lambda bb, h: (rb0 + bb, kb0 + h)),
                  pl.BlockSpec((rows, RET_DV), lambda bb, h: (rb0 + bb, vb0 + h)),
                  pl.BlockSpec((rows, RET_DV), lambda bb, h: (rb0 + bb, zb0 + h)),
                  pl.BlockSpec((rows, RET_DK // 2), lambda bb, h: (0, 0)),
                  pl.BlockSpec((rows, RET_DK // 2), lambda bb, h: (0, 0)),
                  pl.BlockSpec((None, rows, rows), lambda bb, h: (h, 0, 0)),
                  pl.BlockSpec((None, rows, RET_DV), lambda bb, h: (h, 0, 0)),
                  pl.BlockSpec((None, rows, RET_DK), lambda bb, h: (h, 0, 0)),
                  pl.BlockSpec((None, 1, RET_DV), lambda bb, h: (h, 0, 0)),
                  pl.BlockSpec((1, RET_DV), lambda bb, h: (0, h)),
                  st_spec,
                  pl.BlockSpec(memory_space=pl.ANY)],
        out_specs=(pl.BlockSpec((rows, RET_DV), lambda bb, h: (rb0 + bb, h)), st_spec),
        input_output_aliases={12: 0},
        compiler_params=_cparams(2),
    )(qkvz, qkvz, qkvz, qkvz, cos, sin, dmask, qdec, kdec, cdec, gn.reshape(1, RET_VW), state, y_all)


def _head_expand_matrix():
    r = lax.broadcasted_iota(jnp.int32, (128, SSD_GW), 0)
    c = lax.broadcasted_iota(jnp.int32, (128, SSD_GW), 1)
    return jnp.where(r == (c >> _log2(SSD_HEADDIM)), 1.0, 0.0).astype(BF16)


def _expand_heads(v, expand_mat):
    v1 = v.astype(BF16)
    r1 = v - v1.astype(F32)
    v2 = r1.astype(BF16)
    v3 = (r1 - v2.astype(F32)).astype(BF16)
    return (_dot(v1, expand_mat) + _dot(v2, expand_mat)) + _dot(v3, expand_mat)


def _ssd_chunk(xs, bs, cs, z, dt_raw, dt_bias, a_log, d_skip, norm_g, states, seg_len, expand_mat):
    rows = xs.shape[0]
    single = len(states) == 1
    hp = SSD_HPG
    pad = 128
    t8 = _row_pos(rows, seg_len, pad)
    seg8 = _row_segment(rows, seg_len, pad)
    tt = lax.broadcasted_iota(jnp.int32, (rows, rows), 0)
    ss = lax.broadcasted_iota(jnp.int32, (rows, rows), 1)
    causal = (tt >= ss) & ((tt >> _log2(seg_len)) == (ss >> _log2(seg_len)))

    def lane_pad(v):
        return jnp.concatenate([v, jnp.zeros((v.shape[0], pad - hp), F32)], axis=1)

    dt = jax.nn.softplus(lane_pad(dt_raw) + lane_pad(dt_bias))
    da = dt * (-jnp.exp(lane_pad(a_log)))
    cum = da
    shift = 1
    while shift < seg_len:
        cum = cum + jnp.where(t8 >= shift, pltpu.roll(cum, shift, 0), 0.0)
        shift *= 2
    cum_sq = cum if rows == pad else jnp.concatenate([cum, jnp.zeros((pad - rows, pad), F32)], axis=0)
    cum_t = cum_sq.T[0:hp, 0:rows]
    lasts = [cum[(i + 1) * seg_len - 1:(i + 1) * seg_len, :] for i in range(len(states))]
    last_row = lasts[0]
    if not single:
        last_row = jnp.zeros((rows, pad), F32)
        for i, l in enumerate(lasts):
            last_row = jnp.where(seg8 == i, l, last_row)
    to_end = jnp.exp(last_row - cum)
    expanded = _expand_heads(jnp.concatenate([dt, jnp.exp(cum), to_end], axis=0), expand_mat)
    dt_x, ecum, to_end_x = expanded[0:rows], expanded[rows:2 * rows], expanded[2 * rows:3 * rows]

    csb = cs.astype(BF16)
    bsb = bs.astype(BF16)
    cb = _dot_nt(csb, bsb)
    xdt = xs * dt_x
    ycols = []
    for r in range(hp):
        lmat = jnp.where(causal, jnp.exp(cum[:, r:r + 1] - cum_t[r:r + 1, :]), 0.0)
        ycols.append(_dot((cb * lmat).astype(BF16), xdt[:, r * SSD_HEADDIM:(r + 1) * SSD_HEADDIM].astype(BF16)))
    y = jnp.concatenate(ycols, axis=1)
    xte = xdt * to_end_x
    segw = None if single else _row_segment(rows, seg_len, SSD_GW)
    new_states = []
    for i, h in enumerate(states):
        y_off = _dot_nt(csb, h.astype(BF16)) * ecum
        xi = xte
        if not single:
            y_off = jnp.where(segw == i, y_off, 0.0)
            xi = jnp.where(segw == i, xte, 0.0)
        y = y + y_off
        e_last = jnp.exp(lasts[i])
        dec = jnp.concatenate([jnp.broadcast_to(e_last[:, r:r + 1], (SSD_HEADDIM, SSD_STATE)) for r in range(hp)],
                              axis=0)
        new_states.append(h * dec + _dot_tn(xi.astype(BF16), bsb))
    y = y + xs * _expand_cols(d_skip, SSD_HEADDIM)
    yg = y * jax.nn.silu(z)
    yg = yg * lax.rsqrt(jnp.mean(yg * yg, axis=-1, keepdims=True) + EPS)
    return yg * norm_g, new_states


def _ssd_prompt_body(z_ref, x_ref, b_ref, c_ref, dt_ref, wx_ref, wb_ref, wc_ref, bx_ref, bb_ref, bc_ref,
                     dtb_ref, al_ref, ds_ref, ng_ref, y_ref, hl_ref, xe_ref, be_ref, ce_ref, h_ref):
    c = pl.program_id(1)

    @pl.when(c == 0)
    def _():
        xe_ref[0:8, :] = jnp.zeros((8, SSD_INNER), F32)
        be_ref[0:8, :] = jnp.zeros((8, SSD_GROUPS * SSD_STATE), F32)
        ce_ref[0:8, :] = jnp.zeros((8, SSD_GROUPS * SSD_STATE), F32)
        h_ref[...] = jnp.zeros_like(h_ref)

    def conv(raw_ref, ext_ref, w_ref, bias_ref, sl):
        raw = raw_ref[:, sl]
        ext_ref[8:8 + CHUNK, sl] = raw
        acc = w_ref[3:4, sl] * raw + bias_ref[:, sl]
        for k in range(CONV_K - 1):
            acc = acc + w_ref[k:k + 1, sl] * ext_ref[5 + k:5 + k + CHUNK, sl]
        ext_ref[0:8, sl] = raw[CHUNK - 8:CHUNK, :]
        return jax.nn.silu(acc)

    expand_mat = _head_expand_matrix()
    for g in range(SSD_GROUPS):
        xsl = slice(g * SSD_GW, (g + 1) * SSD_GW)
        nsl = slice(g * SSD_STATE, (g + 1) * SSD_STATE)
        xs = conv(x_ref, xe_ref, wx_ref, bx_ref, xsl)
        bs = conv(b_ref, be_ref, wb_ref, bb_ref, nsl)
        cs = conv(c_ref, ce_ref, wc_ref, bc_ref, nsl)
        y, (h_new,) = _ssd_chunk(xs, bs, cs, z_ref[:, xsl], dt_ref[g], dtb_ref[g], al_ref[g], ds_ref[g],
                                 ng_ref[:, xsl], [h_ref[g]], CHUNK, expand_mat)
        h_ref[g] = h_new
        y_ref[:, xsl] = y.astype(y_ref.dtype)

        @pl.when(c == pl.num_programs(1) - 1)
        def _():
            hl_ref[g] = h_new


def _ssd_col_blocks():
    xb0 = SSD_INNER // SSD_GW
    bb0 = (2 * SSD_INNER) // SSD_STATE
    cb0 = bb0 + SSD_GROUPS
    return xb0, bb0, cb0


def _ssd_param_arrays(conv_w, conv_b, dt_bias, a_log, d_skip, norm_g):
    g, hp = SSD_GROUPS, SSD_HPG
    return [conv_w, conv_w, conv_w, conv_b.reshape(1, 1, -1), conv_b.reshape(1, 1, -1), conv_b.reshape(1, 1, -1),
            dt_bias.reshape(g, 1, hp), a_log.reshape(g, 1, hp), d_skip.reshape(g, 1, hp), norm_g.reshape(1, SSD_INNER)]


def _ssd_param_specs(gmap):
    wxb0 = 0
    wbb0 = SSD_INNER // SSD_STATE
    wcb0 = wbb0 + SSD_GROUPS
    return [pl.BlockSpec((None, CONV_K, SSD_GW), gmap(lambda g: (0, 0, wxb0 + g))),
            pl.BlockSpec((None, CONV_K, SSD_STATE), gmap(lambda g: (0, 0, wbb0 + g))),
            pl.BlockSpec((None, CONV_K, SSD_STATE), gmap(lambda g: (0, 0, wcb0 + g))),
            pl.BlockSpec((None, 1, SSD_GW), gmap(lambda g: (0, 0, wxb0 + g))),
            pl.BlockSpec((None, 1, SSD_STATE), gmap(lambda g: (0, 0, wbb0 + g))),
            pl.BlockSpec((None, 1, SSD_STATE), gmap(lambda g: (0, 0, wcb0 + g))),
            pl.BlockSpec((None, 1, SSD_HPG), gmap(lambda g: (g, 0, 0))),
            pl.BlockSpec((None, 1, SSD_HPG), gmap(lambda g: (g, 0, 0))),
            pl.BlockSpec((None, 1, SSD_HPG), gmap(lambda g: (g, 0, 0))),
            pl.BlockSpec((1, SSD_GW), gmap(lambda g: (0, g)))]


def _ssd_prompt(zx, dtg, params, bp, tp):
    m = zx.shape[0]
    nc = tp // CHUNK
    gn = SSD_GROUPS * SSD_STATE
    b_blk = 2 * SSD_INNER // gn
    wb_blk = SSD_INNER // gn
    hp3 = (SSD_GROUPS, 1, SSD_HPG)
    zero3 = lambda b, c: (0, 0, 0)
    return pl.pallas_call(
        _ssd_prompt_body,
        out_shape=(jax.ShapeDtypeStruct((m, SSD_INNER), BF16),
                   jax.ShapeDtypeStruct((bp, SSD_GROUPS, SSD_GW, SSD_STATE), F32)),
        grid=(bp, nc),
        in_specs=[pl.BlockSpec((CHUNK, SSD_INNER), lambda b, c: (b * nc + c, 0)),
                  pl.BlockSpec((CHUNK, SSD_INNER), lambda b, c: (b * nc + c, 1)),
                  pl.BlockSpec((CHUNK, gn), lambda b, c: (b * nc + c, b_blk)),
                  pl.BlockSpec((CHUNK, gn), lambda b, c: (b * nc + c, b_blk + 1)),
                  pl.BlockSpec((SSD_GROUPS, CHUNK, SSD_HPG), lambda b, c: (0, b * nc + c, 0)),
                  pl.BlockSpec((None, CONV_K, SSD_INNER), zero3),
                  pl.BlockSpec((None, CONV_K, gn), lambda b, c: (0, 0, wb_blk)),
                  pl.BlockSpec((None, CONV_K, gn), lambda b, c: (0, 0, wb_blk + 1)),
                  pl.BlockSpec((None, 1, SSD_INNER), zero3),
                  pl.BlockSpec((None, 1, gn), lambda b, c: (0, 0, wb_blk)),
                  pl.BlockSpec((None, 1, gn), lambda b, c: (0, 0, wb_blk + 1)),
                  pl.BlockSpec(hp3, zero3), pl.BlockSpec(hp3, zero3), pl.BlockSpec(hp3, zero3),
                  pl.BlockSpec((1, SSD_INNER), lambda b, c: (0, 0))],
        out_specs=(pl.BlockSpec((CHUNK, SSD_INNER), lambda b, c: (b * nc + c, 0)),
                   pl.BlockSpec((None, SSD_GROUPS, SSD_GW, SSD_STATE), lambda b, c: (b, 0, 0, 0))),
        scratch_shapes=[pltpu.VMEM((CHUNK + 8, SSD_INNER), F32), pltpu.VMEM((CHUNK + 8, gn), F32),
                        pltpu.VMEM((CHUNK + 8, gn), F32), pltpu.VMEM((SSD_GROUPS, SSD_GW, SSD_STATE), F32)],
        compiler_params=_cparams(2),
    )(zx, zx, zx, zx, dtg, *params)


def _ssd_sample_body(z_ref, x_ref, b_ref, c_ref, dt_ref, px_ref, pb_ref, pc_ref, wx_ref, wb_ref, wc_ref,
                     bx_ref, bb_ref, bc_ref, dtb_ref, al_ref, ds_ref, ng_ref, h0_ref, yin_ref,
                     y_ref, h_ref, nx_ref, nb_ref, nc_ref):
    del yin_ref

    def conv(raw_ref, buf_ref, new_ref, w_ref, bias_ref):
        raw = raw_ref[...]
        rows, width = raw.shape
        t = _row_pos(rows, SAMPLE_T, width)
        acc = w_ref[3:4, :] * raw + bias_ref[...]
        for k in range(1, CONV_K):
            prev = jnp.concatenate([piece for i in range(SAMPLE_NB) for piece in
                                    (buf_ref[i, CONV_K - 1 - k:CONV_K - 1, :], jnp.zeros((SAMPLE_T - k, width), F32))],
                                   axis=0)
            acc = acc + w_ref[3 - k:4 - k, :] * jnp.where(t >= k, pltpu.roll(raw, k, 0), prev)
        for i in range(SAMPLE_NB):
            new_ref[i] = raw[i * SAMPLE_T + SAMPLE_T - (CONV_K - 1):(i + 1) * SAMPLE_T, :]
        return jax.nn.silu(acc)

    xs = conv(x_ref, px_ref, nx_ref, wx_ref, bx_ref)
    bs = conv(b_ref, pb_ref, nb_ref, wb_ref, bb_ref)
    cs = conv(c_ref, pc_ref, nc_ref, wc_ref, bc_ref)
    y, h_new = _ssd_chunk(xs, bs, cs, z_ref[...], dt_ref[...], dtb_ref[...], al_ref[...], ds_ref[...],
                          ng_ref[...], [h0_ref[i] for i in range(SAMPLE_NB)], SAMPLE_T, _head_expand_matrix())
    for i in range(SAMPLE_NB):
        h_ref[i] = h_new[i]
    y_ref[...] = y.astype(y_ref.dtype)


def _ssd_sample(zx, dtg, conv_state, params, state, y_all, rows_p, bs):
    rows = SAMPLE_NB * SAMPLE_T
    rb0 = rows_p // rows
    xb0, bb0, cb0 = _ssd_col_blocks()
    wbb0 = SSD_INNER // SSD_STATE
    wcb0 = wbb0 + SSD_GROUPS
    gn = SSD_GROUPS * SSD_STATE
    nk = CONV_K - 1

    def gmap(f):
        return lambda bb, g: f(g)
    st_spec = pl.BlockSpec((None, SAMPLE_NB, None, SSD_GW, SSD_STATE), lambda bb, g: (0, bb, g, 0, 0))
    new_x_spec = pl.BlockSpec((SAMPLE_NB, nk, SSD_GW), lambda bb, g: (bb, 0, g))
    new_n_spec = pl.BlockSpec((SAMPLE_NB, nk, SSD_STATE), lambda bb, g: (bb, 0, g))
    return pl.pallas_call(
        _ssd_sample_body,
        out_shape=(jax.ShapeDtypeStruct(y_all.shape, y_all.dtype), jax.ShapeDtypeStruct(state.shape, F32),
                   jax.ShapeDtypeStruct((bs, nk, SSD_INNER), F32), jax.ShapeDtypeStruct((bs, nk, gn), F32),
                   jax.ShapeDtypeStruct((bs, nk, gn), F32)),
        grid=(bs // SAMPLE_NB, SSD_GROUPS),
        in_specs=[pl.BlockSpec((rows, SSD_GW), lambda bb, g: (rb0 + bb, g)),
                  pl.BlockSpec((rows, SSD_GW), lambda bb, g: (rb0 + bb, xb0 + g)),
                  pl.BlockSpec((rows, SSD_STATE), lambda bb, g: (rb0 + bb, bb0 + g)),
                  pl.BlockSpec((rows, SSD_STATE), lambda bb, g: (rb0 + bb, cb0 + g)),
                  pl.BlockSpec((None, rows, SSD_HPG), lambda bb, g: (g, rb0 + bb, 0)),
                  pl.BlockSpec((None, SAMPLE_NB, nk, SSD_GW), lambda bb, g: (0, bb, 0, g)),
                  pl.BlockSpec((None, SAMPLE_NB, nk, SSD_STATE), lambda bb, g: (0, bb, 0, wbb0 + g)),
                  pl.BlockSpec((None, SAMPLE_NB, nk, SSD_STATE), lambda bb, g: (0, bb, 0, wcb0 + g))]
                 + _ssd_param_specs(gmap) + [st_spec, pl.BlockSpec(memory_space=pl.ANY)],
        out_specs=(pl.BlockSpec((rows, SSD_GW), lambda bb, g: (rb0 + bb, g)), st_spec,
                   new_x_spec, new_n_spec, new_n_spec),
        input_output_aliases={19: 0},
        compiler_params=_cparams(2),
    )(zx, zx, zx, zx, dtg, conv_state, conv_state, conv_state, *params, state, y_all)


def _s5_body(*refs, sub, nsub, n_scan, has_h0, has_alias):
    it = iter(refs)
    x_ref, wdc_ref, kdc_ref, vtc_ref = next(it), next(it), next(it), next(it)
    are_ref, aim_ref, sre_ref, sim_ref, dsk_ref = (next(it) for _ in range(5))
    h0_ref = next(it) if has_h0 else None
    if has_alias:
        next(it)
    s_ref, hl_ref = next(it), next(it)
    m1_ref, vt_ref = next(it), next(it)
    hw2 = 2 * S5_HW

    @pl.when(pl.program_id(1) == 0)
    def _():
        def iota(shape, axis):
            return lax.broadcasted_iota(jnp.int32, shape, axis)

        def expand(compact, spread, keep):
            return jnp.where(keep, _dot(compact.astype(BF16), spread), 0.0).astype(BF16)

        q, col = iota((128, hw2), 0), iota((128, hw2), 1)
        spread = jnp.where(((q >> 6) == (col >> 9)) & ((q & 63) == (col & 63)), 1.0, 0.0).astype(BF16)
        row, col = iota((sub * 128, hw2), 0), iota((sub * 128, hw2), 1)
        m1_ref[:, 0:hw2] = expand(wdc_ref[...], spread, ((row >> 4) & 7) == ((col >> 6) & 7))
        q, col = iota((128, 128), 0), iota((128, 128), 1)
        spread = jnp.where((q < S5_GROUP) & (q == (col & 15)), 1.0, 0.0).astype(BF16)
        keep = (q >> 4) == (col >> 4)
        kd = [expand(kdc_ref[d], spread, keep) for d in range(sub)]
        for s in range(sub):
            for t in range(sub):
                blk = kd[t - s] if t >= s else jnp.zeros((128, 128), BF16)
                m1_ref[s * 128:(s + 1) * 128, hw2 + t * 128:hw2 + (t + 1) * 128] = blk
        q, col = iota((128, sub * 128), 0), iota((128, sub * 128), 1)
        spread = jnp.where(((q >> 4) == (col >> 7)) & ((q & 15) == (col & 15)), 1.0, 0.0).astype(BF16)
        row, col = iota((hw2, sub * 128), 0), iota((hw2, sub * 128), 1)
        vt_ref[...] = expand(vtc_ref[...], spread, ((row >> 6) & 7) == ((col >> 4) & 7))

    xs = [x_ref[pl.ds(s, nsub, stride=sub), :] for s in range(sub)]
    r = _dot(jnp.concatenate(xs, axis=1).astype(BF16), m1_ref[...])
    cb, y = r[:, :2 * S5_HW], r[:, 2 * S5_HW:]

    def cmul(h, a_re, a_im):
        return h * a_re + pltpu.roll(h, S5_HW, 1) * a_im

    if has_h0:
        hprev = h0_ref[...]
        h = cb + cmul(hprev, are_ref[...], aim_ref[...])
    else:
        h = cb
        j = lax.broadcasted_iota(jnp.int32, h.shape, 0)
        for step in range(n_scan):
            shift = 1 << step
            prev = jnp.where(j >= shift, pltpu.roll(h, shift, 0), 0.0)
            h = h + cmul(prev, sre_ref[step], sim_ref[step])
        hprev = jnp.where(j >= 1, pltpu.roll(h, 1, 0), 0.0)
    hl_ref[...] = h[nsub - 1:nsub, :] if not has_h0 else h
    y = y + _dot(hprev.astype(BF16), vt_ref[...])
    for t in range(sub):
        s_ref[pl.ds(t, nsub, stride=sub), :] = y[:, t * 128:(t + 1) * 128] + dsk_ref[...] * xs[t]


def _s5_tables(lam_re, lam_im, b_re, b_im, c_re, c_im, log_dt, sub, n_scan):
    nb, g8, gc, p = S5_NBLK, S5_G8, S5_GROUP, S5_STATE
    dt = jnp.exp(log_dt)[:, None]

    def apow(ds):
        d = jnp.asarray(ds, F32).reshape(-1, 1, 1)
        mag = jnp.exp(lam_re * dt * d)
        return mag * jnp.cos(lam_im * dt * d), mag * jnp.sin(lam_im * dt * d)

    a_re, a_im = (v[0] for v in apow([1.0]))
    den = lam_re * lam_re + lam_im * lam_im
    f_re = ((a_re - 1.0) * lam_re + a_im * lam_im) / den
    f_im = (a_im * lam_re - (a_re - 1.0) * lam_im) / den
    bb_re = f_re[..., None] * b_re - f_im[..., None] * b_im
    bb_im = f_re[..., None] * b_im + f_im[..., None] * b_re

    def lane_pad(m):
        return jnp.pad(m, [(0, 0)] * (m.ndim - 1) + [(0, 128 - m.shape[-1])])

    p_re, p_im = apow(range(sub))
    ab_re = p_re[..., None] * bb_re - p_im[..., None] * bb_im
    ab_im = p_re[..., None] * bb_im + p_im[..., None] * bb_re
    kd = jnp.einsum('gcp,dgpk->dgkc', c_re, ab_re) - jnp.einsum('gcp,dgpk->dgkc', c_im, ab_im)
    kdc = lane_pad(kd.reshape(sub, nb, g8 * gc, gc).transpose(1, 0, 2, 3))
    w = jnp.concatenate([jnp.swapaxes(ab_re, 2, 3), jnp.swapaxes(ab_im, 2, 3)], axis=3)[::-1]
    wdc = w.reshape(sub, nb, g8 * gc, 2 * p).transpose(1, 0, 2, 3).reshape(nb, sub * g8 * gc, 2 * p)
    q_re, q_im = apow(range(1, sub + 1))
    m_re = c_re[None] * q_re[:, :, None, :] - c_im[None] * q_im[:, :, None, :]
    m_im = c_re[None] * q_im[:, :, None, :] + c_im[None] * q_re[:, :, None, :]

    def rows(m):
        return m.reshape(sub, nb, g8, gc, p).transpose(1, 2, 4, 0, 3).reshape(nb, g8 * p, sub * gc)
    vtc = lane_pad(jnp.concatenate([rows(m_re), -rows(m_im)], axis=1))

    def lanes(re, im):
        n = re.shape[0]
        re = re.reshape(n, nb, 1, S5_HW).transpose(1, 0, 2, 3)
        im = im.reshape(n, nb, 1, S5_HW).transpose(1, 0, 2, 3)
        return jnp.concatenate([re, re], axis=3), jnp.concatenate([-im, im], axis=3)

    are, aim = (v[:, 0] for v in lanes(*apow([float(sub)])))
    sre, sim = lanes(*apow([float(sub * (1 << k)) for k in range(max(n_scan, 1))]))
    return wdc, kdc, vtc, are, aim, sre, sim


def _s5_core(xz, tables, d_skip, s_all, h0, *, sub, nsub, n_seq, row_block0, n_scan):
    has_h0 = h0 is not None
    rows = sub * nsub
    ns = tables[5].shape[1]
    hw2 = 2 * S5_HW
    sw = sub * 128

    def blk(shape, f):
        return pl.BlockSpec(shape, lambda gb, b: f(gb, b))
    in_specs = [blk((rows, 128), lambda gb, b: (row_block0 + b, gb)),
                blk((None, sw, 128), lambda gb, b: (gb, 0, 0)),
                blk((None, sub, 128, 128), lambda gb, b: (gb, 0, 0, 0)),
                blk((None, hw2, 128), lambda gb, b: (gb, 0, 0)),
                blk((None, 1, hw2), lambda gb, b: (gb, 0, 0)),
                blk((None, 1, hw2), lambda gb, b: (gb, 0, 0)),
                blk((None, ns, 1, hw2), lambda gb, b: (gb, 0, 0, 0)),
                blk((None, ns, 1, hw2), lambda gb, b: (gb, 0, 0, 0)),
                blk((1, 128), lambda gb, b: (0, gb))]
    args = [xz, *tables, d_skip.reshape(1, -1)]
    if has_h0:
        in_specs.append(blk((None, nsub, hw2), lambda gb, b: (gb, 0, 0)))
        args.append(h0)
        hl_shape = (S5_NBLK, nsub, hw2)
        hl_spec = blk((None, nsub, hw2), lambda gb, b: (gb, 0, 0))
    else:
        hl_shape = (S5_NBLK, n_seq, 1, hw2)
        hl_spec = blk((None, None, 1, hw2), lambda gb, b: (gb, b, 0, 0))
    aliases = {}
    if s_all is not None:
        in_specs.append(pl.BlockSpec(memory_space=pl.ANY))
        args.append(s_all)
        aliases = {len(args) - 1: 0}
    body = functools.partial(_s5_body, sub=sub, nsub=nsub, n_scan=n_scan, has_h0=has_h0,
                             has_alias=s_all is not None)
    return pl.pallas_call(
        body,
        out_shape=(jax.ShapeDtypeStruct((xz.shape[0], D_MODEL), F32), jax.ShapeDtypeStruct(hl_shape, F32)),
        grid=(S5_NBLK, n_seq),
        in_specs=in_specs,
        out_specs=(blk((rows, 128), lambda gb, b: (row_block0 + b, gb)), hl_spec),
        scratch_shapes=[pltpu.VMEM((sw, hw2 + sw), BF16), pltpu.VMEM((hw2, sw), BF16)],
        input_output_aliases=aliases,
        compiler_params=_cparams(2),
    )(*args)


def _s5_glu_body(s_ref, st_ref, z_ref, w_ref, b_ref, o_ref, gb_ref, *, tm, row_chunk):
    @pl.when(pl.program_id(1) == 0)
    def _():
        def body(i, carry):
            r0 = pl.multiple_of(i * row_chunk, 16)
            gb_ref[pl.ds(r0, row_chunk), :] = jax.nn.gelu(s_ref[pl.ds(r0, row_chunk), :]).astype(BF16)
            return carry
        lax.fori_loop(0, tm // row_chunk, body, 0)
    g = jax.nn.gelu(st_ref[...])
    o = g * jax.nn.sigmoid(_dot(gb_ref[...], w_ref[...].astype(BF16)) + b_ref[...])
    o_ref[...] = (o * jax.nn.silu(z_ref[...])).astype(o_ref.dtype)


def _s5_glu(s_all, xz, glu_w, glu_b):
    m, k = s_all.shape
    tn = 512
    tm = _largest_divisor(m, 1088, 16)
    row_chunk = _largest_divisor(tm, 272, 16)
    zb0 = k // tn
    body = functools.partial(_s5_glu_body, tm=tm, row_chunk=row_chunk)
    return pl.pallas_call(
        body,
        out_shape=jax.ShapeDtypeStruct((m, k), BF16),
        grid=(m // tm, k // tn),
        in_specs=[pl.BlockSpec((tm, k), lambda i, j: (i, 0)),
                  pl.BlockSpec((tm, tn), lambda i, j: (i, j)),
                  pl.BlockSpec((tm, tn), lambda i, j: (i, zb0 + j)),
                  pl.BlockSpec((None, k, tn), lambda i, j: (0, 0, j)),
                  pl.BlockSpec((1, tn), lambda i, j: (0, j))],
        out_specs=pl.BlockSpec((tm, tn), lambda i, j: (i, j)),
        scratch_shapes=[pltpu.VMEM((tm, k), BF16)],
        compiler_params=_cparams(2),
    )(s_all, s_all, xz, glu_w, glu_b.reshape(1, k))


def _last_conv_inputs(a, bp, tp, col0, n_cols):
    return jnp.stack([a[(b + 1) * tp - (CONV_K - 1):(b + 1) * tp, col0:col0 + n_cols] for b in range(bp)])


def _lru_layer(in_proj, out_proj, j, dims, state_conv, state_h, w_in, conv_w, conv_b, wa, ba, wx, bx, lam, w_out):
    bp, tp, bs = dims
    rows_p = bp * tp
    xz = in_proj(w_in, j)
    params = _lru_params(conv_w[j:j + 1], conv_b[j], wa[j:j + 1], wx[j:j + 1], ba[j], bx[j], lam[j])
    y, hl_p = _lru_prompt(xz, params, bp, tp)
    prevs = _conv_prev_rows(state_conv[j])
    h0_rows = jnp.repeat(state_h[j], SAMPLE_T, axis=0)
    y, h_rows = _lru_sample(xz, prevs, h0_rows, params, y, rows_p)
    x_s = xz[rows_p:, :LRU_WIDTH].reshape(bs, SAMPLE_T, LRU_WIDTH)
    outs = (_last_conv_inputs(xz, bp, tp, 0, LRU_WIDTH), x_s[:, SAMPLE_T - (CONV_K - 1):],
            hl_p.reshape(bp, LRU_WIDTH), h_rows.reshape(bs, SAMPLE_T, LRU_WIDTH)[:, SAMPLE_T - 1])
    return out_proj(y, w_out, j), outs


def _ret_layer(in_proj, out_proj, j, dims, state, w_in, gn, w_out):
    bp, tp, bs = dims
    rows_p = bp * tp
    qkvz = in_proj(w_in, j)
    y, s_p = _ret_prompt(qkvz, gn[j], bp, tp)
    y, s_s = _ret_sample(qkvz, gn[j], state[j:j + 1], y, rows_p, bs)
    return out_proj(y, w_out, j, tn=256), (s_p, s_s[0])


def _ssd_layer(in_proj, out_proj, j, dims, state_conv, state, w_in, conv_w, conv_b, dt_bias, a_log, d_skip,
               norm_g, w_out):
    bp, tp, bs = dims
    rows_p = bp * tp
    m = rows_p + bs * SAMPLE_T
    n_main = SSD_INNER + SSD_CONV_DIM
    w_t = jnp.swapaxes(w_in, 1, 2)
    zx = in_proj(w_t, j, n_cols=n_main, w_transposed=True)
    dt = in_proj(w_t[j:j + 1, n_main:, :], 0, w_transposed=True)
    dtg = dt.reshape(m, SSD_GROUPS, SSD_HPG).transpose(1, 0, 2)
    params = _ssd_param_arrays(conv_w[j:j + 1], conv_b[j], dt_bias[j], a_log[j], d_skip[j], norm_g[j])
    y, hl_p = _ssd_prompt(zx, dtg, params, bp, tp)
    st = state[j:j + 1].reshape(1, bs, SSD_GROUPS, SSD_GW, SSD_STATE)
    y, hl_s, *new_conv = _ssd_sample(zx, dtg, state_conv[j:j + 1], params, st, y, rows_p, bs)
    shape = (SSD_HEADS, SSD_HEADDIM, SSD_STATE)
    outs = (_last_conv_inputs(zx, bp, tp, SSD_INNER, SSD_CONV_DIM), jnp.concatenate(new_conv, axis=2),
            hl_p.reshape((bp,) + shape), hl_s.reshape((bs,) + shape))
    return out_proj(y, w_out, j, tn=256), outs


S5_SUB = 8


def _s5_layer(in_proj, out_proj, j, dims, h0_re, h0_im, w_in, lam_re, lam_im, b_re, b_im, c_re, c_im, d_skip,
              log_dt, glu_w, glu_b, w_out):
    bp, tp, bs = dims
    rows_p = bp * tp
    xz = in_proj(w_in, j)
    nsub_p = tp // S5_SUB
    n_scan = max(nsub_p - 1, 0).bit_length()
    par = (lam_re[j], lam_im[j], b_re[j], b_im[j], c_re[j], c_im[j], log_dt[j])
    s_all, hl_p = _s5_core(xz, _s5_tables(*par, S5_SUB, n_scan), d_skip[j], None, None,
                           sub=S5_SUB, nsub=nsub_p, n_seq=bp, row_block0=0, n_scan=n_scan)

    def to_lanes(v):
        return v.reshape(bs, S5_NBLK, S5_HW).transpose(1, 0, 2)
    h0 = jnp.concatenate([to_lanes(h0_re[j]), to_lanes(h0_im[j])], axis=2)
    s_all, hl_s = _s5_core(xz, _s5_tables(*par, SAMPLE_T, 0), d_skip[j], s_all, h0,
                           sub=SAMPLE_T, nsub=bs, n_seq=1, row_block0=rows_p // (bs * SAMPLE_T), n_scan=0)
    y = _s5_glu(s_all, xz, glu_w[j:j + 1], glu_b[j])

    def from_lanes(v, nb):
        return v.transpose(1, 0, 2).reshape(nb, S5_GROUPS, S5_STATE)
    hl_p = hl_p.reshape(S5_NBLK, bp, 2 * S5_HW)
    outs = (from_lanes(hl_p[..., :S5_HW], bp), from_lanes(hl_s[..., :S5_HW], bs),
            from_lanes(hl_p[..., S5_HW:], bp), from_lanes(hl_s[..., S5_HW:], bs))
    return out_proj(y, w_out, j), outs


def kernel(x_prompt, x_sample, state_lru_conv, state_lru_h, state_ret, state_ssd_conv, state_ssd, state_s5_re, state_s5_im, cache_mem_k, cache_mem_v, mem_prompt, mix_norm, xa_norm, xa_mem_norm, xa_wq, xa_wkv, xa_wo, final_norm, lru_w_in, lru_conv_w, lru_conv_b, lru_wa, lru_ba, lru_wx, lru_bx, lru_lambda, lru_w_out, ret_w_in, ret_gn, ret_w_out, ssd_w_in, ssd_conv_w, ssd_conv_b, ssd_dt_bias, ssd_a_log, ssd_d, ssd_norm, ssd_w_out, s5_w_in, s5_lambda_re, s5_lambda_im, s5_b_re, s5_b_im, s5_c_re, s5_c_im, s5_d, s5_log_dt, s5_glu_w, s5_glu_b, s5_w_out):
    bp, tp, d = x_prompt.shape
    bs, ts, _ = x_sample.shape
    depth = mix_norm.shape[0]
    assert d == D_MODEL and ts == SAMPLE_T and tp % LRU_TC == 0 and bs % SAMPLE_NB == 0
    rows_p, rows_s = bp * tp, bs * ts
    assert rows_p % rows_s == 0
    dims = (bp, tp, bs)
    h = jnp.concatenate([x_prompt.reshape(rows_p, d), x_sample.reshape(rows_s, d)], axis=0)
    mem = mem_prompt.reshape(bp * MEM_LEN, d)
    mix_g = mix_norm.reshape(depth, 1, d)
    xa_g = xa_norm.reshape(depth, 1, d)
    mem_g = xa_mem_norm.reshape(depth, 1, d)

    outs = {k: [] for k in ("lru", "ret", "ssd", "s5")}
    mem_k, mem_v = [], []
    xn = None
    for i in range(depth):
        kind, j = i % 4, i // 4

        def normed_proj(gain, w, idx, h=h, xn=xn, i=i, **kw):
            if xn is not None:
                return _matmul(xn, w, idx, **kw)
            return _matmul(h, w, idx, norm_g=gain, g_idx=i, **kw)

        def out_proj(y, w, idx, h=h, i=i, **kw):
            if w.shape[1] == D_MODEL:
                return _matmul_res_norm(y, w, idx, h, xa_g, i)
            return _matmul(y, w, idx, residual=h, **kw), None

        in_proj = functools.partial(normed_proj, mix_g)
        if kind == 0:
            (h, xn), o = _lru_layer(in_proj, out_proj, j, dims, state_lru_conv, state_lru_h, lru_w_in,
                                    lru_conv_w, lru_conv_b, lru_wa, lru_ba, lru_wx, lru_bx, lru_lambda, lru_w_out)
            outs["lru"].append(o)
        elif kind == 1:
            (h, xn), o = _ret_layer(in_proj, out_proj, j, dims, state_ret, ret_w_in, ret_gn, ret_w_out)
            outs["ret"].append(o)
        elif kind == 2:
            (h, xn), o = _ssd_layer(in_proj, out_proj, j, dims, state_ssd_conv, state_ssd, ssd_w_in,
                                    ssd_conv_w, ssd_conv_b, ssd_dt_bias, ssd_a_log, ssd_d, ssd_norm, ssd_w_out)
            outs["ssd"].append(o)
        else:
            (h, xn), o = _s5_layer(in_proj, out_proj, j, dims, state_s5_re, state_s5_im, s5_w_in,
                                   s5_lambda_re, s5_lambda_im, s5_b_re, s5_b_im, s5_c_re, s5_c_im, s5_d, s5_log_dt,
                                   s5_glu_w, s5_glu_b, s5_w_out)
            outs["s5"].append(o)
        kv = _matmul(mem, xa_wkv, i, norm_g=mem_g, g_idx=i)
        q = normed_proj(xa_g, xa_wq, i, h=h, xn=xn, out_dtype=BF16)
        o_att = _xattn_prompt(q, kv, bp, tp)
        o_att = _xattn_sample(q, cache_mem_k, cache_mem_v, i, o_att, rows_p, bs)
        if i + 1 < depth:
            h, xn = _matmul_res_norm(o_att, xa_wo, i, h, mix_g, i + 1)
        else:
            h, xn = _matmul(o_att, xa_wo, i, residual=h), None
        mem_k.append(kv[:, :d].reshape(bp, MEM_LEN, XA_HEADS, XA_HD))
        mem_v.append(kv[:, d:].reshape(bp, MEM_LEN, XA_HEADS, XA_HD))
    y_p = _rmsnorm(h, final_norm, 0, rows_p)
    y_s = _rmsnorm(h, final_norm, rows_p, rows_s)

    def stack(kind, idx):
        return jnp.stack([o[idx] for o in outs[kind]])
    return (y_p.reshape(bp, tp, d), y_s.reshape(bs, ts, d),
            stack("lru", 0), stack("lru", 1), stack("lru", 2), stack("lru", 3),
            stack("ret", 0), stack("ret", 1),
            stack("ssd", 0), stack("ssd", 1), stack("ssd", 2), stack("ssd", 3),
            stack("s5", 0), stack("s5", 1), stack("s5", 2), stack("s5", 3),
            jnp.stack(mem_k), jnp.stack(mem_v))
```

```python
import functools
import math

import jax
import jax.numpy as jnp
from jax import lax
from jax.experimental import pallas as pl
from jax.experimental.pallas import tpu as pltpu

F32 = jnp.float32
BF16 = jnp.bfloat16

D_MODEL = 2048
PAST_LEN = 16384
EPS = 1e-6
CONV_K = 4
CHUNK = 128
LRU_WIDTH = D_MODEL
LRU_BLOCK = 256
LRU_C = 8.0
RET_HEADS = 8
RET_DK = 256
RET_DV = 512
RET_VW = RET_HEADS * RET_DV
ROPE_BASE = 10000.0
SSD_INNER = 2 * D_MODEL
SSD_HEADDIM = 64
SSD_HEADS = 64
SSD_GROUPS = 8
SSD_HPG = 8
SSD_STATE = 128
SSD_GW = SSD_HPG * SSD_HEADDIM
SSD_CONV_DIM = SSD_INNER + 2 * SSD_GROUPS * SSD_STATE
S5_GROUP = 16
S5_GROUPS = 128
S5_STATE = 64
S5_G8 = 8
S5_NBLK = S5_GROUPS // S5_G8
S5_HW = S5_G8 * S5_STATE
MEM_LEN = 256
XA_HEADS = 4
XA_HD = 512
SAMPLE_T = 4
SAMPLE_NB = 4

VMEM_LIMIT_BYTES = 56 * 1024 * 1024
MM_MAX_ROWS = 2176
MM_X_DOUBLE_BUFFER_BYTES = 18 * 1024 * 1024
NT_DIMS = (((1,), (1,)), ((), ()))
TN_DIMS = (((0,), (0,)), ((), ()))


def _cparams(n_axes):
    return pltpu.CompilerParams(dimension_semantics=("arbitrary",) * n_axes,
                                vmem_limit_bytes=VMEM_LIMIT_BYTES)


def _dot(a, b):
    return jnp.dot(a, b, preferred_element_type=F32)


def _dot_nt(a, b):
    return lax.dot_general(a, b, NT_DIMS, preferred_element_type=F32)


def _dot_tn(a, b):
    return lax.dot_general(a, b, TN_DIMS, preferred_element_type=F32)


def _largest_divisor(n, cap, mult):
    for d in range(min(cap, n) // mult * mult, 0, -mult):
        if n % d == 0:
            return d
    raise ValueError(f"no divisor of {n} that is a multiple of {mult}")


def _expm1(x):
    return jnp.where(jnp.abs(x) < 0.5, jnp.tanh(0.5 * x) * (jnp.exp(x) + 1.0), jnp.exp(x) - 1.0)


def _expand_cols(v, width):
    rows, n = v.shape
    return jnp.concatenate([jnp.broadcast_to(v[:, r:r + 1], (rows, width)) for r in range(n)], axis=1)


def _mm_body(*refs, has_norm, has_res, stage_x, tm, row_chunk, w_transposed):
    it = iter(refs)
    x_ref, w_ref = next(it), next(it)
    g_ref = next(it) if has_norm else None
    r_ref = next(it) if has_res else None
    o_ref = next(it)
    if stage_x:
        xb_ref = next(it)

        @pl.when(pl.program_id(1) == 0)
        def _():
            def body(i, carry):
                r0 = pl.multiple_of(i * row_chunk, 16)
                x = x_ref[pl.ds(r0, row_chunk), :].astype(F32)
                if has_norm:
                    x = x * lax.rsqrt(jnp.mean(x * x, axis=-1, keepdims=True) + EPS) * g_ref[...]
                xb_ref[pl.ds(r0, row_chunk), :] = x.astype(BF16)
                return carry
            lax.fori_loop(0, tm // row_chunk, body, 0)
        xb = xb_ref[...]
    else:
        xb = x_ref[...]
    wb = w_ref[...].astype(BF16)
    acc = _dot_nt(xb, wb) if w_transposed else _dot(xb, wb)
    if has_res:
        acc = acc + r_ref[...]
    o_ref[...] = acc.astype(o_ref.dtype)


def _matmul(x, w, w_idx, *, norm_g=None, g_idx=0, residual=None, out_dtype=F32, tn=512, col_off=0, n_cols=None,
            w_transposed=False):
    m, k = x.shape
    n_total = w.shape[1] if w_transposed else w.shape[2]
    n_cols = n_total if n_cols is None else n_cols
    tn = min(tn, n_cols)
    assert n_cols % tn == 0 and col_off % tn == 0
    tm = _largest_divisor(m, MM_MAX_ROWS, 16)
    stage_x = x.dtype != BF16 or norm_g is not None
    row_chunk = _largest_divisor(tm, 272, 16)
    cb = col_off // tn
    x_bytes = tm * k * x.dtype.itemsize
    x_mode = {} if 2 * x_bytes <= MM_X_DOUBLE_BUFFER_BYTES else {"pipeline_mode": pl.Buffered(1)}
    w_spec = (pl.BlockSpec((None, tn, k), lambda i, j: (w_idx, j + cb, 0)) if w_transposed
              else pl.BlockSpec((None, k, tn), lambda i, j: (w_idx, 0, j + cb)))
    in_specs = [pl.BlockSpec((tm, k), lambda i, j: (i, 0), **x_mode), w_spec]
    args = [x, w]
    if norm_g is not None:
        in_specs.append(pl.BlockSpec((None, 1, k), lambda i, j: (g_idx, 0, 0)))
        args.append(norm_g)
    if residual is not None:
        in_specs.append(pl.BlockSpec((tm, tn), lambda i, j: (i, j)))
        args.append(residual)
    body = functools.partial(_mm_body, has_norm=norm_g is not None, has_res=residual is not None,
                             stage_x=stage_x, tm=tm, row_chunk=row_chunk, w_transposed=w_transposed)
    return pl.pallas_call(
        body,
        out_shape=jax.ShapeDtypeStruct((m, n_cols), out_dtype),
        grid=(m // tm, n_cols // tn),
        in_specs=in_specs,
        out_specs=pl.BlockSpec((tm, tn), lambda i, j: (i, j)),
        scratch_shapes=[pltpu.VMEM((tm, k), BF16)] if stage_x else [],
        compiler_params=_cparams(2),
    )(*args)


def _mm_res_norm_body(x_ref, w_ref, r_ref, g_ref, h_ref, xn_ref, *, tm, tn, n_tiles, row_chunk):
    j = pl.program_id(1)
    acc = _dot(x_ref[...], w_ref[...].astype(BF16)) + r_ref[...]
    for jj in range(n_tiles):
        @pl.when(j == jj)
        def _():
            h_ref[:, jj * tn:(jj + 1) * tn] = acc

    @pl.when(j == n_tiles - 1)
    def _():
        def body(i, carry):
            r0 = pl.multiple_of(i * row_chunk, 16)
            h = h_ref[pl.ds(r0, row_chunk), :]
            hn = h * lax.rsqrt(jnp.mean(h * h, axis=-1, keepdims=True) + EPS) * g_ref[...]
            xn_ref[pl.ds(r0, row_chunk), :] = hn.astype(BF16)
            return carry
        lax.fori_loop(0, tm // row_chunk, body, 0)


MM_NORM_OUT_ROWS = 1088


def _matmul_res_norm(x, w, w_idx, residual, next_g, g_idx, *, tn=512):
    m, k = x.shape
    n = w.shape[2]
    tm = _largest_divisor(m, MM_NORM_OUT_ROWS, 16)
    row_chunk = _largest_divisor(tm, 272, 16)
    n_tiles = n // tn
    body = functools.partial(_mm_res_norm_body, tm=tm, tn=tn, n_tiles=n_tiles, row_chunk=row_chunk)
    return pl.pallas_call(
        body,
        out_shape=(jax.ShapeDtypeStruct((m, n), F32), jax.ShapeDtypeStruct((m, n), BF16)),
        grid=(m // tm, n_tiles),
        in_specs=[pl.BlockSpec((tm, k), lambda i, j: (i, 0)),
                  pl.BlockSpec((None, k, tn), lambda i, j: (w_idx, 0, j)),
                  pl.BlockSpec((tm, tn), lambda i, j: (i, j)),
                  pl.BlockSpec((None, 1, n), lambda i, j: (g_idx, 0, 0))],
        out_specs=(pl.BlockSpec((tm, n), lambda i, j: (i, 0)), pl.BlockSpec((tm, n), lambda i, j: (i, 0))),
        compiler_params=_cparams(2),
    )(x, w, residual, next_g)


def _rmsnorm_body(x_ref, g_ref, o_ref):
    x = x_ref[...]
    o_ref[...] = x * lax.rsqrt(jnp.mean(x * x, axis=-1, keepdims=True) + EPS) * g_ref[...]


def _rmsnorm(x, g, row0, n_rows):
    k = x.shape[1]
    tm = _largest_divisor(math.gcd(n_rows, row0) if row0 else n_rows, 512, 8)
    rb0 = row0 // tm
    return pl.pallas_call(
        _rmsnorm_body,
        out_shape=jax.ShapeDtypeStruct((n_rows, k), F32),
        grid=(n_rows // tm,),
        in_specs=[pl.BlockSpec((tm, k), lambda i: (rb0 + i, 0)), pl.BlockSpec((1, k), lambda i: (0, 0))],
        out_specs=pl.BlockSpec((tm, k), lambda i: (i, 0)),
        compiler_params=_cparams(1),
    )(x, g.reshape(1, k))


def _softmax_rows(s):
    e = jnp.exp(s - jnp.max(s, axis=-1, keepdims=True))
    return e / jnp.sum(e, axis=-1, keepdims=True)


def _xattn_prompt_body(q_ref, k_ref, v_ref, o_ref):
    kb = k_ref[...].astype(BF16)
    vb = v_ref[...].astype(BF16)
    for h in range(XA_HEADS):
        sl = slice(h * XA_HD, (h + 1) * XA_HD)
        p = _softmax_rows(_dot_nt(q_ref[:, sl], kb[:, sl]) * XA_HD ** -0.5)
        o_ref[:, sl] = _dot(p.astype(BF16), vb[:, sl]).astype(o_ref.dtype)


def _xattn_prompt(q, kv, bp, tp):
    m = q.shape[0]
    tq = min(tp, 1024)
    nq = tp // tq
    return pl.pallas_call(
        _xattn_prompt_body,
        out_shape=jax.ShapeDtypeStruct((m, D_MODEL), BF16),
        grid=(bp, nq),
        in_specs=[pl.BlockSpec((tq, D_MODEL), lambda b, i: (b * nq + i, 0)),
                  pl.BlockSpec((MEM_LEN, D_MODEL), lambda b, i: (b, 0)),
                  pl.BlockSpec((MEM_LEN, D_MODEL), lambda b, i: (b, 1))],
        out_specs=pl.BlockSpec((tq, D_MODEL), lambda b, i: (b * nq + i, 0)),
        compiler_params=_cparams(2),
    )(q, kv, kv)


def _log2(n):
    assert n & (n - 1) == 0, n
    return n.bit_length() - 1


def _row_segment(rows, seg_len, width):
    return lax.broadcasted_iota(jnp.int32, (rows, width), 0) >> _log2(seg_len)


def _row_pos(rows, seg_len, width):
    return lax.broadcasted_iota(jnp.int32, (rows, width), 0) & (seg_len - 1)


XA_SNB = 4


def _xattn_sample_body(q_ref, k_ref, v_ref, oin_ref, o_ref, acc_ref):
    del oin_ref
    part = pl.program_id(1)
    rows = SAMPLE_NB * SAMPLE_T

    @pl.when(part == 0)
    def _():
        acc_ref[...] = jnp.zeros_like(acc_ref)

    lane_blocks = XA_HD // 128

    def head(ref, i, h):
        return jnp.concatenate([ref[i, pl.ds(j * XA_HEADS + h, MEM_LEN, stride=lane_blocks * XA_HEADS), :]
                                for j in range(lane_blocks)], axis=1).astype(BF16)

    pairs = [(i, h) for i in range(XA_SNB) for h in range(XA_HEADS)]
    s = jnp.concatenate([_dot_nt(q_ref[:, h * XA_HD:(h + 1) * XA_HD], head(k_ref, i, h)) for i, h in pairs], axis=0)
    p = _softmax_rows(s * XA_HD ** -0.5).astype(BF16)
    seg = _row_segment(rows, SAMPLE_T, XA_HD)
    for h in range(XA_HEADS):
        sl = slice(h * XA_HD, (h + 1) * XA_HD)
        o = acc_ref[:, sl]
        for i in range(XA_SNB):
            n = i * XA_HEADS + h
            o = jnp.where(seg == part * XA_SNB + i, _dot(p[n * rows:(n + 1) * rows], head(v_ref, i, h)), o)
        acc_ref[:, sl] = o

    @pl.when(part == pl.num_programs(1) - 1)
    def _():
        o_ref[...] = acc_ref[...].astype(o_ref.dtype)


def _xattn_sample(q, cache_k, cache_v, layer, o_all, rows_p, bs):
    rows = SAMPLE_NB * SAMPLE_T
    rb0 = rows_p // rows
    nparts = SAMPLE_NB // XA_SNB
    depth = cache_k.shape[0]
    lane_blocks = XA_HD // 128
    kv_rows = MEM_LEN * lane_blocks * XA_HEADS

    def relayout(c):
        c = c.reshape(depth, bs, MEM_LEN, XA_HEADS, lane_blocks, 128).transpose(0, 1, 2, 4, 3, 5)
        return c.reshape(depth, bs, kv_rows, 128)
    cache_k, cache_v = relayout(cache_k), relayout(cache_v)
    kv_spec = pl.BlockSpec((None, XA_SNB, kv_rows, 128), lambda bb, s: (layer, bb * nparts + s, 0, 0))
    return pl.pallas_call(
        _xattn_sample_body,
        out_shape=jax.ShapeDtypeStruct(o_all.shape, o_all.dtype),
        grid=(bs // SAMPLE_NB, nparts),
        in_specs=[pl.BlockSpec((rows, D_MODEL), lambda bb, s: (rb0 + bb, 0)), kv_spec, kv_spec,
                  pl.BlockSpec(memory_space=pl.ANY)],
        out_specs=pl.BlockSpec((rows, D_MODEL), lambda bb, s: (rb0 + bb, 0)),
        scratch_shapes=[pltpu.VMEM((rows, D_MODEL), F32)],
        input_output_aliases={3: 0},
        compiler_params=_cparams(2),
    )(q, cache_k, cache_v, o_all)


LRU_CW = 512
LRU_TC = 256


def _lru_gate_scan(xc, wa_ref, wx_ref, ba, bx, lam, h0, seg_len):
    rows = xc.shape[0]
    rs, gis = [], []
    for n in range(LRU_CW // LRU_BLOCK):
        xb = xc[:, n * LRU_BLOCK:(n + 1) * LRU_BLOCK].astype(BF16)
        rs.append(_dot(xb, wa_ref[n].astype(BF16)))
        gis.append(_dot(xb, wx_ref[n].astype(BF16)))
    r = jax.nn.sigmoid(jnp.concatenate(rs, axis=1) + ba)
    gi = jax.nn.sigmoid(jnp.concatenate(gis, axis=1) + bx)
    log_a = -LRU_C * r * jax.nn.softplus(-lam)
    a = jnp.exp(log_a)
    b = jnp.sqrt(-_expm1(2.0 * log_a)) * (gi * xc)
    t = _row_pos(rows, seg_len, LRU_CW)
    shift = 1
    while shift < seg_len:
        keep = t >= shift
        a_prev = pltpu.roll(a, shift, 0)
        b_prev = pltpu.roll(b, shift, 0)
        b = jnp.where(keep, a * b_prev + b, b)
        a = jnp.where(keep, a * a_prev, a)
        shift *= 2
    return b + a * h0


def _lru_prompt_body(x_ref, z_ref, cw_ref, cb_ref, wa_ref, wx_ref, ba_ref, bx_ref, lam_ref,
                     y_ref, hl_ref, xext_ref, hc_ref):
    c = pl.program_id(2)

    @pl.when(c == 0)
    def _():
        xext_ref[0:8, :] = jnp.zeros((8, LRU_CW), F32)
        hc_ref[...] = jnp.zeros_like(hc_ref)

    x = x_ref[...]
    xext_ref[8:8 + LRU_TC, :] = x
    xc = cw_ref[3:4, :] * x + cb_ref[...]
    for k in range(CONV_K - 1):
        xc = xc + cw_ref[k:k + 1, :] * xext_ref[5 + k:5 + k + LRU_TC, :]
    xext_ref[0:8, :] = x[LRU_TC - 8:LRU_TC, :]
    h = _lru_gate_scan(xc, wa_ref, wx_ref, ba_ref[...], bx_ref[...], lam_ref[...], hc_ref[0:1, :], LRU_TC)
    hc_ref[0:1, :] = h[LRU_TC - 1:LRU_TC, :]
    y_ref[...] = (h * jax.nn.silu(z_ref[...])).astype(y_ref.dtype)

    @pl.when(c == pl.num_programs(2) - 1)
    def _():
        hl_ref[...] = h[LRU_TC - 1:LRU_TC, :]


def _lru_param_specs(n_grid):
    def cmap(block):
        if n_grid == 3:
            return lambda b, cb, c: block(cb)
        return lambda cb: block(cb)
    return [pl.BlockSpec((None, CONV_K, LRU_CW), cmap(lambda cb: (0, 0, cb))),
            pl.BlockSpec((None, 1, LRU_CW), cmap(lambda cb: (0, 0, cb))),
            pl.BlockSpec((None, LRU_CW // LRU_BLOCK, LRU_BLOCK, LRU_BLOCK), cmap(lambda cb: (0, cb, 0, 0))),
            pl.BlockSpec((None, LRU_CW // LRU_BLOCK, LRU_BLOCK, LRU_BLOCK), cmap(lambda cb: (0, cb, 0, 0))),
            pl.BlockSpec((None, 1, LRU_CW), cmap(lambda cb: (0, 0, cb))),
            pl.BlockSpec((None, 1, LRU_CW), cmap(lambda cb: (0, 0, cb))),
            pl.BlockSpec((None, 1, LRU_CW), cmap(lambda cb: (0, 0, cb)))]


def _lru_params(conv_w, conv_b, wa, wx, ba, bx, lam):
    w = LRU_WIDTH
    return [conv_w, conv_b.reshape(1, 1, w), wa, wx, ba.reshape(1, 1, w), bx.reshape(1, 1, w), lam.reshape(1, 1, w)]


def _lru_prompt(xz, params, bp, tp):
    m = xz.shape[0]
    nc = tp // LRU_TC
    ncb = LRU_WIDTH // LRU_CW
    return pl.pallas_call(
        _lru_prompt_body,
        out_shape=(jax.ShapeDtypeStruct((m, LRU_WIDTH), BF16), jax.ShapeDtypeStruct((bp, 1, LRU_WIDTH), F32)),
        grid=(bp, ncb, nc),
        in_specs=[pl.BlockSpec((LRU_TC, LRU_CW), lambda b, cb, c: (b * nc + c, cb)),
                  pl.BlockSpec((LRU_TC, LRU_CW), lambda b, cb, c: (b * nc + c, ncb + cb))] + _lru_param_specs(3),
        out_specs=(pl.BlockSpec((LRU_TC, LRU_CW), lambda b, cb, c: (b * nc + c, cb)),
                   pl.BlockSpec((None, 1, LRU_CW), lambda b, cb, c: (b, 0, cb))),
        scratch_shapes=[pltpu.VMEM((LRU_TC + 8, LRU_CW), F32), pltpu.VMEM((8, LRU_CW), F32)],
        compiler_params=_cparams(3),
    )(xz, xz, *params)


def _lru_sample_body(x_ref, z_ref, p1_ref, p2_ref, p3_ref, h0_ref, cw_ref, cb_ref, wa_ref, wx_ref,
                     ba_ref, bx_ref, lam_ref, yin_ref, y_ref, h_ref):
    del yin_ref
    x = x_ref[...]
    rows = x.shape[0]
    t = _row_pos(rows, SAMPLE_T, LRU_CW)
    xc = cw_ref[3:4, :] * x + cb_ref[...]
    for k, prev_ref in ((1, p1_ref), (2, p2_ref), (3, p3_ref)):
        xc = xc + cw_ref[3 - k:4 - k, :] * jnp.where(t >= k, pltpu.roll(x, k, 0), prev_ref[...])
    h = _lru_gate_scan(xc, wa_ref, wx_ref, ba_ref[...], bx_ref[...], lam_ref[...], h0_ref[...], SAMPLE_T)
    h_ref[...] = h
    y_ref[...] = (h * jax.nn.silu(z_ref[...])).astype(y_ref.dtype)


def _lru_sample(xz, prevs, h0_rows, params, y_all, rows_p):
    rows_s = prevs[0].shape[0]
    rb0 = rows_p // rows_s
    ncb = LRU_WIDTH // LRU_CW
    small = pl.BlockSpec((rows_s, LRU_CW), lambda cb: (0, cb))
    return pl.pallas_call(
        _lru_sample_body,
        out_shape=(jax.ShapeDtypeStruct(y_all.shape, y_all.dtype), jax.ShapeDtypeStruct((rows_s, LRU_WIDTH), F32)),
        grid=(ncb,),
        in_specs=[pl.BlockSpec((rows_s, LRU_CW), lambda cb: (rb0, cb)),
                  pl.BlockSpec((rows_s, LRU_CW), lambda cb: (rb0, ncb + cb)),
                  small, small, small, small] + _lru_param_specs(1) + [pl.BlockSpec(memory_space=pl.ANY)],
        out_specs=(pl.BlockSpec((rows_s, LRU_CW), lambda cb: (rb0, cb)), small),
        input_output_aliases={13: 0},
        compiler_params=_cparams(1),
    )(xz, xz, *prevs, h0_rows, *params, y_all)


def _conv_prev_rows(buf):
    b, _, c = buf.shape
    out = []
    for k in range(1, CONV_K):
        pad = jnp.zeros((b, SAMPLE_T - k, c), buf.dtype)
        out.append(jnp.concatenate([buf[:, CONV_K - 1 - k:], pad], axis=1).reshape(b * SAMPLE_T, c))
    return out


def _rope(x, cos, sin):
    half = RET_DK // 2
    x1, x2 = x[:, :half], x[:, half:]
    return jnp.concatenate([x1 * cos - x2 * sin, x1 * sin + x2 * cos], axis=1)


def _ret_chunk(q, k, v, z, cos, sin, dmask, qdec, kdec, cdec, gn, states, seg_len):
    rows = q.shape[0]
    qb = _rope(q, cos, sin).astype(BF16)
    kr = _rope(k, cos, sin) * RET_DK ** -0.5
    kb = kr.astype(BF16)
    vb = v.astype(BF16)
    kd = kr * kdec
    o = _dot((_dot_nt(qb, kb) * dmask).astype(BF16), vb)
    new_states = []
    single = len(states) == 1
    seg_v = None if single else _row_segment(rows, seg_len, RET_DV)
    seg_k = None if single else _row_segment(rows, seg_len, RET_DK)
    for i, s in enumerate(states):
        cross = _dot(qb, s.astype(BF16)) * qdec
        kdi = kd
        if not single:
            cross = jnp.where(seg_v == i, cross, 0.0)
            kdi = jnp.where(seg_k == i, kd, 0.0)
        o = o + cross
        new_states.append(s * cdec + _dot_tn(kdi.astype(BF16), vb))
    mu = jnp.mean(o, axis=-1, keepdims=True)
    var = jnp.mean(jnp.square(o - mu), axis=-1, keepdims=True)
    on = (o - mu) * lax.rsqrt(var + EPS) * gn
    return on * jax.nn.silu(z), new_states


def _ret_prompt_body(q_ref, k_ref, v_ref, z_ref, cos_ref, sin_ref, dm_ref, qd_ref, kd_ref, cd_ref, gn_ref,
                     y_ref, sl_ref, s_ref):
    c = pl.program_id(1)

    @pl.when(c == 0)
    def _():
        s_ref[...] = jnp.zeros_like(s_ref)

    cos, sin = cos_ref[...], sin_ref[...]
    for h in range(RET_HEADS):
        ksl = slice(h * RET_DK, (h + 1) * RET_DK)
        vsl = slice(h * RET_DV, (h + 1) * RET_DV)
        y, (s_new,) = _ret_chunk(q_ref[:, ksl], k_ref[:, ksl], v_ref[:, vsl], z_ref[:, vsl], cos, sin,
                                 dm_ref[h], qd_ref[h], kd_ref[h], cd_ref[h], gn_ref[:, vsl], [s_ref[h]], CHUNK)
        s_ref[h] = s_new
        y_ref[:, vsl] = y.astype(y_ref.dtype)

        @pl.when(c == pl.num_programs(1) - 1)
        def _():
            sl_ref[h] = s_new


def _ret_tables(seg_len, nseg, pos):
    rows = seg_len * nseg
    log_g = jnp.log1p(-jnp.exp2(-5.0 - jnp.arange(RET_HEADS, dtype=F32)))[:, None, None]
    t = (jnp.arange(rows) % seg_len).astype(F32)
    seg = jnp.arange(rows) // seg_len
    rel = t[:, None] - t[None, :]
    ok = (rel >= 0) & (seg[:, None] == seg[None, :])
    dmask = jnp.where(ok, jnp.exp(log_g * jnp.where(ok, rel, 0.0)), 0.0)
    qdec = jnp.broadcast_to(jnp.exp(log_g * (t + 1.0)[None, :, None]), (RET_HEADS, rows, RET_DV))
    kdec = jnp.broadcast_to(jnp.exp(log_g * (seg_len - 1.0 - t)[None, :, None]), (RET_HEADS, rows, RET_DK))
    cdec = jnp.broadcast_to(jnp.exp(log_g * seg_len), (RET_HEADS, 1, RET_DV))
    half = RET_DK // 2
    inv = ROPE_BASE ** (-jnp.arange(half, dtype=F32) / half)
    ang = pos.astype(F32)[:, None] * inv
    return jnp.cos(ang), jnp.sin(ang), dmask, qdec, kdec, cdec


def _ret_prompt(qkvz, gn, bp, tp):
    m = qkvz.shape[0]
    nc = tp // CHUNK
    cos, sin, dmask, qdec, kdec, cdec = _ret_tables(CHUNK, 1, jnp.arange(tp))
    qk_w = RET_HEADS * RET_DK
    full = lambda b, c: (0, 0, 0)
    return pl.pallas_call(
        _ret_prompt_body,
        out_shape=(jax.ShapeDtypeStruct((m, RET_VW), BF16),
                   jax.ShapeDtypeStruct((bp, RET_HEADS, RET_DK, RET_DV), F32)),
        grid=(bp, nc),
        in_specs=[pl.BlockSpec((CHUNK, qk_w), lambda b, c: (b * nc + c, 0)),
                  pl.BlockSpec((CHUNK, qk_w), lambda b, c: (b * nc + c, 1)),
                  pl.BlockSpec((CHUNK, RET_VW), lambda b, c: (b * nc + c, 1)),
                  pl.BlockSpec((CHUNK, RET_VW), lambda b, c: (b * nc + c, 2)),
                  pl.BlockSpec((CHUNK, RET_DK // 2), lambda b, c: (c, 0)),
                  pl.BlockSpec((CHUNK, RET_DK // 2), lambda b, c: (c, 0)),
                  pl.BlockSpec((RET_HEADS, CHUNK, CHUNK), full),
                  pl.BlockSpec((RET_HEADS, CHUNK, RET_DV), full),
                  pl.BlockSpec((RET_HEADS, CHUNK, RET_DK), full),
                  pl.BlockSpec((RET_HEADS, 1, RET_DV), full),
                  pl.BlockSpec((1, RET_VW), lambda b, c: (0, 0))],
        out_specs=(pl.BlockSpec((CHUNK, RET_VW), lambda b, c: (b * nc + c, 0)),
                   pl.BlockSpec((None, RET_HEADS, RET_DK, RET_DV), lambda b, c: (b, 0, 0, 0))),
        scratch_shapes=[pltpu.VMEM((RET_HEADS, RET_DK, RET_DV), F32)],
        compiler_params=_cparams(2),
    )(qkvz, qkvz, qkvz, qkvz, cos, sin, dmask, qdec, kdec, cdec, gn.reshape(1, RET_VW))


def _ret_sample_body(q_ref, k_ref, v_ref, z_ref, cos_ref, sin_ref, dm_ref, qd_ref, kd_ref, cd_ref, gn_ref,
                     s0_ref, yin_ref, y_ref, s_ref):
    del yin_ref
    cos, sin = cos_ref[...], sin_ref[...]
    for h in range(RET_SAMPLE_HEADS):
        ksl = slice(h * RET_DK, (h + 1) * RET_DK)
        vsl = slice(h * RET_DV, (h + 1) * RET_DV)
        y, s_new = _ret_chunk(q_ref[:, ksl], k_ref[:, ksl], v_ref[:, vsl], z_ref[:, vsl], cos, sin,
                              dm_ref[h], qd_ref[h], kd_ref[h], cd_ref[h], gn_ref[:, vsl],
                              [s0_ref[i, h] for i in range(SAMPLE_NB)], SAMPLE_T)
        for i in range(SAMPLE_NB):
            s_ref[i, h] = s_new[i]
        y_ref[:, vsl] = y.astype(y_ref.dtype)


RET_SAMPLE_HEADS = 2


def _ret_sample(qkvz, gn, state, y_all, rows_p, bs):
    rows = SAMPLE_NB * SAMPLE_T
    rb0 = rows_p // rows
    pos = PAST_LEN + jnp.arange(rows) % SAMPLE_T
    cos, sin, dmask, qdec, kdec, cdec = _ret_tables(SAMPLE_T, SAMPLE_NB, pos)
    nh = RET_SAMPLE_HEADS
    kw, vw = nh * RET_DK, nh * RET_DV
    k_blk = RET_HEADS * RET_DK // kw
    v_blk = 2 * RET_HEADS * RET_DK // vw
    z_blk = v_blk + RET_HEADS // nh
    st_spec = pl.BlockSpec((None, SAMPLE_NB, nh, RET_DK, RET_DV), lambda bb, p: (0, bb, p, 0, 0))
    return pl.pallas_call(
        _ret_sample_body,
        out_shape=(jax.ShapeDtypeStruct(y_all.shape, y_all.dtype), jax.ShapeDtypeStruct(state.shape, F32)),
        grid=(bs // SAMPLE_NB, RET_HEADS // nh),
        in_specs=[pl.BlockSpec((rows, kw), lambda bb, p: (rb0 + bb, p)),
                  pl.BlockSpec((rows, kw), lambda bb, p: (rb0 + bb, k_blk + p)),
                  pl.BlockSpec((rows, vw), lambda bb, p: (rb0 + bb, v_blk + p)),
                  pl.BlockSpec((rows, vw), lambda bb, p: (rb0 + bb, z_blk + p)),
                  pl.BlockSpec((rows, RET_DK // 2), lambda bb, p: (0, 0)),
                  pl.BlockSpec((rows, RET_DK // 2), lambda bb, p: (0, 0)),
                  pl.BlockSpec((nh, rows, rows), lambda bb, p: (p, 0, 0)),
                  pl.BlockSpec((nh, rows, RET_DV), lambda bb, p: (p, 0, 0)),
                  pl.BlockSpec((nh, rows, RET_DK), lambda bb, p: (p, 0, 0)),
                  pl.BlockSpec((nh, 1, RET_DV), lambda bb, p: (p, 0, 0)),
                  pl.BlockSpec((1, vw), lambda bb, p: (0, p)),
                  st_spec,
                  pl.BlockSpec(memory_space=pl.ANY)],
        out_specs=(pl.BlockSpec((rows, vw), lambda bb, p: (rb0 + bb, p)), st_spec),
        input_output_aliases={12: 0},
        compiler_params=_cparams(2),
    )(qkvz, qkvz, qkvz, qkvz, cos, sin, dmask, qdec, kdec, cdec, gn.reshape(1, RET_VW), state, y_all)


def _head_expand_matrix():
    r = lax.broadcasted_iota(jnp.int32, (128, SSD_GW), 0)
    c = lax.broadcasted_iota(jnp.int32, (128, SSD_GW), 1)
    return jnp.where(r == (c >> _log2(SSD_HEADDIM)), 1.0, 0.0).astype(BF16)


def _expand_heads(v, expand_mat):
    v1 = v.astype(BF16)
    r1 = v - v1.astype(F32)
    v2 = r1.astype(BF16)
    v3 = (r1 - v2.astype(F32)).astype(BF16)
    return (_dot(v1, expand_mat) + _dot(v2, expand_mat)) + _dot(v3, expand_mat)


def _ssd_chunk(xs, bs, cs, z, dt_raw, dt_bias, a_log, d_skip, norm_g, states, seg_len, expand_mat):
    rows = xs.shape[0]
    single = len(states) == 1
    hp = SSD_HPG
    pad = 128
    t8 = _row_pos(rows, seg_len, pad)
    seg8 = _row_segment(rows, seg_len, pad)
    tt = lax.broadcasted_iota(jnp.int32, (rows, rows), 0)
    ss = lax.broadcasted_iota(jnp.int32, (rows, rows), 1)
    causal = (tt >= ss) & ((tt >> _log2(seg_len)) == (ss >> _log2(seg_len)))

    def lane_pad(v):
        return jnp.concatenate([v, jnp.zeros((v.shape[0], pad - hp), F32)], axis=1)

    dt = jax.nn.softplus(lane_pad(dt_raw) + lane_pad(dt_bias))
    da = dt * (-jnp.exp(lane_pad(a_log)))
    cum = da
    shift = 1
    while shift < seg_len:
        cum = cum + jnp.where(t8 >= shift, pltpu.roll(cum, shift, 0), 0.0)
        shift *= 2
    cum_sq = cum if rows == pad else jnp.concatenate([cum, jnp.zeros((pad - rows, pad), F32)], axis=0)
    cum_t = cum_sq.T[0:hp, 0:rows]
    lasts = [cum[(i + 1) * seg_len - 1:(i + 1) * seg_len, :] for i in range(len(states))]
    last_row = lasts[0]
    if not single:
        last_row = jnp.zeros((rows, pad), F32)
        for i, l in enumerate(lasts):
            last_row = jnp.where(seg8 == i, l, last_row)
    to_end = jnp.exp(last_row - cum)
    expanded = _expand_heads(jnp.concatenate([dt, jnp.exp(cum), to_end], axis=0), expand_mat)
    dt_x, ecum, to_end_x = expanded[0:rows], expanded[rows:2 * rows], expanded[2 * rows:3 * rows]

    csb = cs.astype(BF16)
    bsb = bs.astype(BF16)
    cb = _dot_nt(csb, bsb)
    xdt = xs * dt_x
    ycols = []
    for r in range(hp):
        lmat = jnp.where(causal, jnp.exp(cum[:, r:r + 1] - cum_t[r:r + 1, :]), 0.0)
        ycols.append(_dot((cb * lmat).astype(BF16), xdt[:, r * SSD_HEADDIM:(r + 1) * SSD_HEADDIM].astype(BF16)))
    y = jnp.concatenate(ycols, axis=1)
    xte = xdt * to_end_x
    segw = None if single else _row_segment(rows, seg_len, SSD_GW)
    new_states = []
    for i, h in enumerate(states):
        y_off = _dot_nt(csb, h.astype(BF16)) * ecum
        xi = xte
        if not single:
            y_off = jnp.where(segw == i, y_off, 0.0)
            xi = jnp.where(segw == i, xte, 0.0)
        y = y + y_off
        e_last = jnp.exp(lasts[i])
        dec = jnp.concatenate([jnp.broadcast_to(e_last[:, r:r + 1], (SSD_HEADDIM, SSD_STATE)) for r in range(hp)],
                              axis=0)
        new_states.append(h * dec + _dot_tn(xi.astype(BF16), bsb))
    y = y + xs * _expand_cols(d_skip, SSD_HEADDIM)
    yg = y * jax.nn.silu(z)
    yg = yg * lax.rsqrt(jnp.mean(yg * yg, axis=-1, keepdims=True) + EPS)
    return yg * norm_g, new_states


def _ssd_prompt_body(z_ref, x_ref, b_ref, c_ref, dt_ref, wx_ref, wb_ref, wc_ref, bx_ref, bb_ref, bc_ref,
                     dtb_ref, al_ref, ds_ref, ng_ref, y_ref, hl_ref, xe_ref, be_ref, ce_ref, h_ref):
    c = pl.program_id(1)

    @pl.when(c == 0)
    def _():
        xe_ref[0:8, :] = jnp.zeros((8, SSD_INNER), F32)
        be_ref[0:8, :] = jnp.zeros((8, SSD_GROUPS * SSD_STATE), F32)
        ce_ref[0:8, :] = jnp.zeros((8, SSD_GROUPS * SSD_STATE), F32)
        h_ref[...] = jnp.zeros_like(h_ref)

    def conv(raw_ref, ext_ref, w_ref, bias_ref, sl):
        raw = raw_ref[:, sl]
        ext_ref[8:8 + CHUNK, sl] = raw
        acc = w_ref[3:4, sl] * raw + bias_ref[:, sl]
        for k in range(CONV_K - 1):
            acc = acc + w_ref[k:k + 1, sl] * ext_ref[5 + k:5 + k + CHUNK, sl]
        ext_ref[0:8, sl] = raw[CHUNK - 8:CHUNK, :]
        return jax.nn.silu(acc)

    expand_mat = _head_expand_matrix()
    for g in range(SSD_GROUPS):
        xsl = slice(g * SSD_GW, (g + 1) * SSD_GW)
        nsl = slice(g * SSD_STATE, (g + 1) * SSD_STATE)
        xs = conv(x_ref, xe_ref, wx_ref, bx_ref, xsl)
        bs = conv(b_ref, be_ref, wb_ref, bb_ref, nsl)
        cs = conv(c_ref, ce_ref, wc_ref, bc_ref, nsl)
        y, (h_new,) = _ssd_chunk(xs, bs, cs, z_ref[:, xsl], dt_ref[g], dtb_ref[g], al_ref[g], ds_ref[g],
                                 ng_ref[:, xsl], [h_ref[g]], CHUNK, expand_mat)
        h_ref[g] = h_new
        y_ref[:, xsl] = y.astype(y_ref.dtype)

        @pl.when(c == pl.num_programs(1) - 1)
        def _():
            hl_ref[g] = h_new


SSD_SAMPLE_GROUPS = 4


def _ssd_param_arrays(conv_w, conv_b, dt_bias, a_log, d_skip, norm_g):
    g, hp = SSD_GROUPS, SSD_HPG
    return [conv_w, conv_w, conv_w, conv_b.reshape(1, 1, -1), conv_b.reshape(1, 1, -1), conv_b.reshape(1, 1, -1),
            dt_bias.reshape(g, 1, hp), a_log.reshape(g, 1, hp), d_skip.reshape(g, 1, hp), norm_g.reshape(1, SSD_INNER)]


def _ssd_prompt(zx, dtg, params, bp, tp):
    m = zx.shape[0]
    nc = tp // CHUNK
    gn = SSD_GROUPS * SSD_STATE
    b_blk = 2 * SSD_INNER // gn
    wb_blk = SSD_INNER // gn
    hp3 = (SSD_GROUPS, 1, SSD_HPG)
    zero3 = lambda b, c: (0, 0, 0)
    return pl.pallas_call(
        _ssd_prompt_body,
        out_shape=(jax.ShapeDtypeStruct((m, SSD_INNER), BF16),
                   jax.ShapeDtypeStruct((bp, SSD_GROUPS, SSD_GW, SSD_STATE), F32)),
        grid=(bp, nc),
        in_specs=[pl.BlockSpec((CHUNK, SSD_INNER), lambda b, c: (b * nc + c, 0)),
                  pl.BlockSpec((CHUNK, SSD_INNER), lambda b, c: (b * nc + c, 1)),
                  pl.BlockSpec((CHUNK, gn), lambda b, c: (b * nc + c, b_blk)),
                  pl.BlockSpec((CHUNK, gn), lambda b, c: (b * nc + c, b_blk + 1)),
                  pl.BlockSpec((SSD_GROUPS, CHUNK, SSD_HPG), lambda b, c: (0, b * nc + c, 0)),
                  pl.BlockSpec((None, CONV_K, SSD_INNER), zero3),
                  pl.BlockSpec((None, CONV_K, gn), lambda b, c: (0, 0, wb_blk)),
                  pl.BlockSpec((None, CONV_K, gn), lambda b, c: (0, 0, wb_blk + 1)),
                  pl.BlockSpec((None, 1, SSD_INNER), zero3),
                  pl.BlockSpec((None, 1, gn), lambda b, c: (0, 0, wb_blk)),
                  pl.BlockSpec((None, 1, gn), lambda b, c: (0, 0, wb_blk + 1)),
                  pl.BlockSpec(hp3, zero3), pl.BlockSpec(hp3, zero3), pl.BlockSpec(hp3, zero3),
                  pl.BlockSpec((1, SSD_INNER), lambda b, c: (0, 0))],
        out_specs=(pl.BlockSpec((CHUNK, SSD_INNER), lambda b, c: (b * nc + c, 0)),
                   pl.BlockSpec((None, SSD_GROUPS, SSD_GW, SSD_STATE), lambda b, c: (b, 0, 0, 0))),
        scratch_shapes=[pltpu.VMEM((CHUNK + 8, SSD_INNER), F32), pltpu.VMEM((CHUNK + 8, gn), F32),
                        pltpu.VMEM((CHUNK + 8, gn), F32), pltpu.VMEM((SSD_GROUPS, SSD_GW, SSD_STATE), F32)],
        compiler_params=_cparams(2),
    )(zx, zx, zx, zx, dtg, *params)


def _ssd_sample_body(z_ref, x_ref, b_ref, c_ref, dt_ref, px_ref, pb_ref, pc_ref, wx_ref, wb_ref, wc_ref,
                     bx_ref, bb_ref, bc_ref, dtb_ref, al_ref, ds_ref, ng_ref, h0_ref, yin_ref,
                     y_ref, h_ref, nx_ref, nb_ref, nc_ref):
    del yin_ref

    def conv(raw_ref, buf_ref, new_ref, w_ref, bias_ref):
        raw = raw_ref[...]
        rows, width = raw.shape
        t = _row_pos(rows, SAMPLE_T, width)
        acc = w_ref[3:4, :] * raw + bias_ref[...]
        for k in range(1, CONV_K):
            prev = jnp.concatenate([piece for i in range(SAMPLE_NB) for piece in
                                    (buf_ref[i, CONV_K - 1 - k:CONV_K - 1, :], jnp.zeros((SAMPLE_T - k, width), F32))],
                                   axis=0)
            acc = acc + w_ref[3 - k:4 - k, :] * jnp.where(t >= k, pltpu.roll(raw, k, 0), prev)
        for i in range(SAMPLE_NB):
            new_ref[i] = raw[i * SAMPLE_T + SAMPLE_T - (CONV_K - 1):(i + 1) * SAMPLE_T, :]
        return jax.nn.silu(acc)

    xs_all = conv(x_ref, px_ref, nx_ref, wx_ref, bx_ref)
    bs_all = conv(b_ref, pb_ref, nb_ref, wb_ref, bb_ref)
    cs_all = conv(c_ref, pc_ref, nc_ref, wc_ref, bc_ref)
    expand_mat = _head_expand_matrix()
    for g in range(SSD_SAMPLE_GROUPS):
        xsl = slice(g * SSD_GW, (g + 1) * SSD_GW)
        nsl = slice(g * SSD_STATE, (g + 1) * SSD_STATE)
        y, h_new = _ssd_chunk(xs_all[:, xsl], bs_all[:, nsl], cs_all[:, nsl], z_ref[:, xsl], dt_ref[g], dtb_ref[g],
                              al_ref[g], ds_ref[g], ng_ref[:, xsl], [h0_ref[i, g] for i in range(SAMPLE_NB)],
                              SAMPLE_T, expand_mat)
        for i in range(SAMPLE_NB):
            h_ref[i, g] = h_new[i]
        y_ref[:, xsl] = y.astype(y_ref.dtype)


def _ssd_sample(zx, dtg, conv_state, params, state, y_all, rows_p, bs):
    rows = SAMPLE_NB * SAMPLE_T
    rb0 = rows_p // rows
    ng = SSD_SAMPLE_GROUPS
    xw, nw = ng * SSD_GW, ng * SSD_STATE
    gn = SSD_GROUPS * SSD_STATE
    nk = CONV_K - 1
    x_blk = SSD_INNER // xw
    b_blk, c_blk = 2 * SSD_INNER // nw, (2 * SSD_INNER + gn) // nw
    wb_blk, wc_blk = SSD_INNER // nw, (SSD_INNER + gn) // nw
    hp3 = (ng, 1, SSD_HPG)
    st_spec = pl.BlockSpec((None, SAMPLE_NB, ng, SSD_GW, SSD_STATE), lambda bb, q: (0, bb, q, 0, 0))
    new_x_spec = pl.BlockSpec((SAMPLE_NB, nk, xw), lambda bb, q: (bb, 0, q))
    new_n_spec = pl.BlockSpec((SAMPLE_NB, nk, nw), lambda bb, q: (bb, 0, q))
    return pl.pallas_call(
        _ssd_sample_body,
        out_shape=(jax.ShapeDtypeStruct(y_all.shape, y_all.dtype), jax.ShapeDtypeStruct(state.shape, F32),
                   jax.ShapeDtypeStruct((bs, nk, SSD_INNER), F32), jax.ShapeDtypeStruct((bs, nk, gn), F32),
                   jax.ShapeDtypeStruct((bs, nk, gn), F32)),
        grid=(bs // SAMPLE_NB, SSD_GROUPS // ng),
        in_specs=[pl.BlockSpec((rows, xw), lambda bb, q: (rb0 + bb, q)),
                  pl.BlockSpec((rows, xw), lambda bb, q: (rb0 + bb, x_blk + q)),
                  pl.BlockSpec((rows, nw), lambda bb, q: (rb0 + bb, b_blk + q)),
                  pl.BlockSpec((rows, nw), lambda bb, q: (rb0 + bb, c_blk + q)),
                  pl.BlockSpec((ng, rows, SSD_HPG), lambda bb, q: (q, rb0 + bb, 0)),
                  pl.BlockSpec((None, SAMPLE_NB, nk, xw), lambda bb, q: (0, bb, 0, q)),
                  pl.BlockSpec((None, SAMPLE_NB, nk, nw), lambda bb, q: (0, bb, 0, wb_blk + q)),
                  pl.BlockSpec((None, SAMPLE_NB, nk, nw), lambda bb, q: (0, bb, 0, wc_blk + q)),
                  pl.BlockSpec((None, CONV_K, xw), lambda bb, q: (0, 0, q)),
                  pl.BlockSpec((None, CONV_K, nw), lambda bb, q: (0, 0, wb_blk + q)),
                  pl.BlockSpec((None, CONV_K, nw), lambda bb, q: (0, 0, wc_blk + q)),
                  pl.BlockSpec((None, 1, xw), lambda bb, q: (0, 0, q)),
                  pl.BlockSpec((None, 1, nw), lambda bb, q: (0, 0, wb_blk + q)),
                  pl.BlockSpec((None, 1, nw), lambda bb, q: (0, 0, wc_blk + q)),
                  pl.BlockSpec(hp3, lambda bb, q: (q, 0, 0)),
                  pl.BlockSpec(hp3, lambda bb, q: (q, 0, 0)),
                  pl.BlockSpec(hp3, lambda bb, q: (q, 0, 0)),
                  pl.BlockSpec((1, xw), lambda bb, q: (0, q)),
                  st_spec, pl.BlockSpec(memory_space=pl.ANY)],
        out_specs=(pl.BlockSpec((rows, xw), lambda bb, q: (rb0 + bb, q)), st_spec,
                   new_x_spec, new_n_spec, new_n_spec),
        input_output_aliases={19: 0},
        compiler_params=_cparams(2),
    )(zx, zx, zx, zx, dtg, conv_state, conv_state, conv_state, *params, state, y_all)


def _s5_body(*refs, sub, nsub, n_scan, has_h0, has_alias):
    it = iter(refs)
    x_ref, wdc_ref, kdc_ref, vtc_ref = next(it), next(it), next(it), next(it)
    are_ref, aim_ref, sre_ref, sim_ref, dsk_ref = (next(it) for _ in range(5))
    h0_ref = next(it) if has_h0 else None
    if has_alias:
        next(it)
    s_ref, hl_ref = next(it), next(it)
    m1_ref, vt_ref = next(it), next(it)
    hw2 = 2 * S5_HW

    @pl.when(pl.program_id(1) == 0)
    def _():
        def iota(shape, axis):
            return lax.broadcasted_iota(jnp.int32, shape, axis)

        def expand(compact, spread, keep):
            return jnp.where(keep, _dot(compact.astype(BF16), spread), 0.0).astype(BF16)

        q, col = iota((128, hw2), 0), iota((128, hw2), 1)
        spread = jnp.where(((q >> 6) == (col >> 9)) & ((q & 63) == (col & 63)), 1.0, 0.0).astype(BF16)
        row, col = iota((sub * 128, hw2), 0), iota((sub * 128, hw2), 1)
        m1_ref[:, 0:hw2] = expand(wdc_ref[...], spread, ((row >> 4) & 7) == ((col >> 6) & 7))
        q, col = iota((128, 128), 0), iota((128, 128), 1)
        spread = jnp.where((q < S5_GROUP) & (q == (col & 15)), 1.0, 0.0).astype(BF16)
        keep = (q >> 4) == (col >> 4)
        kd = [expand(kdc_ref[d], spread, keep) for d in range(sub)]
        for s in range(sub):
            for t in range(sub):
                blk = kd[t - s] if t >= s else jnp.zeros((128, 128), BF16)
                m1_ref[s * 128:(s + 1) * 128, hw2 + t * 128:hw2 + (t + 1) * 128] = blk
        q, col = iota((128, sub * 128), 0), iota((128, sub * 128), 1)
        spread = jnp.where(((q >> 4) == (col >> 7)) & ((q & 15) == (col & 15)), 1.0, 0.0).astype(BF16)
        row, col = iota((hw2, sub * 128), 0), iota((hw2, sub * 128), 1)
        vt_ref[...] = expand(vtc_ref[...], spread, ((row >> 6) & 7) == ((col >> 4) & 7))

    xs = [x_ref[pl.ds(s, nsub, stride=sub), :] for s in range(sub)]
    r = _dot(jnp.concatenate(xs, axis=1).astype(BF16), m1_ref[...])
    cb, y = r[:, :2 * S5_HW], r[:, 2 * S5_HW:]

    def cmul(h, a_re, a_im):
        return h * a_re + pltpu.roll(h, S5_HW, 1) * a_im

    if has_h0:
        hprev = h0_ref[...]
        h = cb + cmul(hprev, are_ref[...], aim_ref[...])
    else:
        h = cb
        j = lax.broadcasted_iota(jnp.int32, h.shape, 0)
        for step in range(n_scan):
            shift = 1 << step
            prev = jnp.where(j >= shift, pltpu.roll(h, shift, 0), 0.0)
            h = h + cmul(prev, sre_ref[step], sim_ref[step])
        hprev = jnp.where(j >= 1, pltpu.roll(h, 1, 0), 0.0)
    hl_ref[...] = h[nsub - 1:nsub, :] if not has_h0 else h
    y = y + _dot(hprev.astype(BF16), vt_ref[...])
    for t in range(sub):
        s_ref[pl.ds(t, nsub, stride=sub), :] = y[:, t * 128:(t + 1) * 128] + dsk_ref[...] * xs[t]


def _s5_tables(lam_re, lam_im, b_re, b_im, c_re, c_im, log_dt, sub, n_scan):
    nb, g8, gc, p = S5_NBLK, S5_G8, S5_GROUP, S5_STATE
    dt = jnp.exp(log_dt)[:, None]

    def apow(ds):
        d = jnp.asarray(ds, F32).reshape(-1, 1, 1)
        mag = jnp.exp(lam_re * dt * d)
        return mag * jnp.cos(lam_im * dt * d), mag * jnp.sin(lam_im * dt * d)

    a_re, a_im = (v[0] for v in apow([1.0]))
    den = lam_re * lam_re + lam_im * lam_im
    f_re = ((a_re - 1.0) * lam_re + a_im * lam_im) / den
    f_im = (a_im * lam_re - (a_re - 1.0) * lam_im) / den
    bb_re = f_re[..., None] * b_re - f_im[..., None] * b_im
    bb_im = f_re[..., None] * b_im + f_im[..., None] * b_re

    def lane_pad(m):
        return jnp.pad(m, [(0, 0)] * (m.ndim - 1) + [(0, 128 - m.shape[-1])])

    p_re, p_im = apow(range(sub))
    ab_re = p_re[..., None] * bb_re - p_im[..., None] * bb_im
    ab_im = p_re[..., None] * bb_im + p_im[..., None] * bb_re
    kd = jnp.einsum('gcp,dgpk->dgkc', c_re, ab_re) - jnp.einsum('gcp,dgpk->dgkc', c_im, ab_im)
    kdc = lane_pad(kd.reshape(sub, nb, g8 * gc, gc).transpose(1, 0, 2, 3))
    w = jnp.concatenate([jnp.swapaxes(ab_re, 2, 3), jnp.swapaxes(ab_im, 2, 3)], axis=3)[::-1]
    wdc = w.reshape(sub, nb, g8 * gc, 2 * p).transpose(1, 0, 2, 3).reshape(nb, sub * g8 * gc, 2 * p)
    q_re, q_im = apow(range(1, sub + 1))
    m_re = c_re[None] * q_re[:, :, None, :] - c_im[None] * q_im[:, :, None, :]
    m_im = c_re[None] * q_im[:, :, None, :] + c_im[None] * q_re[:, :, None, :]

    def rows(m):
        return m.reshape(sub, nb, g8, gc, p).transpose(1, 2, 4, 0, 3).reshape(nb, g8 * p, sub * gc)
    vtc = lane_pad(jnp.concatenate([rows(m_re), -rows(m_im)], axis=1))

    def lanes(re, im):
        n = re.shape[0]
        re = re.reshape(n, nb, 1, S5_HW).transpose(1, 0, 2, 3)
        im = im.reshape(n, nb, 1, S5_HW).transpose(1, 0, 2, 3)
        return jnp.concatenate([re, re], axis=3), jnp.concatenate([-im, im], axis=3)

    are, aim = (v[:, 0] for v in lanes(*apow([float(sub)])))
    sre, sim = lanes(*apow([float(sub * (1 << k)) for k in range(max(n_scan, 1))]))
    return wdc, kdc, vtc, are, aim, sre, sim


def _s5_core(xz, tables, d_skip, s_all, h0, *, sub, nsub, n_seq, row_block0, n_scan):
    has_h0 = h0 is not None
    rows = sub * nsub
    ns = tables[5].shape[1]
    hw2 = 2 * S5_HW
    sw = sub * 128

    def blk(shape, f):
        return pl.BlockSpec(shape, lambda gb, b: f(gb, b))
    in_specs = [blk((rows, 128), lambda gb, b: (row_block0 + b, gb)),
                blk((None, sw, 128), lambda gb, b: (gb, 0, 0)),
                blk((None, sub, 128, 128), lambda gb, b: (gb, 0, 0, 0)),
                blk((None, hw2, 128), lambda gb, b: (gb, 0, 0)),
                blk((None, 1, hw2), lambda gb, b: (gb, 0, 0)),
                blk((None, 1, hw2), lambda gb, b: (gb, 0, 0)),
                blk((None, ns, 1, hw2), lambda gb, b: (gb, 0, 0, 0)),
                blk((None, ns, 1, hw2), lambda gb, b: (gb, 0, 0, 0)),
                blk((1, 128), lambda gb, b: (0, gb))]
    args = [xz, *tables, d_skip.reshape(1, -1)]
    if has_h0:
        in_specs.append(blk((None, nsub, hw2), lambda gb, b: (gb, 0, 0)))
        args.append(h0)
        hl_shape = (S5_NBLK, nsub, hw2)
        hl_spec = blk((None, nsub, hw2), lambda gb, b: (gb, 0, 0))
    else:
        hl_shape = (S5_NBLK, n_seq, 1, hw2)
        hl_spec = blk((None, None, 1, hw2), lambda gb, b: (gb, b, 0, 0))
    aliases = {}
    if s_all is not None:
        in_specs.append(pl.BlockSpec(memory_space=pl.ANY))
        args.append(s_all)
        aliases = {len(args) - 1: 0}
    body = functools.partial(_s5_body, sub=sub, nsub=nsub, n_scan=n_scan, has_h0=has_h0,
                             has_alias=s_all is not None)
    return pl.pallas_call(
        body,
        out_shape=(jax.ShapeDtypeStruct((xz.shape[0], D_MODEL), F32), jax.ShapeDtypeStruct(hl_shape, F32)),
        grid=(S5_NBLK, n_seq),
        in_specs=in_specs,
        out_specs=(blk((rows, 128), lambda gb, b: (row_block0 + b, gb)), hl_spec),
        scratch_shapes=[pltpu.VMEM((sw, hw2 + sw), BF16), pltpu.VMEM((hw2, sw), BF16)],
        input_output_aliases=aliases,
        compiler_params=_cparams(2),
    )(*args)


def _s5_glu_body(s_ref, st_ref, z_ref, w_ref, b_ref, o_ref, gb_ref, *, tm, row_chunk):
    @pl.when(pl.program_id(1) == 0)
    def _():
        def body(i, carry):
            r0 = pl.multiple_of(i * row_chunk, 16)
            gb_ref[pl.ds(r0, row_chunk), :] = jax.nn.gelu(s_ref[pl.ds(r0, row_chunk), :]).astype(BF16)
            return carry
        lax.fori_loop(0, tm // row_chunk, body, 0)
    g = jax.nn.gelu(st_ref[...])
    o = g * jax.nn.sigmoid(_dot(gb_ref[...], w_ref[...].astype(BF16)) + b_ref[...])
    o_ref[...] = (o * jax.nn.silu(z_ref[...])).astype(o_ref.dtype)


def _s5_glu(s_all, xz, glu_w, glu_b):
    m, k = s_all.shape
    tn = 512
    tm = _largest_divisor(m, 1088, 16)
    row_chunk = _largest_divisor(tm, 272, 16)
    zb0 = k // tn
    body = functools.partial(_s5_glu_body, tm=tm, row_chunk=row_chunk)
    return pl.pallas_call(
        body,
        out_shape=jax.ShapeDtypeStruct((m, k), BF16),
        grid=(m // tm, k // tn),
        in_specs=[pl.BlockSpec((tm, k), lambda i, j: (i, 0)),
                  pl.BlockSpec((tm, tn), lambda i, j: (i, j)),
                  pl.BlockSpec((tm, tn), lambda i, j: (i, zb0 + j)),
                  pl.BlockSpec((None, k, tn), lambda i, j: (0, 0, j)),
                  pl.BlockSpec((1, tn), lambda i, j: (0, j))],
        out_specs=pl.BlockSpec((tm, tn), lambda i, j: (i, j)),
        scratch_shapes=[pltpu.VMEM((tm, k), BF16)],
        compiler_params=_cparams(2),
    )(s_all, s_all, xz, glu_w, glu_b.reshape(1, k))


def _last_conv_inputs(a, bp, tp, col0, n_cols):
    return jnp.stack([a[(b + 1) * tp - (CONV_K - 1):(b + 1) * tp, col0:col0 + n_cols] for b in range(bp)])


def _lru_layer(in_proj, out_proj, j, dims, state_conv, state_h, w_in, conv_w, conv_b, wa, ba, wx, bx, lam, w_out):
    bp, tp, bs = dims
    rows_p = bp * tp
    xz = in_proj(w_in, j)
    params = _lru_params(conv_w[j:j + 1], conv_b[j], wa[j:j + 1], wx[j:j + 1], ba[j], bx[j], lam[j])
    y, hl_p = _lru_prompt(xz, params, bp, tp)
    prevs = _conv_prev_rows(state_conv[j])
    h0_rows = jnp.repeat(state_h[j], SAMPLE_T, axis=0)
    y, h_rows = _lru_sample(xz, prevs, h0_rows, params, y, rows_p)
    x_s = xz[rows_p:, :LRU_WIDTH].reshape(bs, SAMPLE_T, LRU_WIDTH)
    outs = (_last_conv_inputs(xz, bp, tp, 0, LRU_WIDTH), x_s[:, SAMPLE_T - (CONV_K - 1):],
            hl_p.reshape(bp, LRU_WIDTH), h_rows.reshape(bs, SAMPLE_T, LRU_WIDTH)[:, SAMPLE_T - 1])
    return out_proj(y, w_out, j), outs


def _ret_layer(in_proj, out_proj, j, dims, state, w_in, gn, w_out):
    bp, tp, bs = dims
    rows_p = bp * tp
    qkvz = in_proj(w_in, j)
    y, s_p = _ret_prompt(qkvz, gn[j], bp, tp)
    y, s_s = _ret_sample(qkvz, gn[j], state[j:j + 1], y, rows_p, bs)
    return out_proj(y, w_out, j, tn=256), (s_p, s_s[0])


def _ssd_layer(in_proj, out_proj, j, dims, state_conv, state, w_in, conv_w, conv_b, dt_bias, a_log, d_skip,
               norm_g, w_out):
    bp, tp, bs = dims
    rows_p = bp * tp
    m = rows_p + bs * SAMPLE_T
    n_main = SSD_INNER + SSD_CONV_DIM
    w_t = jnp.swapaxes(w_in, 1, 2)
    zx = in_proj(w_t, j, n_cols=n_main, w_transposed=True)
    dt = in_proj(w_t[j:j + 1, n_main:, :], 0, w_transposed=True)
    dtg = dt.reshape(m, SSD_GROUPS, SSD_HPG).transpose(1, 0, 2)
    params = _ssd_param_arrays(conv_w[j:j + 1], conv_b[j], dt_bias[j], a_log[j], d_skip[j], norm_g[j])
    y, hl_p = _ssd_prompt(zx, dtg, params, bp, tp)
    st = state[j:j + 1].reshape(1, bs, SSD_GROUPS, SSD_GW, SSD_STATE)
    y, hl_s, *new_conv = _ssd_sample(zx, dtg, state_conv[j:j + 1], params, st, y, rows_p, bs)
    shape = (SSD_HEADS, SSD_HEADDIM, SSD_STATE)
    outs = (_last_conv_inputs(zx, bp, tp, SSD_INNER, SSD_CONV_DIM), jnp.concatenate(new_conv, axis=2),
            hl_p.reshape((bp,) + shape), hl_s.reshape((bs,) + shape))
    return out_proj(y, w_out, j, tn=256), outs


S5_SUB = 8


def _s5_layer(in_proj, out_proj, j, dims, h0_re, h0_im, w_in, lam_re, lam_im, b_re, b_im, c_re, c_im, d_skip,
              log_dt, glu_w, glu_b, w_out):
    bp, tp, bs = dims
    rows_p = bp * tp
    xz = in_proj(w_in, j)
    nsub_p = tp // S5_SUB
    n_scan = max(nsub_p - 1, 0).bit_length()
    par = (lam_re[j], lam_im[j], b_re[j], b_im[j], c_re[j], c_im[j], log_dt[j])
    s_all, hl_p = _s5_core(xz, _s5_tables(*par, S5_SUB, n_scan), d_skip[j], None, None,
                           sub=S5_SUB, nsub=nsub_p, n_seq=bp, row_block0=0, n_scan=n_scan)

    def to_lanes(v):
        return v.reshape(bs, S5_NBLK, S5_HW).transpose(1, 0, 2)
    h0 = jnp.concatenate([to_lanes(h0_re[j]), to_lanes(h0_im[j])], axis=2)
    s_all, hl_s = _s5_core(xz, _s5_tables(*par, SAMPLE_T, 0), d_skip[j], s_all, h0,
                           sub=SAMPLE_T, nsub=bs, n_seq=1, row_block0=rows_p // (bs * SAMPLE_T), n_scan=0)
    y = _s5_glu(s_all, xz, glu_w[j:j + 1], glu_b[j])

    def from_lanes(v, nb):
        return v.transpose(1, 0, 2).reshape(nb, S5_GROUPS, S5_STATE)
    hl_p = hl_p.reshape(S5_NBLK, bp, 2 * S5_HW)
    outs = (from_lanes(hl_p[..., :S5_HW], bp), from_lanes(hl_s[..., :S5_HW], bs),
            from_lanes(hl_p[..., S5_HW:], bp), from_lanes(hl_s[..., S5_HW:], bs))
    return out_proj(y, w_out, j), outs


def kernel(x_prompt, x_sample, state_lru_conv, state_lru_h, state_ret, state_ssd_conv, state_ssd, state_s5_re, state_s5_im, cache_mem_k, cache_mem_v, mem_prompt, mix_norm, xa_norm, xa_mem_norm, xa_wq, xa_wkv, xa_wo, final_norm, lru_w_in, lru_conv_w, lru_conv_b, lru_wa, lru_ba, lru_wx, lru_bx, lru_lambda, lru_w_out, ret_w_in, ret_gn, ret_w_out, ssd_w_in, ssd_conv_w, ssd_conv_b, ssd_dt_bias, ssd_a_log, ssd_d, ssd_norm, ssd_w_out, s5_w_in, s5_lambda_re, s5_lambda_im, s5_b_re, s5_b_im, s5_c_re, s5_c_im, s5_d, s5_log_dt, s5_glu_w, s5_glu_b, s5_w_out):
    bp, tp, d = x_prompt.shape
    bs, ts, _ = x_sample.shape
    depth = mix_norm.shape[0]
    assert d == D_MODEL and ts == SAMPLE_T and tp % LRU_TC == 0 and bs % SAMPLE_NB == 0
    rows_p, rows_s = bp * tp, bs * ts
    assert rows_p % rows_s == 0
    dims = (bp, tp, bs)
    h = jnp.concatenate([x_prompt.reshape(rows_p, d), x_sample.reshape(rows_s, d)], axis=0)
    mem = mem_prompt.reshape(bp * MEM_LEN, d)
    mix_g = mix_norm.reshape(depth, 1, d)
    xa_g = xa_norm.reshape(depth, 1, d)
    mem_g = xa_mem_norm.reshape(depth, 1, d)

    outs = {k: [] for k in ("lru", "ret", "ssd", "s5")}
    mem_k, mem_v = [], []
    xn = None
    for i in range(depth):
        kind, j = i % 4, i // 4

        def normed_proj(gain, w, idx, h=h, xn=xn, i=i, **kw):
            if xn is not None:
                return _matmul(xn, w, idx, **kw)
            return _matmul(h, w, idx, norm_g=gain, g_idx=i, **kw)

        def out_proj(y, w, idx, h=h, i=i, **kw):
            if w.shape[1] == D_MODEL:
                return _matmul_res_norm(y, w, idx, h, xa_g, i)
            return _matmul(y, w, idx, residual=h, **kw), None

        in_proj = functools.partial(normed_proj, mix_g)
        if kind == 0:
            (h, xn), o = _lru_layer(in_proj, out_proj, j, dims, state_lru_conv, state_lru_h, lru_w_in,
                                    lru_conv_w, lru_conv_b, lru_wa, lru_ba, lru_wx, lru_bx, lru_lambda, lru_w_out)
            outs["lru"].append(o)
        elif kind == 1:
            (h, xn), o = _ret_layer(in_proj, out_proj, j, dims, state_ret, ret_w_in, ret_gn, ret_w_out)
            outs["ret"].append(o)
        elif kind == 2:
            (h, xn), o = _ssd_layer(in_proj, out_proj, j, dims, state_ssd_conv, state_ssd, ssd_w_in,
                                    ssd_conv_w, ssd_conv_b, ssd_dt_bias, ssd_a_log, ssd_d, ssd_norm, ssd_w_out)
            outs["ssd"].append(o)
        else:
            (h, xn), o = _s5_layer(in_proj, out_proj, j, dims, state_s5_re, state_s5_im, s5_w_in,
                                   s5_lambda_re, s5_lambda_im, s5_b_re, s5_b_im, s5_c_re, s5_c_im, s5_d, s5_log_dt,
                                   s5_glu_w, s5_glu_b, s5_w_out)
            outs["s5"].append(o)
        kv = _matmul(mem, xa_wkv, i, norm_g=mem_g, g_idx=i)
        q = normed_proj(xa_g, xa_wq, i, h=h, xn=xn, out_dtype=BF16)
        o_att = _xattn_prompt(q, kv, bp, tp)
        o_att = _xattn_sample(q, cache_mem_k, cache_mem_v, i, o_att, rows_p, bs)
        if i + 1 < depth:
            h, xn = _matmul_res_norm(o_att, xa_wo, i, h, mix_g, i + 1)
        else:
            h, xn = _matmul(o_att, xa_wo, i, residual=h), None
        mem_k.append(kv[:, :d].reshape(bp, MEM_LEN, XA_HEADS, XA_HD))
        mem_v.append(kv[:, d:].reshape(bp, MEM_LEN, XA_HEADS, XA_HD))
    y_p = _rmsnorm(h, final_norm, 0, rows_p)
    y_s = _rmsnorm(h, final_norm, rows_p, rows_s)

    def stack(kind, idx):
        return jnp.stack([o[idx] for o in outs[kind]])
    return (y_p.reshape(bp, tp, d), y_s.reshape(bs, ts, d),
            stack("lru", 0), stack("lru", 1), stack("lru", 2), stack("lru", 3),
            stack("ret", 0), stack("ret", 1),
            stack("ssd", 0), stack("ssd", 1), stack("ssd", 2), stack("ssd", 3),
            stack("s5", 0), stack("s5", 1), stack("s5", 2), stack("s5", 3),
            jnp.stack(mem_k), jnp.stack(mem_v))
```

```python
import functools
import math

import jax
import jax.numpy as jnp
from jax import lax
from jax.experimental import pallas as pl
from jax.experimental.pallas import tpu as pltpu

F32 = jnp.float32
BF16 = jnp.bfloat16

D_MODEL = 2048
PAST_LEN = 16384
EPS = 1e-6
CONV_K = 4
CHUNK = 128
LRU_WIDTH = D_MODEL
LRU_BLOCK = 256
LRU_C = 8.0
RET_HEADS = 8
RET_DK = 256
RET_DV = 512
RET_VW = RET_HEADS * RET_DV
ROPE_BASE = 10000.0
SSD_INNER = 2 * D_MODEL
SSD_HEADDIM = 64
SSD_HEADS = 64
SSD_GROUPS = 8
SSD_HPG = 8
SSD_STATE = 128
SSD_GW = SSD_HPG * SSD_HEADDIM
SSD_CONV_DIM = SSD_INNER + 2 * SSD_GROUPS * SSD_STATE
S5_GROUP = 16
S5_GROUPS = 128
S5_STATE = 64
S5_G8 = 8
S5_NBLK = S5_GROUPS // S5_G8
S5_HW = S5_G8 * S5_STATE
MEM_LEN = 256
XA_HEADS = 4
XA_HD = 512
SAMPLE_T = 4
SAMPLE_NB = 4

VMEM_LIMIT_BYTES = 56 * 1024 * 1024
MM_MAX_ROWS = 2176
MM_X_DOUBLE_BUFFER_BYTES = 18 * 1024 * 1024
NT_DIMS = (((1,), (1,)), ((), ()))
TN_DIMS = (((0,), (0,)), ((), ()))


def _cparams(n_axes):
    return pltpu.CompilerParams(dimension_semantics=("arbitrary",) * n_axes,
                                vmem_limit_bytes=VMEM_LIMIT_BYTES)


def _dot(a, b):
    return jnp.dot(a, b, preferred_element_type=F32)


def _dot_nt(a, b):
    return lax.dot_general(a, b, NT_DIMS, preferred_element_type=F32)


def _dot_tn(a, b):
    return lax.dot_general(a, b, TN_DIMS, preferred_element_type=F32)


def _largest_divisor(n, cap, mult):
    for d in range(min(cap, n) // mult * mult, 0, -mult):
        if n % d == 0:
            return d
    raise ValueError(f"no divisor of {n} that is a multiple of {mult}")


def _expm1(x):
    return jnp.where(jnp.abs(x) < 0.5, jnp.tanh(0.5 * x) * (jnp.exp(x) + 1.0), jnp.exp(x) - 1.0)


def _expand_cols(v, width):
    rows, n = v.shape
    return jnp.concatenate([jnp.broadcast_to(v[:, r:r + 1], (rows, width)) for r in range(n)], axis=1)


def _mm_body(*refs, has_norm, has_res, stage_x, tm, row_chunk, w_transposed):
    it = iter(refs)
    x_ref, w_ref = next(it), next(it)
    g_ref = next(it) if has_norm else None
    r_ref = next(it) if has_res else None
    o_ref = next(it)
    if stage_x:
        xb_ref = next(it)

        @pl.when(pl.program_id(1) == 0)
        def _():
            def body(i, carry):
                r0 = pl.multiple_of(i * row_chunk, 16)
                x = x_ref[pl.ds(r0, row_chunk), :].astype(F32)
                if has_norm:
                    x = x * lax.rsqrt(jnp.mean(x * x, axis=-1, keepdims=True) + EPS) * g_ref[...]
                xb_ref[pl.ds(r0, row_chunk), :] = x.astype(BF16)
                return carry
            lax.fori_loop(0, tm // row_chunk, body, 0)
        xb = xb_ref[...]
    else:
        xb = x_ref[...]
    wb = w_ref[...].astype(BF16)
    acc = _dot_nt(xb, wb) if w_transposed else _dot(xb, wb)
    if has_res:
        acc = acc + r_ref[...]
    o_ref[...] = acc.astype(o_ref.dtype)


def _matmul(x, w, w_idx, *, norm_g=None, g_idx=0, residual=None, out_dtype=F32, tn=512, col_off=0, n_cols=None,
            w_transposed=False):
    m, k = x.shape
    n_total = w.shape[1] if w_transposed else w.shape[2]
    n_cols = n_total if n_cols is None else n_cols
    tn = min(tn, n_cols)
    assert n_cols % tn == 0 and col_off % tn == 0
    tm = _largest_divisor(m, MM_MAX_ROWS, 16)
    stage_x = x.dtype != BF16 or norm_g is not None
    row_chunk = _largest_divisor(tm, 272, 16)
    cb = col_off // tn
    x_bytes = tm * k * x.dtype.itemsize
    x_mode = {} if 2 * x_bytes <= MM_X_DOUBLE_BUFFER_BYTES else {"pipeline_mode": pl.Buffered(1)}
    w_spec = (pl.BlockSpec((None, tn, k), lambda i, j: (w_idx, j + cb, 0)) if w_transposed
              else pl.BlockSpec((None, k, tn), lambda i, j: (w_idx, 0, j + cb)))
    in_specs = [pl.BlockSpec((tm, k), lambda i, j: (i, 0), **x_mode), w_spec]
    args = [x, w]
    if norm_g is not None:
        in_specs.append(pl.BlockSpec((None, 1, k), lambda i, j: (g_idx, 0, 0)))
        args.append(norm_g)
    if residual is not None:
        in_specs.append(pl.BlockSpec((tm, tn), lambda i, j: (i, j)))
        args.append(residual)
    body = functools.partial(_mm_body, has_norm=norm_g is not None, has_res=residual is not None,
                             stage_x=stage_x, tm=tm, row_chunk=row_chunk, w_transposed=w_transposed)
    return pl.pallas_call(
        body,
        out_shape=jax.ShapeDtypeStruct((m, n_cols), out_dtype),
        grid=(m // tm, n_cols // tn),
        in_specs=in_specs,
        out_specs=pl.BlockSpec((tm, tn), lambda i, j: (i, j)),
        scratch_shapes=[pltpu.VMEM((tm, k), BF16)] if stage_x else [],
        compiler_params=_cparams(2),
    )(*args)


def _mm_res_norm_body(x_ref, w_ref, r_ref, g_ref, h_ref, xn_ref, *, tm, tn, n_tiles, row_chunk):
    j = pl.program_id(1)
    acc = _dot(x_ref[...], w_ref[...].astype(BF16)) + r_ref[...]
    for jj in range(n_tiles):
        @pl.when(j == jj)
        def _():
            h_ref[:, jj * tn:(jj + 1) * tn] = acc

    @pl.when(j == n_tiles - 1)
    def _():
        def body(i, carry):
            r0 = pl.multiple_of(i * row_chunk, 16)
            h = h_ref[pl.ds(r0, row_chunk), :]
            hn = h * lax.rsqrt(jnp.mean(h * h, axis=-1, keepdims=True) + EPS) * g_ref[...]
            xn_ref[pl.ds(r0, row_chunk), :] = hn.astype(BF16)
            return carry
        lax.fori_loop(0, tm // row_chunk, body, 0)


MM_NORM_OUT_ROWS = 1088


def _matmul_res_norm(x, w, w_idx, residual, next_g, g_idx, *, tn=512):
    m, k = x.shape
    n = w.shape[2]
    tm = _largest_divisor(m, MM_NORM_OUT_ROWS, 16)
    row_chunk = _largest_divisor(tm, 272, 16)
    n_tiles = n // tn
    body = functools.partial(_mm_res_norm_body, tm=tm, tn=tn, n_tiles=n_tiles, row_chunk=row_chunk)
    return pl.pallas_call(
        body,
        out_shape=(jax.ShapeDtypeStruct((m, n), F32), jax.ShapeDtypeStruct((m, n), BF16)),
        grid=(m // tm, n_tiles),
        in_specs=[pl.BlockSpec((tm, k), lambda i, j: (i, 0)),
                  pl.BlockSpec((None, k, tn), lambda i, j: (w_idx, 0, j)),
                  pl.BlockSpec((tm, tn), lambda i, j: (i, j)),
                  pl.BlockSpec((None, 1, n), lambda i, j: (g_idx, 0, 0))],
        out_specs=(pl.BlockSpec((tm, n), lambda i, j: (i, 0)), pl.BlockSpec((tm, n), lambda i, j: (i, 0))),
        compiler_params=_cparams(2),
    )(x, w, residual, next_g)


def _rmsnorm_body(x_ref, g_ref, o_ref):
    x = x_ref[...]
    o_ref[...] = x * lax.rsqrt(jnp.mean(x * x, axis=-1, keepdims=True) + EPS) * g_ref[...]


def _rmsnorm(x, g, row0, n_rows):
    k = x.shape[1]
    tm = _largest_divisor(math.gcd(n_rows, row0) if row0 else n_rows, 512, 8)
    rb0 = row0 // tm
    return pl.pallas_call(
        _rmsnorm_body,
        out_shape=jax.ShapeDtypeStruct((n_rows, k), F32),
        grid=(n_rows // tm,),
        in_specs=[pl.BlockSpec((tm, k), lambda i: (rb0 + i, 0)), pl.BlockSpec((1, k), lambda i: (0, 0))],
        out_specs=pl.BlockSpec((tm, k), lambda i: (i, 0)),
        compiler_params=_cparams(1),
    )(x, g.reshape(1, k))


def _softmax_rows(s):
    e = jnp.exp(s - jnp.max(s, axis=-1, keepdims=True))
    return e / jnp.sum(e, axis=-1, keepdims=True)


def _xattn_prompt_body(q_ref, k_ref, v_ref, o_ref):
    kb = k_ref[...].astype(BF16)
    vb = v_ref[...].astype(BF16)
    for h in range(XA_HEADS):
        sl = slice(h * XA_HD, (h + 1) * XA_HD)
        p = _softmax_rows(_dot_nt(q_ref[:, sl], kb[:, sl]) * XA_HD ** -0.5)
        o_ref[:, sl] = _dot(p.astype(BF16), vb[:, sl]).astype(o_ref.dtype)


def _xattn_prompt(q, kv, bp, tp):
    m = q.shape[0]
    tq = min(tp, 1024)
    nq = tp // tq
    return pl.pallas_call(
        _xattn_prompt_body,
        out_shape=jax.ShapeDtypeStruct((m, D_MODEL), BF16),
        grid=(bp, nq),
        in_specs=[pl.BlockSpec((tq, D_MODEL), lambda b, i: (b * nq + i, 0)),
                  pl.BlockSpec((MEM_LEN, D_MODEL), lambda b, i: (b, 0)),
                  pl.BlockSpec((MEM_LEN, D_MODEL), lambda b, i: (b, 1))],
        out_specs=pl.BlockSpec((tq, D_MODEL), lambda b, i: (b * nq + i, 0)),
        compiler_params=_cparams(2),
    )(q, kv, kv)


def _log2(n):
    assert n & (n - 1) == 0, n
    return n.bit_length() - 1


def _row_segment(rows, seg_len, width):
    return lax.broadcasted_iota(jnp.int32, (rows, width), 0) >> _log2(seg_len)


def _row_pos(rows, seg_len, width):
    return lax.broadcasted_iota(jnp.int32, (rows, width), 0) & (seg_len - 1)


XA_SNB = 4


def _xattn_sample_body(q_ref, k_ref, v_ref, oin_ref, o_ref, acc_ref):
    del oin_ref
    part = pl.program_id(1)
    rows = SAMPLE_NB * SAMPLE_T

    @pl.when(part == 0)
    def _():
        acc_ref[...] = jnp.zeros_like(acc_ref)

    lane_blocks = XA_HD // 128

    def head(ref, i, h):
        return jnp.concatenate([ref[i, pl.ds(j * XA_HEADS + h, MEM_LEN, stride=lane_blocks * XA_HEADS), :]
                                for j in range(lane_blocks)], axis=1).astype(BF16)

    pairs = [(i, h) for i in range(XA_SNB) for h in range(XA_HEADS)]
    s = jnp.concatenate([_dot_nt(q_ref[:, h * XA_HD:(h + 1) * XA_HD], head(k_ref, i, h)) for i, h in pairs], axis=0)
    p = _softmax_rows(s * XA_HD ** -0.5).astype(BF16)
    seg = _row_segment(rows, SAMPLE_T, XA_HD)
    for h in range(XA_HEADS):
        sl = slice(h * XA_HD, (h + 1) * XA_HD)
        o = acc_ref[:, sl]
        for i in range(XA_SNB):
            n = i * XA_HEADS + h
            o = jnp.where(seg == part * XA_SNB + i, _dot(p[n * rows:(n + 1) * rows], head(v_ref, i, h)), o)
        acc_ref[:, sl] = o

    @pl.when(part == pl.num_programs(1) - 1)
    def _():
        o_ref[...] = acc_ref[...].astype(o_ref.dtype)


def _xattn_sample(q, cache_k, cache_v, layer, o_all, rows_p, bs):
    rows = SAMPLE_NB * SAMPLE_T
    rb0 = rows_p // rows
    nparts = SAMPLE_NB // XA_SNB
    depth = cache_k.shape[0]
    lane_blocks = XA_HD // 128
    kv_rows = MEM_LEN * lane_blocks * XA_HEADS

    def relayout(c):
        c = c.reshape(depth, bs, MEM_LEN, XA_HEADS, lane_blocks, 128).transpose(0, 1, 2, 4, 3, 5)
        return c.reshape(depth, bs, kv_rows, 128)
    cache_k, cache_v = relayout(cache_k), relayout(cache_v)
    kv_spec = pl.BlockSpec((None, XA_SNB, kv_rows, 128), lambda bb, s: (layer, bb * nparts + s, 0, 0))
    return pl.pallas_call(
        _xattn_sample_body,
        out_shape=jax.ShapeDtypeStruct(o_all.shape, o_all.dtype),
        grid=(bs // SAMPLE_NB, nparts),
        in_specs=[pl.BlockSpec((rows, D_MODEL), lambda bb, s: (rb0 + bb, 0)), kv_spec, kv_spec,
                  pl.BlockSpec(memory_space=pl.ANY)],
        out_specs=pl.BlockSpec((rows, D_MODEL), lambda bb, s: (rb0 + bb, 0)),
        scratch_shapes=[pltpu.VMEM((rows, D_MODEL), F32)],
        input_output_aliases={3: 0},
        compiler_params=_cparams(2),
    )(q, cache_k, cache_v, o_all)


LRU_CW = 512
LRU_TC = 256


def _lru_gate_scan(xc, wa_ref, wx_ref, ba, bx, lam, h0, seg_len):
    rows = xc.shape[0]
    rs, gis = [], []
    for n in range(LRU_CW // LRU_BLOCK):
        xb = xc[:, n * LRU_BLOCK:(n + 1) * LRU_BLOCK].astype(BF16)
        rs.append(_dot(xb, wa_ref[n].astype(BF16)))
        gis.append(_dot(xb, wx_ref[n].astype(BF16)))
    r = jax.nn.sigmoid(jnp.concatenate(rs, axis=1) + ba)
    gi = jax.nn.sigmoid(jnp.concatenate(gis, axis=1) + bx)
    log_a = -LRU_C * r * jax.nn.softplus(-lam)
    a = jnp.exp(log_a)
    b = jnp.sqrt(-_expm1(2.0 * log_a)) * (gi * xc)
    t = _row_pos(rows, seg_len, LRU_CW)
    shift = 1
    while shift < seg_len:
        keep = t >= shift
        a_prev = pltpu.roll(a, shift, 0)
        b_prev = pltpu.roll(b, shift, 0)
        b = jnp.where(keep, a * b_prev + b, b)
        a = jnp.where(keep, a * a_prev, a)
        shift *= 2
    return b + a * h0


def _lru_prompt_body(x_ref, z_ref, cw_ref, cb_ref, wa_ref, wx_ref, ba_ref, bx_ref, lam_ref,
                     y_ref, hl_ref, xext_ref, hc_ref):
    c = pl.program_id(2)

    @pl.when(c == 0)
    def _():
        xext_ref[0:8, :] = jnp.zeros((8, LRU_CW), F32)
        hc_ref[...] = jnp.zeros_like(hc_ref)

    x = x_ref[...]
    xext_ref[8:8 + LRU_TC, :] = x
    xc = cw_ref[3:4, :] * x + cb_ref[...]
    for k in range(CONV_K - 1):
        xc = xc + cw_ref[k:k + 1, :] * xext_ref[5 + k:5 + k + LRU_TC, :]
    xext_ref[0:8, :] = x[LRU_TC - 8:LRU_TC, :]
    h = _lru_gate_scan(xc, wa_ref, wx_ref, ba_ref[...], bx_ref[...], lam_ref[...], hc_ref[0:1, :], LRU_TC)
    hc_ref[0:1, :] = h[LRU_TC - 1:LRU_TC, :]
    y_ref[...] = (h * jax.nn.silu(z_ref[...])).astype(y_ref.dtype)

    @pl.when(c == pl.num_programs(2) - 1)
    def _():
        hl_ref[...] = h[LRU_TC - 1:LRU_TC, :]


def _lru_param_specs(n_grid):
    def cmap(block):
        if n_grid == 3:
            return lambda b, cb, c: block(cb)
        return lambda cb: block(cb)
    return [pl.BlockSpec((None, CONV_K, LRU_CW), cmap(lambda cb: (0, 0, cb))),
            pl.BlockSpec((None, 1, LRU_CW), cmap(lambda cb: (0, 0, cb))),
            pl.BlockSpec((None, LRU_CW // LRU_BLOCK, LRU_BLOCK, LRU_BLOCK), cmap(lambda cb: (0, cb, 0, 0))),
            pl.BlockSpec((None, LRU_CW // LRU_BLOCK, LRU_BLOCK, LRU_BLOCK), cmap(lambda cb: (0, cb, 0, 0))),
            pl.BlockSpec((None, 1, LRU_CW), cmap(lambda cb: (0, 0, cb))),
            pl.BlockSpec((None, 1, LRU_CW), cmap(lambda cb: (0, 0, cb))),
            pl.BlockSpec((None, 1, LRU_CW), cmap(lambda cb: (0, 0, cb)))]


def _lru_params(conv_w, conv_b, wa, wx, ba, bx, lam):
    w = LRU_WIDTH
    return [conv_w, conv_b.reshape(1, 1, w), wa, wx, ba.reshape(1, 1, w), bx.reshape(1, 1, w), lam.reshape(1, 1, w)]


def _lru_prompt(xz, params, bp, tp):
    m = xz.shape[0]
    nc = tp // LRU_TC
    ncb = LRU_WIDTH // LRU_CW
    return pl.pallas_call(
        _lru_prompt_body,
        out_shape=(jax.ShapeDtypeStruct((m, LRU_WIDTH), BF16), jax.ShapeDtypeStruct((bp, 1, LRU_WIDTH), F32)),
        grid=(bp, ncb, nc),
        in_specs=[pl.BlockSpec((LRU_TC, LRU_CW), lambda b, cb, c: (b * nc + c, cb)),
                  pl.BlockSpec((LRU_TC, LRU_CW), lambda b, cb, c: (b * nc + c, ncb + cb))] + _lru_param_specs(3),
        out_specs=(pl.BlockSpec((LRU_TC, LRU_CW), lambda b, cb, c: (b * nc + c, cb)),
                   pl.BlockSpec((None, 1, LRU_CW), lambda b, cb, c: (b, 0, cb))),
        scratch_shapes=[pltpu.VMEM((LRU_TC + 8, LRU_CW), F32), pltpu.VMEM((8, LRU_CW), F32)],
        compiler_params=_cparams(3),
    )(xz, xz, *params)


def _lru_sample_body(x_ref, z_ref, p1_ref, p2_ref, p3_ref, h0_ref, cw_ref, cb_ref, wa_ref, wx_ref,
                     ba_ref, bx_ref, lam_ref, yin_ref, y_ref, h_ref):
    del yin_ref
    x = x_ref[...]
    rows = x.shape[0]
    t = _row_pos(rows, SAMPLE_T, LRU_CW)
    xc = cw_ref[3:4, :] * x + cb_ref[...]
    for k, prev_ref in ((1, p1_ref), (2, p2_ref), (3, p3_ref)):
        xc = xc + cw_ref[3 - k:4 - k, :] * jnp.where(t >= k, pltpu.roll(x, k, 0), prev_ref[...])
    h = _lru_gate_scan(xc, wa_ref, wx_ref, ba_ref[...], bx_ref[...], lam_ref[...], h0_ref[...], SAMPLE_T)
    h_ref[...] = h
    y_ref[...] = (h * jax.nn.silu(z_ref[...])).astype(y_ref.dtype)


def _lru_sample(xz, prevs, h0_rows, params, y_all, rows_p):
    rows_s = prevs[0].shape[0]
    rb0 = rows_p // rows_s
    ncb = LRU_WIDTH // LRU_CW
    small = pl.BlockSpec((rows_s, LRU_CW), lambda cb: (0, cb))
    return pl.pallas_call(
        _lru_sample_body,
        out_shape=(jax.ShapeDtypeStruct(y_all.shape, y_all.dtype), jax.ShapeDtypeStruct((rows_s, LRU_WIDTH), F32)),
        grid=(ncb,),
        in_specs=[pl.BlockSpec((rows_s, LRU_CW), lambda cb: (rb0, cb)),
                  pl.BlockSpec((rows_s, LRU_CW), lambda cb: (rb0, ncb + cb)),
                  small, small, small, small] + _lru_param_specs(1) + [pl.BlockSpec(memory_space=pl.ANY)],
        out_specs=(pl.BlockSpec((rows_s, LRU_CW), lambda cb: (rb0, cb)), small),
        input_output_aliases={13: 0},
        compiler_params=_cparams(1),
    )(xz, xz, *prevs, h0_rows, *params, y_all)


def _conv_prev_rows(buf):
    b, _, c = buf.shape
    out = []
    for k in range(1, CONV_K):
        pad = jnp.zeros((b, SAMPLE_T - k, c), buf.dtype)
        out.append(jnp.concatenate([buf[:, CONV_K - 1 - k:], pad], axis=1).reshape(b * SAMPLE_T, c))
    return out


def _rope(x, cos, sin):
    half = RET_DK // 2
    x1, x2 = x[:, :half], x[:, half:]
    return jnp.concatenate([x1 * cos - x2 * sin, x1 * sin + x2 * cos], axis=1)


def _ret_chunk(q, k, v, z, cos, sin, dmask, qdec, kdec, cdec, gn, states, seg_len):
    rows = q.shape[0]
    qb = _rope(q, cos, sin).astype(BF16)
    kr = _rope(k, cos, sin) * RET_DK ** -0.5
    kb = kr.astype(BF16)
    vb = v.astype(BF16)
    kd = kr * kdec
    o = _dot((_dot_nt(qb, kb) * dmask).astype(BF16), vb)
    new_states = []
    single = len(states) == 1
    seg_v = None if single else _row_segment(rows, seg_len, RET_DV)
    seg_k = None if single else _row_segment(rows, seg_len, RET_DK)
    for i, s in enumerate(states):
        cross = _dot(qb, s.astype(BF16)) * qdec
        kdi = kd
        if not single:
            cross = jnp.where(seg_v == i, cross, 0.0)
            kdi = jnp.where(seg_k == i, kd, 0.0)
        o = o + cross
        new_states.append(s * cdec + _dot_tn(kdi.astype(BF16), vb))
    mu = jnp.mean(o, axis=-1, keepdims=True)
    var = jnp.mean(jnp.square(o - mu), axis=-1, keepdims=True)
    on = (o - mu) * lax.rsqrt(var + EPS) * gn
    return on * jax.nn.silu(z), new_states


def _ret_prompt_body(q_ref, k_ref, v_ref, z_ref, cos_ref, sin_ref, dm_ref, qd_ref, kd_ref, cd_ref, gn_ref,
                     y_ref, sl_ref, s_ref):
    c = pl.program_id(1)

    @pl.when(c == 0)
    def _():
        s_ref[...] = jnp.zeros_like(s_ref)

    cos, sin = cos_ref[...], sin_ref[...]
    for h in range(RET_HEADS):
        ksl = slice(h * RET_DK, (h + 1) * RET_DK)
        vsl = slice(h * RET_DV, (h + 1) * RET_DV)
        y, (s_new,) = _ret_chunk(q_ref[:, ksl], k_ref[:, ksl], v_ref[:, vsl], z_ref[:, vsl], cos, sin,
                                 dm_ref[h], qd_ref[h], kd_ref[h], cd_ref[h], gn_ref[:, vsl], [s_ref[h]], CHUNK)
        s_ref[h] = s_new
        y_ref[:, vsl] = y.astype(y_ref.dtype)

        @pl.when(c == pl.num_programs(1) - 1)
        def _():
            sl_ref[h] = s_new


def _ret_tables(seg_len, nseg, pos):
    rows = seg_len * nseg
    log_g = jnp.log1p(-jnp.exp2(-5.0 - jnp.arange(RET_HEADS, dtype=F32)))[:, None, None]
    t = (jnp.arange(rows) % seg_len).astype(F32)
    seg = jnp.arange(rows) // seg_len
    rel = t[:, None] - t[None, :]
    ok = (rel >= 0) & (seg[:, None] == seg[None, :])
    dmask = jnp.where(ok, jnp.exp(log_g * jnp.where(ok, rel, 0.0)), 0.0)
    qdec = jnp.broadcast_to(jnp.exp(log_g * (t + 1.0)[None, :, None]), (RET_HEADS, rows, RET_DV))
    kdec = jnp.broadcast_to(jnp.exp(log_g * (seg_len - 1.0 - t)[None, :, None]), (RET_HEADS, rows, RET_DK))
    cdec = jnp.broadcast_to(jnp.exp(log_g * seg_len), (RET_HEADS, 1, RET_DV))
    half = RET_DK // 2
    inv = ROPE_BASE ** (-jnp.arange(half, dtype=F32) / half)
    ang = pos.astype(F32)[:, None] * inv
    return jnp.cos(ang), jnp.sin(ang), dmask, qdec, kdec, cdec


def _ret_prompt(qkvz, gn, bp, tp):
    m = qkvz.shape[0]
    nc = tp // CHUNK
    cos, sin, dmask, qdec, kdec, cdec = _ret_tables(CHUNK, 1, jnp.arange(tp))
    qk_w = RET_HEADS * RET_DK
    full = lambda b, c: (0, 0, 0)
    return pl.pallas_call(
        _ret_prompt_body,
        out_shape=(jax.ShapeDtypeStruct((m, RET_VW), BF16),
                   jax.ShapeDtypeStruct((bp, RET_HEADS, RET_DK, RET_DV), F32)),
        grid=(bp, nc),
        in_specs=[pl.BlockSpec((CHUNK, qk_w), lambda b, c: (b * nc + c, 0)),
                  pl.BlockSpec((CHUNK, qk_w), lambda b, c: (b * nc + c, 1)),
                  pl.BlockSpec((CHUNK, RET_VW), lambda b, c: (b * nc + c, 1)),
                  pl.BlockSpec((CHUNK, RET_VW), lambda b, c: (b * nc + c, 2)),
                  pl.BlockSpec((CHUNK, RET_DK // 2), lambda b, c: (c, 0)),
                  pl.BlockSpec((CHUNK, RET_DK // 2), lambda b, c: (c, 0)),
                  pl.BlockSpec((RET_HEADS, CHUNK, CHUNK), full),
                  pl.BlockSpec((RET_HEADS, CHUNK, RET_DV), full),
                  pl.BlockSpec((RET_HEADS, CHUNK, RET_DK), full),
                  pl.BlockSpec((RET_HEADS, 1, RET_DV), full),
                  pl.BlockSpec((1, RET_VW), lambda b, c: (0, 0))],
        out_specs=(pl.BlockSpec((CHUNK, RET_VW), lambda b, c: (b * nc + c, 0)),
                   pl.BlockSpec((None, RET_HEADS, RET_DK, RET_DV), lambda b, c: (b, 0, 0, 0))),
        scratch_shapes=[pltpu.VMEM((RET_HEADS, RET_DK, RET_DV), F32)],
        compiler_params=_cparams(2),
    )(qkvz, qkvz, qkvz, qkvz, cos, sin, dmask, qdec, kdec, cdec, gn.reshape(1, RET_VW))


def _ret_sample_body(q_ref, k_ref, v_ref, z_ref, cos_ref, sin_ref, dm_ref, qd_ref, kd_ref, cd_ref, gn_ref,
                     s0_ref, yin_ref, y_ref, s_ref):
    del yin_ref
    cos, sin = cos_ref[...], sin_ref[...]
    for h in range(RET_SAMPLE_HEADS):
        ksl = slice(h * RET_DK, (h + 1) * RET_DK)
        vsl = slice(h * RET_DV, (h + 1) * RET_DV)
        y, s_new = _ret_chunk(q_ref[:, ksl], k_ref[:, ksl], v_ref[:, vsl], z_ref[:, vsl], cos, sin,
                              dm_ref[h], qd_ref[h], kd_ref[h], cd_ref[h], gn_ref[:, vsl],
                              [s0_ref[i, h] for i in range(SAMPLE_NB)], SAMPLE_T)
        for i in range(SAMPLE_NB):
            s_ref[i, h] = s_new[i]
        y_ref[:, vsl] = y.astype(y_ref.dtype)


RET_SAMPLE_HEADS = 4


def _ret_sample(qkvz, gn, state, y_all, rows_p, bs):
    rows = SAMPLE_NB * SAMPLE_T
    rb0 = rows_p // rows
    pos = PAST_LEN + jnp.arange(rows) % SAMPLE_T
    cos, sin, dmask, qdec, kdec, cdec = _ret_tables(SAMPLE_T, SAMPLE_NB, pos)
    nh = RET_SAMPLE_HEADS
    kw, vw = nh * RET_DK, nh * RET_DV
    k_blk = RET_HEADS * RET_DK // kw
    v_blk = 2 * RET_HEADS * RET_DK // vw
    z_blk = v_blk + RET_HEADS // nh
    st_spec = pl.BlockSpec((None, SAMPLE_NB, nh, RET_DK, RET_DV), lambda bb, p: (0, bb, p, 0, 0))
    return pl.pallas_call(
        _ret_sample_body,
        out_shape=(jax.ShapeDtypeStruct(y_all.shape, y_all.dtype), jax.ShapeDtypeStruct(state.shape, F32)),
        grid=(bs // SAMPLE_NB, RET_HEADS // nh),
        in_specs=[pl.BlockSpec((rows, kw), lambda bb, p: (rb0 + bb, p)),
                  pl.BlockSpec((rows, kw), lambda bb, p: (rb0 + bb, k_blk + p)),
                  pl.BlockSpec((rows, vw), lambda bb, p: (rb0 + bb, v_blk + p)),
                  pl.BlockSpec((rows, vw), lambda bb, p: (rb0 + bb, z_blk + p)),
                  pl.BlockSpec((rows, RET_DK // 2), lambda bb, p: (0, 0)),
                  pl.BlockSpec((rows, RET_DK // 2), lambda bb, p: (0, 0)),
                  pl.BlockSpec((nh, rows, rows), lambda bb, p: (p, 0, 0)),
                  pl.BlockSpec((nh, rows, RET_DV), lambda bb, p: (p, 0, 0)),
                  pl.BlockSpec((nh, rows, RET_DK), lambda bb, p: (p, 0, 0)),
                  pl.BlockSpec((nh, 1, RET_DV), lambda bb, p: (p, 0, 0)),
                  pl.BlockSpec((1, vw), lambda bb, p: (0, p)),
                  st_spec,
                  pl.BlockSpec(memory_space=pl.ANY)],
        out_specs=(pl.BlockSpec((rows, vw), lambda bb, p: (rb0 + bb, p)), st_spec),
        input_output_aliases={12: 0},
        compiler_params=_cparams(2),
    )(qkvz, qkvz, qkvz, qkvz, cos, sin, dmask, qdec, kdec, cdec, gn.reshape(1, RET_VW), state, y_all)


def _head_expand_matrix():
    r = lax.broadcasted_iota(jnp.int32, (128, SSD_GW), 0)
    c = lax.broadcasted_iota(jnp.int32, (128, SSD_GW), 1)
    return jnp.where(r == (c >> _log2(SSD_HEADDIM)), 1.0, 0.0).astype(BF16)


def _expand_heads(v, expand_mat):
    v1 = v.astype(BF16)
    r1 = v - v1.astype(F32)
    v2 = r1.astype(BF16)
    v3 = (r1 - v2.astype(F32)).astype(BF16)
    return (_dot(v1, expand_mat) + _dot(v2, expand_mat)) + _dot(v3, expand_mat)


def _ssd_chunk(xs, bs, cs, z, dt_raw, dt_bias, a_log, d_skip, norm_g, states, seg_len, expand_mat):
    rows = xs.shape[0]
    single = len(states) == 1
    hp = SSD_HPG
    pad = 128
    t8 = _row_pos(rows, seg_len, pad)
    seg8 = _row_segment(rows, seg_len, pad)
    tt = lax.broadcasted_iota(jnp.int32, (rows, rows), 0)
    ss = lax.broadcasted_iota(jnp.int32, (rows, rows), 1)
    causal = (tt >= ss) & ((tt >> _log2(seg_len)) == (ss >> _log2(seg_len)))

    def lane_pad(v):
        return jnp.concatenate([v, jnp.zeros((v.shape[0], pad - hp), F32)], axis=1)

    dt = jax.nn.softplus(lane_pad(dt_raw) + lane_pad(dt_bias))
    da = dt * (-jnp.exp(lane_pad(a_log)))
    cum = da
    shift = 1
    while shift < seg_len:
        cum = cum + jnp.where(t8 >= shift, pltpu.roll(cum, shift, 0), 0.0)
        shift *= 2
    cum_sq = cum if rows == pad else jnp.concatenate([cum, jnp.zeros((pad - rows, pad), F32)], axis=0)
    cum_t = cum_sq.T[0:hp, 0:rows]
    lasts = [cum[(i + 1) * seg_len - 1:(i + 1) * seg_len, :] for i in range(len(states))]
    last_row = lasts[0]
    if not single:
        last_row = jnp.zeros((rows, pad), F32)
        for i, l in enumerate(lasts):
            last_row = jnp.where(seg8 == i, l, last_row)
    to_end = jnp.exp(last_row - cum)
    expanded = _expand_heads(jnp.concatenate([dt, jnp.exp(cum), to_end], axis=0), expand_mat)
    dt_x, ecum, to_end_x = expanded[0:rows], expanded[rows:2 * rows], expanded[2 * rows:3 * rows]

    csb = cs.astype(BF16)
    bsb = bs.astype(BF16)
    cb = _dot_nt(csb, bsb)
    xdt = xs * dt_x
    ycols = []
    for r in range(hp):
        lmat = jnp.where(causal, jnp.exp(cum[:, r:r + 1] - cum_t[r:r + 1, :]), 0.0)
        ycols.append(_dot((cb * lmat).astype(BF16), xdt[:, r * SSD_HEADDIM:(r + 1) * SSD_HEADDIM].astype(BF16)))
    y = jnp.concatenate(ycols, axis=1)
    xte = xdt * to_end_x
    segw = None if single else _row_segment(rows, seg_len, SSD_GW)
    new_states = []
    for i, h in enumerate(states):
        y_off = _dot_nt(csb, h.astype(BF16)) * ecum
        xi = xte
        if not single:
            y_off = jnp.where(segw == i, y_off, 0.0)
            xi = jnp.where(segw == i, xte, 0.0)
        y = y + y_off
        e_last = jnp.exp(lasts[i])
        dec = jnp.concatenate([jnp.broadcast_to(e_last[:, r:r + 1], (SSD_HEADDIM, SSD_STATE)) for r in range(hp)],
                              axis=0)
        new_states.append(h * dec + _dot_tn(xi.astype(BF16), bsb))
    y = y + xs * _expand_cols(d_skip, SSD_HEADDIM)
    yg = y * jax.nn.silu(z)
    yg = yg * lax.rsqrt(jnp.mean(yg * yg, axis=-1, keepdims=True) + EPS)
    return yg * norm_g, new_states


def _ssd_prompt_body(z_ref, x_ref, b_ref, c_ref, dt_ref, wx_ref, wb_ref, wc_ref, bx_ref, bb_ref, bc_ref,
                     dtb_ref, al_ref, ds_ref, ng_ref, y_ref, hl_ref, xe_ref, be_ref, ce_ref, h_ref):
    c = pl.program_id(1)

    @pl.when(c == 0)
    def _():
        xe_ref[0:8, :] = jnp.zeros((8, SSD_INNER), F32)
        be_ref[0:8, :] = jnp.zeros((8, SSD_GROUPS * SSD_STATE), F32)
        ce_ref[0:8, :] = jnp.zeros((8, SSD_GROUPS * SSD_STATE), F32)
        h_ref[...] = jnp.zeros_like(h_ref)

    def conv(raw_ref, ext_ref, w_ref, bias_ref, sl):
        raw = raw_ref[:, sl]
        ext_ref[8:8 + CHUNK, sl] = raw
        acc = w_ref[3:4, sl] * raw + bias_ref[:, sl]
        for k in range(CONV_K - 1):
            acc = acc + w_ref[k:k + 1, sl] * ext_ref[5 + k:5 + k + CHUNK, sl]
        ext_ref[0:8, sl] = raw[CHUNK - 8:CHUNK, :]
        return jax.nn.silu(acc)

    expand_mat = _head_expand_matrix()
    for g in range(SSD_GROUPS):
        xsl = slice(g * SSD_GW, (g + 1) * SSD_GW)
        nsl = slice(g * SSD_STATE, (g + 1) * SSD_STATE)
        xs = conv(x_ref, xe_ref, wx_ref, bx_ref, xsl)
        bs = conv(b_ref, be_ref, wb_ref, bb_ref, nsl)
        cs = conv(c_ref, ce_ref, wc_ref, bc_ref, nsl)
        y, (h_new,) = _ssd_chunk(xs, bs, cs, z_ref[:, xsl], dt_ref[g], dtb_ref[g], al_ref[g], ds_ref[g],
                                 ng_ref[:, xsl], [h_ref[g]], CHUNK, expand_mat)
        h_ref[g] = h_new
        y_ref[:, xsl] = y.astype(y_ref.dtype)

        @pl.when(c == pl.num_programs(1) - 1)
        def _():
            hl_ref[g] = h_new


SSD_SAMPLE_GROUPS = 8


def _ssd_param_arrays(conv_w, conv_b, dt_bias, a_log, d_skip, norm_g):
    g, hp = SSD_GROUPS, SSD_HPG
    return [conv_w, conv_w, conv_w, conv_b.reshape(1, 1, -1), conv_b.reshape(1, 1, -1), conv_b.reshape(1, 1, -1),
            dt_bias.reshape(g, 1, hp), a_log.reshape(g, 1, hp), d_skip.reshape(g, 1, hp), norm_g.reshape(1, SSD_INNER)]


def _ssd_prompt(zx, dtg, params, bp, tp):
    m = zx.shape[0]
    nc = tp // CHUNK
    gn = SSD_GROUPS * SSD_STATE
    b_blk = 2 * SSD_INNER // gn
    wb_blk = SSD_INNER // gn
    hp3 = (SSD_GROUPS, 1, SSD_HPG)
    zero3 = lambda b, c: (0, 0, 0)
    return pl.pallas_call(
        _ssd_prompt_body,
        out_shape=(jax.ShapeDtypeStruct((m, SSD_INNER), BF16),
                   jax.ShapeDtypeStruct((bp, SSD_GROUPS, SSD_GW, SSD_STATE), F32)),
        grid=(bp, nc),
        in_specs=[pl.BlockSpec((CHUNK, SSD_INNER), lambda b, c: (b * nc + c, 0)),
                  pl.BlockSpec((CHUNK, SSD_INNER), lambda b, c: (b * nc + c, 1)),
                  pl.BlockSpec((CHUNK, gn), lambda b, c: (b * nc + c, b_blk)),
                  pl.BlockSpec((CHUNK, gn), lambda b, c: (b * nc + c, b_blk + 1)),
                  pl.BlockSpec((SSD_GROUPS, CHUNK, SSD_HPG), lambda b, c: (0, b * nc + c, 0)),
                  pl.BlockSpec((None, CONV_K, SSD_INNER), zero3),
                  pl.BlockSpec((None, CONV_K, gn), lambda b, c: (0, 0, wb_blk)),
                  pl.BlockSpec((None, CONV_K, gn), lambda b, c: (0, 0, wb_blk + 1)),
                  pl.BlockSpec((None, 1, SSD_INNER), zero3),
                  pl.BlockSpec((None, 1, gn), lambda b, c: (0, 0, wb_blk)),
                  pl.BlockSpec((None, 1, gn), lambda b, c: (0, 0, wb_blk + 1)),
                  pl.BlockSpec(hp3, zero3), pl.BlockSpec(hp3, zero3), pl.BlockSpec(hp3, zero3),
                  pl.BlockSpec((1, SSD_INNER), lambda b, c: (0, 0))],
        out_specs=(pl.BlockSpec((CHUNK, SSD_INNER), lambda b, c: (b * nc + c, 0)),
                   pl.BlockSpec((None, SSD_GROUPS, SSD_GW, SSD_STATE), lambda b, c: (b, 0, 0, 0))),
        scratch_shapes=[pltpu.VMEM((CHUNK + 8, SSD_INNER), F32), pltpu.VMEM((CHUNK + 8, gn), F32),
                        pltpu.VMEM((CHUNK + 8, gn), F32), pltpu.VMEM((SSD_GROUPS, SSD_GW, SSD_STATE), F32)],
        compiler_params=_cparams(2),
    )(zx, zx, zx, zx, dtg, *params)


def _ssd_sample_body(z_ref, x_ref, b_ref, c_ref, dt_ref, px_ref, pb_ref, pc_ref, wx_ref, wb_ref, wc_ref,
                     bx_ref, bb_ref, bc_ref, dtb_ref, al_ref, ds_ref, ng_ref, h0_ref, yin_ref,
                     y_ref, h_ref, nx_ref, nb_ref, nc_ref):
    del yin_ref

    def conv(raw_ref, buf_ref, new_ref, w_ref, bias_ref):
        raw = raw_ref[...]
        rows, width = raw.shape
        t = _row_pos(rows, SAMPLE_T, width)
        acc = w_ref[3:4, :] * raw + bias_ref[...]
        for k in range(1, CONV_K):
            prev = jnp.concatenate([piece for i in range(SAMPLE_NB) for piece in
                                    (buf_ref[i, CONV_K - 1 - k:CONV_K - 1, :], jnp.zeros((SAMPLE_T - k, width), F32))],
                                   axis=0)
            acc = acc + w_ref[3 - k:4 - k, :] * jnp.where(t >= k, pltpu.roll(raw, k, 0), prev)
        for i in range(SAMPLE_NB):
            new_ref[i] = raw[i * SAMPLE_T + SAMPLE_T - (CONV_K - 1):(i + 1) * SAMPLE_T, :]
        return jax.nn.silu(acc)

    xs_all = conv(x_ref, px_ref, nx_ref, wx_ref, bx_ref)
    bs_all = conv(b_ref, pb_ref, nb_ref, wb_ref, bb_ref)
    cs_all = conv(c_ref, pc_ref, nc_ref, wc_ref, bc_ref)
    expand_mat = _head_expand_matrix()
    for g in range(SSD_SAMPLE_GROUPS):
        xsl = slice(g * SSD_GW, (g + 1) * SSD_GW)
        nsl = slice(g * SSD_STATE, (g + 1) * SSD_STATE)
        y, h_new = _ssd_chunk(xs_all[:, xsl], bs_all[:, nsl], cs_all[:, nsl], z_ref[:, xsl], dt_ref[g], dtb_ref[g],
                              al_ref[g], ds_ref[g], ng_ref[:, xsl], [h0_ref[i, g] for i in range(SAMPLE_NB)],
                              SAMPLE_T, expand_mat)
        for i in range(SAMPLE_NB):
            h_ref[i, g] = h_new[i]
        y_ref[:, xsl] = y.astype(y_ref.dtype)


def _ssd_sample(zx, dtg, conv_state, params, state, y_all, rows_p, bs):
    rows = SAMPLE_NB * SAMPLE_T
    rb0 = rows_p // rows
    ng = SSD_SAMPLE_GROUPS
    xw, nw = ng * SSD_GW, ng * SSD_STATE
    gn = SSD_GROUPS * SSD_STATE
    nk = CONV_K - 1
    x_blk = SSD_INNER // xw
    b_blk, c_blk = 2 * SSD_INNER // nw, (2 * SSD_INNER + gn) // nw
    wb_blk, wc_blk = SSD_INNER // nw, (SSD_INNER + gn) // nw
    hp3 = (ng, 1, SSD_HPG)
    st_spec = pl.BlockSpec((None, SAMPLE_NB, ng, SSD_GW, SSD_STATE), lambda bb, q: (0, bb, q, 0, 0))
    new_x_spec = pl.BlockSpec((SAMPLE_NB, nk, xw), lambda bb, q: (bb, 0, q))
    new_n_spec = pl.BlockSpec((SAMPLE_NB, nk, nw), lambda bb, q: (bb, 0, q))
    return pl.pallas_call(
        _ssd_sample_body,
        out_shape=(jax.ShapeDtypeStruct(y_all.shape, y_all.dtype), jax.ShapeDtypeStruct(state.shape, F32),
                   jax.ShapeDtypeStruct((bs, nk, SSD_INNER), F32), jax.ShapeDtypeStruct((bs, nk, gn), F32),
                   jax.ShapeDtypeStruct((bs, nk, gn), F32)),
        grid=(bs // SAMPLE_NB, SSD_GROUPS // ng),
        in_specs=[pl.BlockSpec((rows, xw), lambda bb, q: (rb0 + bb, q)),
                  pl.BlockSpec((rows, xw), lambda bb, q: (rb0 + bb, x_blk + q)),
                  pl.BlockSpec((rows, nw), lambda bb, q: (rb0 + bb, b_blk + q)),
                  pl.BlockSpec((rows, nw), lambda bb, q: (rb0 + bb, c_blk + q)),
                  pl.BlockSpec((ng, rows, SSD_HPG), lambda bb, q: (q, rb0 + bb, 0)),
                  pl.BlockSpec((None, SAMPLE_NB, nk, xw), lambda bb, q: (0, bb, 0, q)),
                  pl.BlockSpec((None, SAMPLE_NB, nk, nw), lambda bb, q: (0, bb, 0, wb_blk + q)),
                  pl.BlockSpec((None, SAMPLE_NB, nk, nw), lambda bb, q: (0, bb, 0, wc_blk + q)),
                  pl.BlockSpec((None, CONV_K, xw), lambda bb, q: (0, 0, q)),
                  pl.BlockSpec((None, CONV_K, nw), lambda bb, q: (0, 0, wb_blk + q)),
                  pl.BlockSpec((None, CONV_K, nw), lambda bb, q: (0, 0, wc_blk + q)),
                  pl.BlockSpec((None, 1, xw), lambda bb, q: (0, 0, q)),
                  pl.BlockSpec((None, 1, nw), lambda bb, q: (0, 0, wb_blk + q)),
                  pl.BlockSpec((None, 1, nw), lambda bb, q: (0, 0, wc_blk + q)),
                  pl.BlockSpec(hp3, lambda bb, q: (q, 0, 0)),
                  pl.BlockSpec(hp3, lambda bb, q: (q, 0, 0)),
                  pl.BlockSpec(hp3, lambda bb, q: (q, 0, 0)),
                  pl.BlockSpec((1, xw), lambda bb, q: (0, q)),
                  st_spec, pl.BlockSpec(memory_space=pl.ANY)],
        out_specs=(pl.BlockSpec((rows, xw), lambda bb, q: (rb0 + bb, q)), st_spec,
                   new_x_spec, new_n_spec, new_n_spec),
        input_output_aliases={19: 0},
        compiler_params=_cparams(2),
    )(zx, zx, zx, zx, dtg, conv_state, conv_state, conv_state, *params, state, y_all)


def _s5_body(*refs, sub, nsub, n_scan, has_h0, has_alias):
    it = iter(refs)
    x_ref, wdc_ref, kdc_ref, vtc_ref = next(it), next(it), next(it), next(it)
    are_ref, aim_ref, sre_ref, sim_ref, dsk_ref = (next(it) for _ in range(5))
    h0_ref = next(it) if has_h0 else None
    if has_alias:
        next(it)
    s_ref, hl_ref = next(it), next(it)
    m1_ref, vt_ref = next(it), next(it)
    hw2 = 2 * S5_HW

    @pl.when(pl.program_id(1) == 0)
    def _():
        def iota(shape, axis):
            return lax.broadcasted_iota(jnp.int32, shape, axis)

        def expand(compact, spread, keep):
            return jnp.where(keep, _dot(compact.astype(BF16), spread), 0.0).astype(BF16)

        q, col = iota((128, hw2), 0), iota((128, hw2), 1)
        spread = jnp.where(((q >> 6) == (col >> 9)) & ((q & 63) == (col & 63)), 1.0, 0.0).astype(BF16)
        row, col = iota((sub * 128, hw2), 0), iota((sub * 128, hw2), 1)
        m1_ref[:, 0:hw2] = expand(wdc_ref[...], spread, ((row >> 4) & 7) == ((col >> 6) & 7))
        q, col = iota((128, 128), 0), iota((128, 128), 1)
        spread = jnp.where((q < S5_GROUP) & (q == (col & 15)), 1.0, 0.0).astype(BF16)
        keep = (q >> 4) == (col >> 4)
        kd = [expand(kdc_ref[d], spread, keep) for d in range(sub)]
        for s in range(sub):
            for t in range(sub):
                blk = kd[t - s] if t >= s else jnp.zeros((128, 128), BF16)
                m1_ref[s * 128:(s + 1) * 128, hw2 + t * 128:hw2 + (t + 1) * 128] = blk
        q, col = iota((128, sub * 128), 0), iota((128, sub * 128), 1)
        spread = jnp.where(((q >> 4) == (col >> 7)) & ((q & 15) == (col & 15)), 1.0, 0.0).astype(BF16)
        row, col = iota((hw2, sub * 128), 0), iota((hw2, sub * 128), 1)
        vt_ref[...] = expand(vtc_ref[...], spread, ((row >> 6) & 7) == ((col >> 4) & 7))

    xs = [x_ref[pl.ds(s, nsub, stride=sub), :] for s in range(sub)]
    r = _dot(jnp.concatenate(xs, axis=1).astype(BF16), m1_ref[...])
    cb, y = r[:, :2 * S5_HW], r[:, 2 * S5_HW:]

    def cmul(h, a_re, a_im):
        return h * a_re + pltpu.roll(h, S5_HW, 1) * a_im

    if has_h0:
        hprev = h0_ref[...]
        h = cb + cmul(hprev, are_ref[...], aim_ref[...])
    else:
        h = cb
        j = lax.broadcasted_iota(jnp.int32, h.shape, 0)
        for step in range(n_scan):
            shift = 1 << step
            prev = jnp.where(j >= shift, pltpu.roll(h, shift, 0), 0.0)
            h = h + cmul(prev, sre_ref[step], sim_ref[step])
        hprev = jnp.where(j >= 1, pltpu.roll(h, 1, 0), 0.0)
    hl_ref[...] = h[nsub - 1:nsub, :] if not has_h0 else h
    y = y + _dot(hprev.astype(BF16), vt_ref[...])
    for t in range(sub):
        s_ref[pl.ds(t, nsub, stride=sub), :] = y[:, t * 128:(t + 1) * 128] + dsk_ref[...] * xs[t]


def _s5_tables(lam_re, lam_im, b_re, b_im, c_re, c_im, log_dt, sub, n_scan):
    nb, g8, gc, p = S5_NBLK, S5_G8, S5_GROUP, S5_STATE
    dt = jnp.exp(log_dt)[:, None]

    def apow(ds):
        d = jnp.asarray(ds, F32).reshape(-1, 1, 1)
        mag = jnp.exp(lam_re * dt * d)
        return mag * jnp.cos(lam_im * dt * d), mag * jnp.sin(lam_im * dt * d)

    a_re, a_im = (v[0] for v in apow([1.0]))
    den = lam_re * lam_re + lam_im * lam_im
    f_re = ((a_re - 1.0) * lam_re + a_im * lam_im) / den
    f_im = (a_im * lam_re - (a_re - 1.0) * lam_im) / den
    bb_re = f_re[..., None] * b_re - f_im[..., None] * b_im
    bb_im = f_re[..., None] * b_im + f_im[..., None] * b_re

    def lane_pad(m):
        return jnp.pad(m, [(0, 0)] * (m.ndim - 1) + [(0, 128 - m.shape[-1])])

    p_re, p_im = apow(range(sub))
    ab_re = p_re[..., None] * bb_re - p_im[..., None] * bb_im
    ab_im = p_re[..., None] * bb_im + p_im[..., None] * bb_re
    kd = jnp.einsum('gcp,dgpk->dgkc', c_re, ab_re) - jnp.einsum('gcp,dgpk->dgkc', c_im, ab_im)
    kdc = lane_pad(kd.reshape(sub, nb, g8 * gc, gc).transpose(1, 0, 2, 3))
    w = jnp.concatenate([jnp.swapaxes(ab_re, 2, 3), jnp.swapaxes(ab_im, 2, 3)], axis=3)[::-1]
    wdc = w.reshape(sub, nb, g8 * gc, 2 * p).transpose(1, 0, 2, 3).reshape(nb, sub * g8 * gc, 2 * p)
    q_re, q_im = apow(range(1, sub + 1))
    m_re = c_re[None] * q_re[:, :, None, :] - c_im[None] * q_im[:, :, None, :]
    m_im = c_re[None] * q_im[:, :, None, :] + c_im[None] * q_re[:, :, None, :]

    def rows(m):
        return m.reshape(sub, nb, g8, gc, p).transpose(1, 2, 4, 0, 3).reshape(nb, g8 * p, sub * gc)
    vtc = lane_pad(jnp.concatenate([rows(m_re), -rows(m_im)], axis=1))

    def lanes(re, im):
        n = re.shape[0]
        re = re.reshape(n, nb, 1, S5_HW).transpose(1, 0, 2, 3)
        im = im.reshape(n, nb, 1, S5_HW).transpose(1, 0, 2, 3)
        return jnp.concatenate([re, re], axis=3), jnp.concatenate([-im, im], axis=3)

    are, aim = (v[:, 0] for v in lanes(*apow([float(sub)])))
    sre, sim = lanes(*apow([float(sub * (1 << k)) for k in range(max(n_scan, 1))]))
    return wdc, kdc, vtc, are, aim, sre, sim


def _s5_core(xz, tables, d_skip, s_all, h0, *, sub, nsub, n_seq, row_block0, n_scan):
    has_h0 = h0 is not None
    rows = sub * nsub
    ns = tables[5].shape[1]
    hw2 = 2 * S5_HW
    sw = sub * 128

    def blk(shape, f):
        return pl.BlockSpec(shape, lambda gb, b: f(gb, b))
    in_specs = [blk((rows, 128), lambda gb, b: (row_block0 + b, gb)),
                blk((None, sw, 128), lambda gb, b: (gb, 0, 0)),
                blk((None, sub, 128, 128), lambda gb, b: (gb, 0, 0, 0)),
                blk((None, hw2, 128), lambda gb, b: (gb, 0, 0)),
                blk((None, 1, hw2), lambda gb, b: (gb, 0, 0)),
                blk((None, 1, hw2), lambda gb, b: (gb, 0, 0)),
                blk((None, ns, 1, hw2), lambda gb, b: (gb, 0, 0, 0)),
                blk((None, ns, 1, hw2), lambda gb, b: (gb, 0, 0, 0)),
                blk((1, 128), lambda gb, b: (0, gb))]
    args = [xz, *tables, d_skip.reshape(1, -1)]
    if has_h0:
        in_specs.append(blk((None, nsub, hw2), lambda gb, b: (gb, 0, 0)))
        args.append(h0)
        hl_shape = (S5_NBLK, nsub, hw2)
        hl_spec = blk((None, nsub, hw2), lambda gb, b: (gb, 0, 0))
    else:
        hl_shape = (S5_NBLK, n_seq, 1, hw2)
        hl_spec = blk((None, None, 1, hw2), lambda gb, b: (gb, b, 0, 0))
    aliases = {}
    if s_all is not None:
        in_specs.append(pl.BlockSpec(memory_space=pl.ANY))
        args.append(s_all)
        aliases = {len(args) - 1: 0}
    body = functools.partial(_s5_body, sub=sub, nsub=nsub, n_scan=n_scan, has_h0=has_h0,
                             has_alias=s_all is not None)
    return pl.pallas_call(
        body,
        out_shape=(jax.ShapeDtypeStruct((xz.shape[0], D_MODEL), F32), jax.ShapeDtypeStruct(hl_shape, F32)),
        grid=(S5_NBLK, n_seq),
        in_specs=in_specs,
        out_specs=(blk((rows, 128), lambda gb, b: (row_block0 + b, gb)), hl_spec),
        scratch_shapes=[pltpu.VMEM((sw, hw2 + sw), BF16), pltpu.VMEM((hw2, sw), BF16)],
        input_output_aliases=aliases,
        compiler_params=_cparams(2),
    )(*args)


def _s5_glu_body(s_ref, st_ref, z_ref, w_ref, b_ref, o_ref, gb_ref, *, tm, row_chunk):
    @pl.when(pl.program_id(1) == 0)
    def _():
        def body(i, carry):
            r0 = pl.multiple_of(i * row_chunk, 16)
            gb_ref[pl.ds(r0, row_chunk), :] = jax.nn.gelu(s_ref[pl.ds(r0, row_chunk), :]).astype(BF16)
            return carry
        lax.fori_loop(0, tm // row_chunk, body, 0)
    g = jax.nn.gelu(st_ref[...])
    o = g * jax.nn.sigmoid(_dot(gb_ref[...], w_ref[...].astype(BF16)) + b_ref[...])
    o_ref[...] = (o * jax.nn.silu(z_ref[...])).astype(o_ref.dtype)


def _s5_glu(s_all, xz, glu_w, glu_b):
    m, k = s_all.shape
    tn = 512
    tm = _largest_divisor(m, 1088, 16)
    row_chunk = _largest_divisor(tm, 272, 16)
    zb0 = k // tn
    body = functools.partial(_s5_glu_body, tm=tm, row_chunk=row_chunk)
    return pl.pallas_call(
        body,
        out_shape=jax.ShapeDtypeStruct((m, k), BF16),
        grid=(m // tm, k // tn),
        in_specs=[pl.BlockSpec((tm, k), lambda i, j: (i, 0)),
                  pl.BlockSpec((tm, tn), lambda i, j: (i, j)),
                  pl.BlockSpec((tm, tn), lambda i, j: (i, zb0 + j)),
                  pl.BlockSpec((None, k, tn), lambda i, j: (0, 0, j)),
                  pl.BlockSpec((1, tn), lambda i, j: (0, j))],
        out_specs=pl.BlockSpec((tm, tn), lambda i, j: (i, j)),
        scratch_shapes=[pltpu.VMEM((tm, k), BF16)],
        compiler_params=_cparams(2),
    )(s_all, s_all, xz, glu_w, glu_b.reshape(1, k))


def _last_conv_inputs(a, bp, tp, col0, n_cols):
    return jnp.stack([a[(b + 1) * tp - (CONV_K - 1):(b + 1) * tp, col0:col0 + n_cols] for b in range(bp)])


def _lru_layer(in_proj, out_proj, j, dims, state_conv, state_h, w_in, conv_w, conv_b, wa, ba, wx, bx, lam, w_out):
    bp, tp, bs = dims
    rows_p = bp * tp
    xz = in_proj(w_in, j)
    params = _lru_params(conv_w[j:j + 1], conv_b[j], wa[j:j + 1], wx[j:j + 1], ba[j], bx[j], lam[j])
    y, hl_p = _lru_prompt(xz, params, bp, tp)
    prevs = _conv_prev_rows(state_conv[j])
    h0_rows = jnp.repeat(state_h[j], SAMPLE_T, axis=0)
    y, h_rows = _lru_sample(xz, prevs, h0_rows, params, y, rows_p)
    x_s = xz[rows_p:, :LRU_WIDTH].reshape(bs, SAMPLE_T, LRU_WIDTH)
    outs = (_last_conv_inputs(xz, bp, tp, 0, LRU_WIDTH), x_s[:, SAMPLE_T - (CONV_K - 1):],
            hl_p.reshape(bp, LRU_WIDTH), h_rows.reshape(bs, SAMPLE_T, LRU_WIDTH)[:, SAMPLE_T - 1])
    return out_proj(y, w_out, j), outs


def _ret_layer(in_proj, out_proj, j, dims, state, w_in, gn, w_out):
    bp, tp, bs = dims
    rows_p = bp * tp
    qkvz = in_proj(w_in, j)
    y, s_p = _ret_prompt(qkvz, gn[j], bp, tp)
    y, s_s = _ret_sample(qkvz, gn[j], state[j:j + 1], y, rows_p, bs)
    return out_proj(y, w_out, j, tn=256), (s_p, s_s[0])


def _ssd_layer(in_proj, out_proj, j, dims, state_conv, state, w_in, conv_w, conv_b, dt_bias, a_log, d_skip,
               norm_g, w_out):
    bp, tp, bs = dims
    rows_p = bp * tp
    m = rows_p + bs * SAMPLE_T
    n_main = SSD_INNER + SSD_CONV_DIM
    w_t = jnp.swapaxes(w_in, 1, 2)
    zx = in_proj(w_t, j, n_cols=n_main, w_transposed=True)
    dt = in_proj(w_t[j:j + 1, n_main:, :], 0, w_transposed=True)
    dtg = dt.reshape(m, SSD_GROUPS, SSD_HPG).transpose(1, 0, 2)
    params = _ssd_param_arrays(conv_w[j:j + 1], conv_b[j], dt_bias[j], a_log[j], d_skip[j], norm_g[j])
    y, hl_p = _ssd_prompt(zx, dtg, params, bp, tp)
    st = state[j:j + 1].reshape(1, bs, SSD_GROUPS, SSD_GW, SSD_STATE)
    y, hl_s, *new_conv = _ssd_sample(zx, dtg, state_conv[j:j + 1], params, st, y, rows_p, bs)
    shape = (SSD_HEADS, SSD_HEADDIM, SSD_STATE)
    outs = (_last_conv_inputs(zx, bp, tp, SSD_INNER, SSD_CONV_DIM), jnp.concatenate(new_conv, axis=2),
            hl_p.reshape((bp,) + shape), hl_s.reshape((bs,) + shape))
    return out_proj(y, w_out, j, tn=256), outs


S5_SUB = 8


def _s5_layer(in_proj, out_proj, j, dims, h0_re, h0_im, w_in, lam_re, lam_im, b_re, b_im, c_re, c_im, d_skip,
              log_dt, glu_w, glu_b, w_out):
    bp, tp, bs = dims
    rows_p = bp * tp
    xz = in_proj(w_in, j)
    nsub_p = tp // S5_SUB
    n_scan = max(nsub_p - 1, 0).bit_length()
    par = (lam_re[j], lam_im[j], b_re[j], b_im[j], c_re[j], c_im[j], log_dt[j])
    s_all, hl_p = _s5_core(xz, _s5_tables(*par, S5_SUB, n_scan), d_skip[j], None, None,
                           sub=S5_SUB, nsub=nsub_p, n_seq=bp, row_block0=0, n_scan=n_scan)

    def to_lanes(v):
        return v.reshape(bs, S5_NBLK, S5_HW).transpose(1, 0, 2)
    h0 = jnp.concatenate([to_lanes(h0_re[j]), to_lanes(h0_im[j])], axis=2)
    s_all, hl_s = _s5_core(xz, _s5_tables(*par, SAMPLE_T, 0), d_skip[j], s_all, h0,
                           sub=SAMPLE_T, nsub=bs, n_seq=1, row_block0=rows_p // (bs * SAMPLE_T), n_scan=0)
    y = _s5_glu(s_all, xz, glu_w[j:j + 1], glu_b[j])

    def from_lanes(v, nb):
        return v.transpose(1, 0, 2).reshape(nb, S5_GROUPS, S5_STATE)
    hl_p = hl_p.reshape(S5_NBLK, bp, 2 * S5_HW)
    outs = (from_lanes(hl_p[..., :S5_HW], bp), from_lanes(hl_s[..., :S5_HW], bs),
            from_lanes(hl_p[..., S5_HW:], bp), from_lanes(hl_s[..., S5_HW:], bs))
    return out_proj(y, w_out, j), outs


def kernel(x_prompt, x_sample, state_lru_conv, state_lru_h, state_ret, state_ssd_conv, state_ssd, state_s5_re, state_s5_im, cache_mem_k, cache_mem_v, mem_prompt, mix_norm, xa_norm, xa_mem_norm, xa_wq, xa_wkv, xa_wo, final_norm, lru_w_in, lru_conv_w, lru_conv_b, lru_wa, lru_ba, lru_wx, lru_bx, lru_lambda, lru_w_out, ret_w_in, ret_gn, ret_w_out, ssd_w_in, ssd_conv_w, ssd_conv_b, ssd_dt_bias, ssd_a_log, ssd_d, ssd_norm, ssd_w_out, s5_w_in, s5_lambda_re, s5_lambda_im, s5_b_re, s5_b_im, s5_c_re, s5_c_im, s5_d, s5_log_dt, s5_glu_w, s5_glu_b, s5_w_out):
    bp, tp, d = x_prompt.shape
    bs, ts, _ = x_sample.shape
    depth = mix_norm.shape[0]
    assert d == D_MODEL and ts == SAMPLE_T and tp % LRU_TC == 0 and bs % SAMPLE_NB == 0
    rows_p, rows_s = bp * tp, bs * ts
    assert rows_p % rows_s == 0
    dims = (bp, tp, bs)
    h = jnp.concatenate([x_prompt.reshape(rows_p, d), x_sample.reshape(rows_s, d)], axis=0)
    mem = mem_prompt.reshape(bp * MEM_LEN, d)
    mix_g = mix_norm.reshape(depth, 1, d)
    xa_g = xa_norm.reshape(depth, 1, d)
    mem_g = xa_mem_norm.reshape(depth, 1, d)

    outs = {k: [] for k in ("lru", "ret", "ssd", "s5")}
    mem_k, mem_v = [], []
    xn = None
    for i in range(depth):
        kind, j = i % 4, i // 4

        def normed_proj(gain, w, idx, h=h, xn=xn, i=i, **kw):
            if xn is not None:
                return _matmul(xn, w, idx, **kw)
            return _matmul(h, w, idx, norm_g=gain, g_idx=i, **kw)

        def out_proj(y, w, idx, h=h, i=i, **kw):
            if w.shape[1] == D_MODEL:
                return _matmul_res_norm(y, w, idx, h, xa_g, i)
            return _matmul(y, w, idx, residual=h, **kw), None

        in_proj = functools.partial(normed_proj, mix_g)
        if kind == 0:
            (h, xn), o = _lru_layer(in_proj, out_proj, j, dims, state_lru_conv, state_lru_h, lru_w_in,
                                    lru_conv_w, lru_conv_b, lru_wa, lru_ba, lru_wx, lru_bx, lru_lambda, lru_w_out)
            outs["lru"].append(o)
        elif kind == 1:
            (h, xn), o = _ret_layer(in_proj, out_proj, j, dims, state_ret, ret_w_in, ret_gn, ret_w_out)
            outs["ret"].append(o)
        elif kind == 2:
            (h, xn), o = _ssd_layer(in_proj, out_proj, j, dims, state_ssd_conv, state_ssd, ssd_w_in,
                                    ssd_conv_w, ssd_conv_b, ssd_dt_bias, ssd_a_log, ssd_d, ssd_norm, ssd_w_out)
            outs["ssd"].append(o)
        else:
            (h, xn), o = _s5_layer(in_proj, out_proj, j, dims, state_s5_re, state_s5_im, s5_w_in,
                                   s5_lambda_re, s5_lambda_im, s5_b_re, s5_b_im, s5_c_re, s5_c_im, s5_d, s5_log_dt,
                                   s5_glu_w, s5_glu_b, s5_w_out)
            outs["s5"].append(o)
        kv = _matmul(mem, xa_wkv, i, norm_g=mem_g, g_idx=i)
        q = normed_proj(xa_g, xa_wq, i, h=h, xn=xn, out_dtype=BF16)
        o_att = _xattn_prompt(q, kv, bp, tp)
        o_att = _xattn_sample(q, cache_mem_k, cache_mem_v, i, o_att, rows_p, bs)
        if i + 1 < depth:
            h, xn = _matmul_res_norm(o_att, xa_wo, i, h, mix_g, i + 1)
        else:
            h, xn = _matmul(o_att, xa_wo, i, residual=h), None
        mem_k.append(kv[:, :d].reshape(bp, MEM_LEN, XA_HEADS, XA_HD))
        mem_v.append(kv[:, d:].reshape(bp, MEM_LEN, XA_HEADS, XA_HD))
    y_p = _rmsnorm(h, final_norm, 0, rows_p)
    y_s = _rmsnorm(h, final_norm, rows_p, rows_s)

    def stack(kind, idx):
        return jnp.stack([o[idx] for o in outs[kind]])
    return (y_p.reshape(bp, tp, d), y_s.reshape(bs, ts, d),
            stack("lru", 0), stack("lru", 1), stack("lru", 2), stack("lru", 3),
            stack("ret", 0), stack("ret", 1),
            stack("ssd", 0), stack("ssd", 1), stack("ssd", 2), stack("ssd", 3),
            stack("s5", 0), stack("s5", 1), stack("s5", 2), stack("s5", 3),
            jnp.stack(mem_k), jnp.stack(mem_v))
```

```python
import functools
import math

import jax
import jax.numpy as jnp
from jax import lax
from jax.experimental import pallas as pl
from jax.experimental.pallas import tpu as pltpu

F32 = jnp.float32
BF16 = jnp.bfloat16

D_MODEL = 2048
PAST_LEN = 16384
EPS = 1e-6
CONV_K = 4
CHUNK = 128
LRU_WIDTH = D_MODEL
LRU_BLOCK = 256
LRU_C = 8.0
RET_HEADS = 8
RET_DK = 256
RET_DV = 512
RET_VW = RET_HEADS * RET_DV
ROPE_BASE = 10000.0
SSD_INNER = 2 * D_MODEL
SSD_HEADDIM = 64
SSD_HEADS = 64
SSD_GROUPS = 8
SSD_HPG = 8
SSD_STATE = 128
SSD_GW = SSD_HPG * SSD_HEADDIM
SSD_CONV_DIM = SSD_INNER + 2 * SSD_GROUPS * SSD_STATE
S5_GROUP = 16
S5_GROUPS = 128
S5_STATE = 64
S5_G8 = 8
S5_NBLK = S5_GROUPS // S5_G8
S5_HW = S5_G8 * S5_STATE
MEM_LEN = 256
XA_HEADS = 4
XA_HD = 512
SAMPLE_T = 4
SAMPLE_NB = 4

VMEM_LIMIT_BYTES = 56 * 1024 * 1024
MM_MAX_ROWS = 2176
MM_X_DOUBLE_BUFFER_BYTES = 18 * 1024 * 1024
NT_DIMS = (((1,), (1,)), ((), ()))
TN_DIMS = (((0,), (0,)), ((), ()))


def _cparams(n_axes):
    return pltpu.CompilerParams(dimension_semantics=("arbitrary",) * n_axes,
                                vmem_limit_bytes=VMEM_LIMIT_BYTES)


def _dot(a, b):
    return jnp.dot(a, b, preferred_element_type=F32)


def _dot_nt(a, b):
    return lax.dot_general(a, b, NT_DIMS, preferred_element_type=F32)


def _dot_tn(a, b):
    return lax.dot_general(a, b, TN_DIMS, preferred_element_type=F32)


def _largest_divisor(n, cap, mult):
    for d in range(min(cap, n) // mult * mult, 0, -mult):
        if n % d == 0:
            return d
    raise ValueError(f"no divisor of {n} that is a multiple of {mult}")


def _expm1(x):
    return jnp.where(jnp.abs(x) < 0.5, jnp.tanh(0.5 * x) * (jnp.exp(x) + 1.0), jnp.exp(x) - 1.0)


def _expand_cols(v, width):
    rows, n = v.shape
    return jnp.concatenate([jnp.broadcast_to(v[:, r:r + 1], (rows, width)) for r in range(n)], axis=1)


def _mm_body(*refs, has_norm, has_res, stage_x, tm, row_chunk, w_transposed, has_alias):
    it = iter(refs)
    x_ref, w_ref = next(it), next(it)
    g_ref = next(it) if has_norm else None
    r_ref = next(it) if has_res else None
    if has_alias:
        next(it)
    o_ref = next(it)
    if stage_x:
        xb_ref = next(it)

        @pl.when(pl.program_id(1) == 0)
        def _():
            def body(i, carry):
                r0 = pl.multiple_of(i * row_chunk, 16)
                x = x_ref[pl.ds(r0, row_chunk), :].astype(F32)
                if has_norm:
                    x = x * lax.rsqrt(jnp.mean(x * x, axis=-1, keepdims=True) + EPS) * g_ref[...]
                xb_ref[pl.ds(r0, row_chunk), :] = x.astype(BF16)
                return carry
            lax.fori_loop(0, tm // row_chunk, body, 0)
        xb = xb_ref[...]
    else:
        xb = x_ref[...]
    wb = w_ref[...].astype(BF16)
    acc = _dot_nt(xb, wb) if w_transposed else _dot(xb, wb)
    if has_res:
        acc = acc + r_ref[...]
    o_ref[...] = acc.astype(o_ref.dtype)


def _matmul(x, w, w_idx, *, norm_g=None, g_idx=0, residual=None, out_dtype=F32, tn=512, col_off=0, n_cols=None,
            w_transposed=False, stack=None):
    m, k = x.shape
    n_total = w.shape[1] if w_transposed else w.shape[2]
    n_cols = n_total if n_cols is None else n_cols
    tn = min(tn, n_cols)
    assert n_cols % tn == 0 and col_off % tn == 0
    tm = _largest_divisor(m, MM_MAX_ROWS, 16)
    stage_x = x.dtype != BF16 or norm_g is not None
    row_chunk = _largest_divisor(tm, 272, 16)
    cb = col_off // tn
    x_bytes = tm * k * x.dtype.itemsize
    x_mode = {} if 2 * x_bytes <= MM_X_DOUBLE_BUFFER_BYTES else {"pipeline_mode": pl.Buffered(1)}
    w_spec = (pl.BlockSpec((None, tn, k), lambda i, j: (w_idx, j + cb, 0)) if w_transposed
              else pl.BlockSpec((None, k, tn), lambda i, j: (w_idx, 0, j + cb)))
    in_specs = [pl.BlockSpec((tm, k), lambda i, j: (i, 0), **x_mode), w_spec]
    args = [x, w]
    if norm_g is not None:
        in_specs.append(pl.BlockSpec((None, 1, k), lambda i, j: (g_idx, 0, 0)))
        args.append(norm_g)
    if residual is not None:
        in_specs.append(pl.BlockSpec((tm, tn), lambda i, j: (i, j)))
        args.append(residual)
    out_shape = jax.ShapeDtypeStruct((m, n_cols), out_dtype)
    out_spec = pl.BlockSpec((tm, tn), lambda i, j: (i, j))
    aliases = {}
    if stack is not None:
        depth, s_idx, buf = stack
        out_shape = jax.ShapeDtypeStruct((depth, m, n_cols), out_dtype)
        out_spec = pl.BlockSpec((None, tm, tn), lambda i, j: (s_idx, i, j))
        if buf is not None:
            in_specs.append(pl.BlockSpec(memory_space=pl.ANY))
            args.append(buf)
            aliases = {len(args) - 1: 0}
    body = functools.partial(_mm_body, has_norm=norm_g is not None, has_res=residual is not None,
                             stage_x=stage_x, tm=tm, row_chunk=row_chunk, w_transposed=w_transposed,
                             has_alias=bool(aliases))
    return pl.pallas_call(
        body,
        out_shape=out_shape,
        grid=(m // tm, n_cols // tn),
        in_specs=in_specs,
        out_specs=out_spec,
        scratch_shapes=[pltpu.VMEM((tm, k), BF16)] if stage_x else [],
        input_output_aliases=aliases,
        compiler_params=_cparams(2),
    )(*args)


def _mm_res_norm_body(x_ref, w_ref, r_ref, g_ref, h_ref, xn_ref, *, tm, tn, n_tiles, row_chunk):
    j = pl.program_id(1)
    acc = _dot(x_ref[...], w_ref[...].astype(BF16)) + r_ref[...]
    for jj in range(n_tiles):
        @pl.when(j == jj)
        def _():
            h_ref[:, jj * tn:(jj + 1) * tn] = acc

    @pl.when(j == n_tiles - 1)
    def _():
        def body(i, carry):
            r0 = pl.multiple_of(i * row_chunk, 16)
            h = h_ref[pl.ds(r0, row_chunk), :]
            hn = h * lax.rsqrt(jnp.mean(h * h, axis=-1, keepdims=True) + EPS) * g_ref[...]
            xn_ref[pl.ds(r0, row_chunk), :] = hn.astype(BF16)
            return carry
        lax.fori_loop(0, tm // row_chunk, body, 0)


MM_NORM_OUT_ROWS = 1088


def _matmul_res_norm(x, w, w_idx, residual, next_g, g_idx, *, tn=512):
    m, k = x.shape
    n = w.shape[2]
    tm = _largest_divisor(m, MM_NORM_OUT_ROWS, 16)
    row_chunk = _largest_divisor(tm, 272, 16)
    n_tiles = n // tn
    body = functools.partial(_mm_res_norm_body, tm=tm, tn=tn, n_tiles=n_tiles, row_chunk=row_chunk)
    return pl.pallas_call(
        body,
        out_shape=(jax.ShapeDtypeStruct((m, n), F32), jax.ShapeDtypeStruct((m, n), BF16)),
        grid=(m // tm, n_tiles),
        in_specs=[pl.BlockSpec((tm, k), lambda i, j: (i, 0)),
                  pl.BlockSpec((None, k, tn), lambda i, j: (w_idx, 0, j)),
                  pl.BlockSpec((tm, tn), lambda i, j: (i, j)),
                  pl.BlockSpec((None, 1, n), lambda i, j: (g_idx, 0, 0))],
        out_specs=(pl.BlockSpec((tm, n), lambda i, j: (i, 0)), pl.BlockSpec((tm, n), lambda i, j: (i, 0))),
        compiler_params=_cparams(2),
    )(x, w, residual, next_g)


def _rmsnorm_body(x_ref, g_ref, o_ref):
    x = x_ref[...]
    o_ref[...] = x * lax.rsqrt(jnp.mean(x * x, axis=-1, keepdims=True) + EPS) * g_ref[...]


def _rmsnorm(x, g, row0, n_rows):
    k = x.shape[1]
    tm = _largest_divisor(math.gcd(n_rows, row0) if row0 else n_rows, 512, 8)
    rb0 = row0 // tm
    return pl.pallas_call(
        _rmsnorm_body,
        out_shape=jax.ShapeDtypeStruct((n_rows, k), F32),
        grid=(n_rows // tm,),
        in_specs=[pl.BlockSpec((tm, k), lambda i: (rb0 + i, 0)), pl.BlockSpec((1, k), lambda i: (0, 0))],
        out_specs=pl.BlockSpec((tm, k), lambda i: (i, 0)),
        compiler_params=_cparams(1),
    )(x, g.reshape(1, k))


def _softmax_rows(s):
    e = jnp.exp(s - jnp.max(s, axis=-1, keepdims=True))
    return e / jnp.sum(e, axis=-1, keepdims=True)


def _xattn_prompt_body(q_ref, k_ref, v_ref, o_ref):
    kb = k_ref[...].astype(BF16)
    vb = v_ref[...].astype(BF16)
    for h in range(XA_HEADS):
        sl = slice(h * XA_HD, (h + 1) * XA_HD)
        p = _softmax_rows(_dot_nt(q_ref[:, sl], kb[:, sl]) * XA_HD ** -0.5)
        o_ref[:, sl] = _dot(p.astype(BF16), vb[:, sl]).astype(o_ref.dtype)


def _xattn_prompt(q, kv, layer, bp, tp):
    m = q.shape[0]
    tq = min(tp, 1024)
    nq = tp // tq
    return pl.pallas_call(
        _xattn_prompt_body,
        out_shape=jax.ShapeDtypeStruct((m, D_MODEL), BF16),
        grid=(bp, nq),
        in_specs=[pl.BlockSpec((tq, D_MODEL), lambda b, i: (b * nq + i, 0)),
                  pl.BlockSpec((None, MEM_LEN, D_MODEL), lambda b, i: (layer, b, 0)),
                  pl.BlockSpec((None, MEM_LEN, D_MODEL), lambda b, i: (layer, b, 1))],
        out_specs=pl.BlockSpec((tq, D_MODEL), lambda b, i: (b * nq + i, 0)),
        compiler_params=_cparams(2),
    )(q, kv, kv)


def _log2(n):
    assert n & (n - 1) == 0, n
    return n.bit_length() - 1


def _row_segment(rows, seg_len, width):
    return lax.broadcasted_iota(jnp.int32, (rows, width), 0) >> _log2(seg_len)


def _row_pos(rows, seg_len, width):
    return lax.broadcasted_iota(jnp.int32, (rows, width), 0) & (seg_len - 1)


XA_SNB = 4


def _xattn_sample_body(q_ref, k_ref, v_ref, oin_ref, o_ref, acc_ref):
    del oin_ref
    part = pl.program_id(1)
    rows = SAMPLE_NB * SAMPLE_T

    @pl.when(part == 0)
    def _():
        acc_ref[...] = jnp.zeros_like(acc_ref)

    lane_blocks = XA_HD // 128

    def head(ref, i, h):
        return jnp.concatenate([ref[i, pl.ds(j * XA_HEADS + h, MEM_LEN, stride=lane_blocks * XA_HEADS), :]
                                for j in range(lane_blocks)], axis=1).astype(BF16)

    pairs = [(i, h) for i in range(XA_SNB) for h in range(XA_HEADS)]
    s = jnp.concatenate([_dot_nt(q_ref[:, h * XA_HD:(h + 1) * XA_HD], head(k_ref, i, h)) for i, h in pairs], axis=0)
    p = _softmax_rows(s * XA_HD ** -0.5).astype(BF16)
    seg = _row_segment(rows, SAMPLE_T, XA_HD)
    for h in range(XA_HEADS):
        sl = slice(h * XA_HD, (h + 1) * XA_HD)
        o = acc_ref[:, sl]
        for i in range(XA_SNB):
            n = i * XA_HEADS + h
            o = jnp.where(seg == part * XA_SNB + i, _dot(p[n * rows:(n + 1) * rows], head(v_ref, i, h)), o)
        acc_ref[:, sl] = o

    @pl.when(part == pl.num_programs(1) - 1)
    def _():
        o_ref[...] = acc_ref[...].astype(o_ref.dtype)


def _xattn_sample(q, cache_k, cache_v, layer, o_all, rows_p, bs):
    rows = SAMPLE_NB * SAMPLE_T
    rb0 = rows_p // rows
    nparts = SAMPLE_NB // XA_SNB
    depth = cache_k.shape[0]
    lane_blocks = XA_HD // 128
    kv_rows = MEM_LEN * lane_blocks * XA_HEADS

    def relayout(c):
        c = c.reshape(depth, bs, MEM_LEN, XA_HEADS, lane_blocks, 128).transpose(0, 1, 2, 4, 3, 5)
        return c.reshape(depth, bs, kv_rows, 128)
    cache_k, cache_v = relayout(cache_k), relayout(cache_v)
    kv_spec = pl.BlockSpec((None, XA_SNB, kv_rows, 128), lambda bb, s: (layer, bb * nparts + s, 0, 0))
    return pl.pallas_call(
        _xattn_sample_body,
        out_shape=jax.ShapeDtypeStruct(o_all.shape, o_all.dtype),
        grid=(bs // SAMPLE_NB, nparts),
        in_specs=[pl.BlockSpec((rows, D_MODEL), lambda bb, s: (rb0 + bb, 0)), kv_spec, kv_spec,
                  pl.BlockSpec(memory_space=pl.ANY)],
        out_specs=pl.BlockSpec((rows, D_MODEL), lambda bb, s: (rb0 + bb, 0)),
        scratch_shapes=[pltpu.VMEM((rows, D_MODEL), F32)],
        input_output_aliases={3: 0},
        compiler_params=_cparams(2),
    )(q, cache_k, cache_v, o_all)


LRU_CW = 512
LRU_TC = 256


def _lru_gate_scan(xc, wa_ref, wx_ref, ba, bx, lam, h0, seg_len):
    rows = xc.shape[0]
    rs, gis = [], []
    for n in range(LRU_CW // LRU_BLOCK):
        xb = xc[:, n * LRU_BLOCK:(n + 1) * LRU_BLOCK].astype(BF16)
        rs.append(_dot(xb, wa_ref[n].astype(BF16)))
        gis.append(_dot(xb, wx_ref[n].astype(BF16)))
    r = jax.nn.sigmoid(jnp.concatenate(rs, axis=1) + ba)
    gi = jax.nn.sigmoid(jnp.concatenate(gis, axis=1) + bx)
    log_a = -LRU_C * r * jax.nn.softplus(-lam)
    a = jnp.exp(log_a)
    b = jnp.sqrt(-_expm1(2.0 * log_a)) * (gi * xc)
    t = _row_pos(rows, seg_len, LRU_CW)
    shift = 1
    while shift < seg_len:
        keep = t >= shift
        a_prev = pltpu.roll(a, shift, 0)
        b_prev = pltpu.roll(b, shift, 0)
        b = jnp.where(keep, a * b_prev + b, b)
        a = jnp.where(keep, a * a_prev, a)
        shift *= 2
    return b + a * h0


def _lru_prompt_body(x_ref, z_ref, cw_ref, cb_ref, wa_ref, wx_ref, ba_ref, bx_ref, lam_ref,
                     y_ref, hl_ref, xext_ref, hc_ref):
    c = pl.program_id(2)

    @pl.when(c == 0)
    def _():
        xext_ref[0:8, :] = jnp.zeros((8, LRU_CW), F32)
        hc_ref[...] = jnp.zeros_like(hc_ref)

    x = x_ref[...]
    xext_ref[8:8 + LRU_TC, :] = x
    xc = cw_ref[3:4, :] * x + cb_ref[...]
    for k in range(CONV_K - 1):
        xc = xc + cw_ref[k:k + 1, :] * xext_ref[5 + k:5 + k + LRU_TC, :]
    xext_ref[0:8, :] = x[LRU_TC - 8:LRU_TC, :]
    h = _lru_gate_scan(xc, wa_ref, wx_ref, ba_ref[...], bx_ref[...], lam_ref[...], hc_ref[0:1, :], LRU_TC)
    hc_ref[0:1, :] = h[LRU_TC - 1:LRU_TC, :]
    y_ref[...] = (h * jax.nn.silu(z_ref[...])).astype(y_ref.dtype)

    @pl.when(c == pl.num_programs(2) - 1)
    def _():
        hl_ref[...] = h[LRU_TC - 1:LRU_TC, :]


def _lru_param_specs(n_grid):
    def cmap(block):
        if n_grid == 3:
            return lambda b, cb, c: block(cb)
        return lambda cb: block(cb)
    return [pl.BlockSpec((None, CONV_K, LRU_CW), cmap(lambda cb: (0, 0, cb))),
            pl.BlockSpec((None, 1, LRU_CW), cmap(lambda cb: (0, 0, cb))),
            pl.BlockSpec((None, LRU_CW // LRU_BLOCK, LRU_BLOCK, LRU_BLOCK), cmap(lambda cb: (0, cb, 0, 0))),
            pl.BlockSpec((None, LRU_CW // LRU_BLOCK, LRU_BLOCK, LRU_BLOCK), cmap(lambda cb: (0, cb, 0, 0))),
            pl.BlockSpec((None, 1, LRU_CW), cmap(lambda cb: (0, 0, cb))),
            pl.BlockSpec((None, 1, LRU_CW), cmap(lambda cb: (0, 0, cb))),
            pl.BlockSpec((None, 1, LRU_CW), cmap(lambda cb: (0, 0, cb)))]


def _lru_params(conv_w, conv_b, wa, wx, ba, bx, lam):
    w = LRU_WIDTH
    return [conv_w, conv_b.reshape(1, 1, w), wa, wx, ba.reshape(1, 1, w), bx.reshape(1, 1, w), lam.reshape(1, 1, w)]


def _lru_prompt(xz, params, bp, tp):
    m = xz.shape[0]
    nc = tp // LRU_TC
    ncb = LRU_WIDTH // LRU_CW
    return pl.pallas_call(
        _lru_prompt_body,
        out_shape=(jax.ShapeDtypeStruct((m, LRU_WIDTH), BF16), jax.ShapeDtypeStruct((bp, 1, LRU_WIDTH), F32)),
        grid=(bp, ncb, nc),
        in_specs=[pl.BlockSpec((LRU_TC, LRU_CW), lambda b, cb, c: (b * nc + c, cb)),
                  pl.BlockSpec((LRU_TC, LRU_CW), lambda b, cb, c: (b * nc + c, ncb + cb))] + _lru_param_specs(3),
        out_specs=(pl.BlockSpec((LRU_TC, LRU_CW), lambda b, cb, c: (b * nc + c, cb)),
                   pl.BlockSpec((None, 1, LRU_CW), lambda b, cb, c: (b, 0, cb))),
        scratch_shapes=[pltpu.VMEM((LRU_TC + 8, LRU_CW), F32), pltpu.VMEM((8, LRU_CW), F32)],
        compiler_params=_cparams(3),
    )(xz, xz, *params)


def _lru_sample_body(x_ref, z_ref, p1_ref, p2_ref, p3_ref, h0_ref, cw_ref, cb_ref, wa_ref, wx_ref,
                     ba_ref, bx_ref, lam_ref, yin_ref, y_ref, h_ref):
    del yin_ref
    x = x_ref[...]
    rows = x.shape[0]
    t = _row_pos(rows, SAMPLE_T, LRU_CW)
    xc = cw_ref[3:4, :] * x + cb_ref[...]
    for k, prev_ref in ((1, p1_ref), (2, p2_ref), (3, p3_ref)):
        xc = xc + cw_ref[3 - k:4 - k, :] * jnp.where(t >= k, pltpu.roll(x, k, 0), prev_ref[...])
    h = _lru_gate_scan(xc, wa_ref, wx_ref, ba_ref[...], bx_ref[...], lam_ref[...], h0_ref[...], SAMPLE_T)
    h_ref[...] = h
    y_ref[...] = (h * jax.nn.silu(z_ref[...])).astype(y_ref.dtype)


def _lru_sample(xz, prevs, h0_rows, params, y_all, rows_p):
    rows_s = prevs[0].shape[0]
    rb0 = rows_p // rows_s
    ncb = LRU_WIDTH // LRU_CW
    small = pl.BlockSpec((rows_s, LRU_CW), lambda cb: (0, cb))
    return pl.pallas_call(
        _lru_sample_body,
        out_shape=(jax.ShapeDtypeStruct(y_all.shape, y_all.dtype), jax.ShapeDtypeStruct((rows_s, LRU_WIDTH), F32)),
        grid=(ncb,),
        in_specs=[pl.BlockSpec((rows_s, LRU_CW), lambda cb: (rb0, cb)),
                  pl.BlockSpec((rows_s, LRU_CW), lambda cb: (rb0, ncb + cb)),
                  small, small, small, small] + _lru_param_specs(1) + [pl.BlockSpec(memory_space=pl.ANY)],
        out_specs=(pl.BlockSpec((rows_s, LRU_CW), lambda cb: (rb0, cb)), small),
        input_output_aliases={13: 0},
        compiler_params=_cparams(1),
    )(xz, xz, *prevs, h0_rows, *params, y_all)


def _conv_prev_rows(buf):
    b, _, c = buf.shape
    out = []
    for k in range(1, CONV_K):
        pad = jnp.zeros((b, SAMPLE_T - k, c), buf.dtype)
        out.append(jnp.concatenate([buf[:, CONV_K - 1 - k:], pad], axis=1).reshape(b * SAMPLE_T, c))
    return out


def _rope(x, cos, sin):
    half = RET_DK // 2
    x1, x2 = x[:, :half], x[:, half:]
    return jnp.concatenate([x1 * cos - x2 * sin, x1 * sin + x2 * cos], axis=1)


def _ret_chunk(q, k, v, z, cos, sin, dmask, qdec, kdec, cdec, gn, states, seg_len):
    rows = q.shape[0]
    qb = _rope(q, cos, sin).astype(BF16)
    kr = _rope(k, cos, sin) * RET_DK ** -0.5
    kb = kr.astype(BF16)
    vb = v.astype(BF16)
    kd = kr * kdec
    o = _dot((_dot_nt(qb, kb) * dmask).astype(BF16), vb)
    new_states = []
    single = len(states) == 1
    seg_v = None if single else _row_segment(rows, seg_len, RET_DV)
    seg_k = None if single else _row_segment(rows, seg_len, RET_DK)
    for i, s in enumerate(states):
        cross = _dot(qb, s.astype(BF16)) * qdec
        kdi = kd
        if not single:
            cross = jnp.where(seg_v == i, cross, 0.0)
            kdi = jnp.where(seg_k == i, kd, 0.0)
        o = o + cross
        new_states.append(s * cdec + _dot_tn(kdi.astype(BF16), vb))
    mu = jnp.mean(o, axis=-1, keepdims=True)
    var = jnp.mean(jnp.square(o - mu), axis=-1, keepdims=True)
    on = (o - mu) * lax.rsqrt(var + EPS) * gn
    return on * jax.nn.silu(z), new_states


def _ret_prompt_body(q_ref, k_ref, v_ref, z_ref, cos_ref, sin_ref, dm_ref, qd_ref, kd_ref, cd_ref, gn_ref,
                     y_ref, sl_ref, s_ref):
    c = pl.program_id(1)

    @pl.when(c == 0)
    def _():
        s_ref[...] = jnp.zeros_like(s_ref)

    cos, sin = cos_ref[...], sin_ref[...]
    for h in range(RET_HEADS):
        ksl = slice(h * RET_DK, (h + 1) * RET_DK)
        vsl = slice(h * RET_DV, (h + 1) * RET_DV)
        y, (s_new,) = _ret_chunk(q_ref[:, ksl], k_ref[:, ksl], v_ref[:, vsl], z_ref[:, vsl], cos, sin,
                                 dm_ref[h], qd_ref[h], kd_ref[h], cd_ref[h], gn_ref[:, vsl], [s_ref[h]], CHUNK)
        s_ref[h] = s_new
        y_ref[:, vsl] = y.astype(y_ref.dtype)

        @pl.when(c == pl.num_programs(1) - 1)
        def _():
            sl_ref[h] = s_new


def _ret_tables(seg_len, nseg, pos):
    rows = seg_len * nseg
    log_g = jnp.log1p(-jnp.exp2(-5.0 - jnp.arange(RET_HEADS, dtype=F32)))[:, None, None]
    t = (jnp.arange(rows) % seg_len).astype(F32)
    seg = jnp.arange(rows) // seg_len
    rel = t[:, None] - t[None, :]
    ok = (rel >= 0) & (seg[:, None] == seg[None, :])
    dmask = jnp.where(ok, jnp.exp(log_g * jnp.where(ok, rel, 0.0)), 0.0)
    qdec = jnp.broadcast_to(jnp.exp(log_g * (t + 1.0)[None, :, None]), (RET_HEADS, rows, RET_DV))
    kdec = jnp.broadcast_to(jnp.exp(log_g * (seg_len - 1.0 - t)[None, :, None]), (RET_HEADS, rows, RET_DK))
    cdec = jnp.broadcast_to(jnp.exp(log_g * seg_len), (RET_HEADS, 1, RET_DV))
    half = RET_DK // 2
    inv = ROPE_BASE ** (-jnp.arange(half, dtype=F32) / half)
    ang = pos.astype(F32)[:, None] * inv
    return jnp.cos(ang), jnp.sin(ang), dmask, qdec, kdec, cdec


def _ret_prompt(qkvz, gn, bp, tp):
    m = qkvz.shape[0]
    nc = tp // CHUNK
    cos, sin, dmask, qdec, kdec, cdec = _ret_tables(CHUNK, 1, jnp.arange(tp))
    qk_w = RET_HEADS * RET_DK
    full = lambda b, c: (0, 0, 0)
    return pl.pallas_call(
        _ret_prompt_body,
        out_shape=(jax.ShapeDtypeStruct((m, RET_VW), BF16),
                   jax.ShapeDtypeStruct((bp, RET_HEADS, RET_DK, RET_DV), F32)),
        grid=(bp, nc),
        in_specs=[pl.BlockSpec((CHUNK, qk_w), lambda b, c: (b * nc + c, 0)),
                  pl.BlockSpec((CHUNK, qk_w), lambda b, c: (b * nc + c, 1)),
                  pl.BlockSpec((CHUNK, RET_VW), lambda b, c: (b * nc + c, 1)),
                  pl.BlockSpec((CHUNK, RET_VW), lambda b, c: (b * nc + c, 2)),
                  pl.BlockSpec((CHUNK, RET_DK // 2), lambda b, c: (c, 0)),
                  pl.BlockSpec((CHUNK, RET_DK // 2), lambda b, c: (c, 0)),
                  pl.BlockSpec((RET_HEADS, CHUNK, CHUNK), full),
                  pl.BlockSpec((RET_HEADS, CHUNK, RET_DV), full),
                  pl.BlockSpec((RET_HEADS, CHUNK, RET_DK), full),
                  pl.BlockSpec((RET_HEADS, 1, RET_DV), full),
                  pl.BlockSpec((1, RET_VW), lambda b, c: (0, 0))],
        out_specs=(pl.BlockSpec((CHUNK, RET_VW), lambda b, c: (b * nc + c, 0)),
                   pl.BlockSpec((None, RET_HEADS, RET_DK, RET_DV), lambda b, c: (b, 0, 0, 0))),
        scratch_shapes=[pltpu.VMEM((RET_HEADS, RET_DK, RET_DV), F32)],
        compiler_params=_cparams(2),
    )(qkvz, qkvz, qkvz, qkvz, cos, sin, dmask, qdec, kdec, cdec, gn.reshape(1, RET_VW))


def _ret_sample_body(q_ref, k_ref, v_ref, z_ref, cos_ref, sin_ref, dm_ref, qd_ref, kd_ref, cd_ref, gn_ref,
                     s0_ref, yin_ref, y_ref, s_ref):
    del yin_ref
    cos, sin = cos_ref[...], sin_ref[...]
    for h in range(RET_SAMPLE_HEADS):
        ksl = slice(h * RET_DK, (h + 1) * RET_DK)
        vsl = slice(h * RET_DV, (h + 1) * RET_DV)
        y, s_new = _ret_chunk(q_ref[:, ksl], k_ref[:, ksl], v_ref[:, vsl], z_ref[:, vsl], cos, sin,
                              dm_ref[h], qd_ref[h], kd_ref[h], cd_ref[h], gn_ref[:, vsl],
                              [s0_ref[i, h] for i in range(SAMPLE_NB)], SAMPLE_T)
        for i in range(SAMPLE_NB):
            s_ref[i, h] = s_new[i]
        y_ref[:, vsl] = y.astype(y_ref.dtype)


RET_SAMPLE_HEADS = 4


def _ret_sample(qkvz, gn, state, y_all, rows_p, bs):
    rows = SAMPLE_NB * SAMPLE_T
    rb0 = rows_p // rows
    pos = PAST_LEN + jnp.arange(rows) % SAMPLE_T
    cos, sin, dmask, qdec, kdec, cdec = _ret_tables(SAMPLE_T, SAMPLE_NB, pos)
    nh = RET_SAMPLE_HEADS
    kw, vw = nh * RET_DK, nh * RET_DV
    k_blk = RET_HEADS * RET_DK // kw
    v_blk = 2 * RET_HEADS * RET_DK // vw
    z_blk = v_blk + RET_HEADS // nh
    st_spec = pl.BlockSpec((None, SAMPLE_NB, nh, RET_DK, RET_DV), lambda bb, p: (0, bb, p, 0, 0))
    return pl.pallas_call(
        _ret_sample_body,
        out_shape=(jax.ShapeDtypeStruct(y_all.shape, y_all.dtype), jax.ShapeDtypeStruct(state.shape, F32)),
        grid=(bs // SAMPLE_NB, RET_HEADS // nh),
        in_specs=[pl.BlockSpec((rows, kw), lambda bb, p: (rb0 + bb, p)),
                  pl.BlockSpec((rows, kw), lambda bb, p: (rb0 + bb, k_blk + p)),
                  pl.BlockSpec((rows, vw), lambda bb, p: (rb0 + bb, v_blk + p)),
                  pl.BlockSpec((rows, vw), lambda bb, p: (rb0 + bb, z_blk + p)),
                  pl.BlockSpec((rows, RET_DK // 2), lambda bb, p: (0, 0)),
                  pl.BlockSpec((rows, RET_DK // 2), lambda bb, p: (0, 0)),
                  pl.BlockSpec((nh, rows, rows), lambda bb, p: (p, 0, 0)),
                  pl.BlockSpec((nh, rows, RET_DV), lambda bb, p: (p, 0, 0)),
                  pl.BlockSpec((nh, rows, RET_DK), lambda bb, p: (p, 0, 0)),
                  pl.BlockSpec((nh, 1, RET_DV), lambda bb, p: (p, 0, 0)),
                  pl.BlockSpec((1, vw), lambda bb, p: (0, p)),
                  st_spec,
                  pl.BlockSpec(memory_space=pl.ANY)],
        out_specs=(pl.BlockSpec((rows, vw), lambda bb, p: (rb0 + bb, p)), st_spec),
        input_output_aliases={12: 0},
        compiler_params=_cparams(2),
    )(qkvz, qkvz, qkvz, qkvz, cos, sin, dmask, qdec, kdec, cdec, gn.reshape(1, RET_VW), state, y_all)


def _head_expand_matrix():
    r = lax.broadcasted_iota(jnp.int32, (128, SSD_GW), 0)
    c = lax.broadcasted_iota(jnp.int32, (128, SSD_GW), 1)
    return jnp.where(r == (c >> _log2(SSD_HEADDIM)), 1.0, 0.0).astype(BF16)


def _expand_heads(v, expand_mat):
    v1 = v.astype(BF16)
    r1 = v - v1.astype(F32)
    v2 = r1.astype(BF16)
    v3 = (r1 - v2.astype(F32)).astype(BF16)
    return (_dot(v1, expand_mat) + _dot(v2, expand_mat)) + _dot(v3, expand_mat)


def _ssd_chunk(xs, bs, cs, z, dt_raw, dt_bias, a_log, d_skip, norm_g, states, seg_len, expand_mat):
    rows = xs.shape[0]
    single = len(states) == 1
    hp = SSD_HPG
    pad = 128
    t8 = _row_pos(rows, seg_len, pad)
    seg8 = _row_segment(rows, seg_len, pad)
    tt = lax.broadcasted_iota(jnp.int32, (rows, rows), 0)
    ss = lax.broadcasted_iota(jnp.int32, (rows, rows), 1)
    causal = (tt >= ss) & ((tt >> _log2(seg_len)) == (ss >> _log2(seg_len)))

    def lane_pad(v):
        return jnp.concatenate([v, jnp.zeros((v.shape[0], pad - hp), F32)], axis=1)

    dt = jax.nn.softplus(lane_pad(dt_raw) + lane_pad(dt_bias))
    da = dt * (-jnp.exp(lane_pad(a_log)))
    cum = da
    shift = 1
    while shift < seg_len:
        cum = cum + jnp.where(t8 >= shift, pltpu.roll(cum, shift, 0), 0.0)
        shift *= 2
    cum_sq = cum if rows == pad else jnp.concatenate([cum, jnp.zeros((pad - rows, pad), F32)], axis=0)
    cum_t = cum_sq.T[0:hp, 0:rows]
    lasts = [cum[(i + 1) * seg_len - 1:(i + 1) * seg_len, :] for i in range(len(states))]
    last_row = lasts[0]
    if not single:
        last_row = jnp.zeros((rows, pad), F32)
        for i, l in enumerate(lasts):
            last_row = jnp.where(seg8 == i, l, last_row)
    to_end = jnp.exp(last_row - cum)
    expanded = _expand_heads(jnp.concatenate([dt, jnp.exp(cum), to_end], axis=0), expand_mat)
    dt_x, ecum, to_end_x = expanded[0:rows], expanded[rows:2 * rows], expanded[2 * rows:3 * rows]

    csb = cs.astype(BF16)
    bsb = bs.astype(BF16)
    cb = _dot_nt(csb, bsb)
    xdt = xs * dt_x
    ycols = []
    for r in range(hp):
        lmat = jnp.where(causal, jnp.exp(cum[:, r:r + 1] - cum_t[r:r + 1, :]), 0.0)
        ycols.append(_dot((cb * lmat).astype(BF16), xdt[:, r * SSD_HEADDIM:(r + 1) * SSD_HEADDIM].astype(BF16)))
    y = jnp.concatenate(ycols, axis=1)
    xte = xdt * to_end_x
    segw = None if single else _row_segment(rows, seg_len, SSD_GW)
    new_states = []
    for i, h in enumerate(states):
        y_off = _dot_nt(csb, h.astype(BF16)) * ecum
        xi = xte
        if not single:
            y_off = jnp.where(segw == i, y_off, 0.0)
            xi = jnp.where(segw == i, xte, 0.0)
        y = y + y_off
        e_last = jnp.exp(lasts[i])
        dec = jnp.concatenate([jnp.broadcast_to(e_last[:, r:r + 1], (SSD_HEADDIM, SSD_STATE)) for r in range(hp)],
                              axis=0)
        new_states.append(h * dec + _dot_tn(xi.astype(BF16), bsb))
    y = y + xs * _expand_cols(d_skip, SSD_HEADDIM)
    yg = y * jax.nn.silu(z)
    yg = yg * lax.rsqrt(jnp.mean(yg * yg, axis=-1, keepdims=True) + EPS)
    return yg * norm_g, new_states


def _ssd_prompt_body(z_ref, x_ref, b_ref, c_ref, dt_ref, wx_ref, wb_ref, wc_ref, bx_ref, bb_ref, bc_ref,
                     dtb_ref, al_ref, ds_ref, ng_ref, y_ref, hl_ref, xe_ref, be_ref, ce_ref, h_ref):
    c = pl.program_id(1)

    @pl.when(c == 0)
    def _():
        xe_ref[0:8, :] = jnp.zeros((8, SSD_INNER), F32)
        be_ref[0:8, :] = jnp.zeros((8, SSD_GROUPS * SSD_STATE), F32)
        ce_ref[0:8, :] = jnp.zeros((8, SSD_GROUPS * SSD_STATE), F32)
        h_ref[...] = jnp.zeros_like(h_ref)

    def conv(raw_ref, ext_ref, w_ref, bias_ref, sl):
        raw = raw_ref[:, sl]
        ext_ref[8:8 + CHUNK, sl] = raw
        acc = w_ref[3:4, sl] * raw + bias_ref[:, sl]
        for k in range(CONV_K - 1):
            acc = acc + w_ref[k:k + 1, sl] * ext_ref[5 + k:5 + k + CHUNK, sl]
        ext_ref[0:8, sl] = raw[CHUNK - 8:CHUNK, :]
        return jax.nn.silu(acc)

    expand_mat = _head_expand_matrix()
    for g in range(SSD_GROUPS):
        xsl = slice(g * SSD_GW, (g + 1) * SSD_GW)
        nsl = slice(g * SSD_STATE, (g + 1) * SSD_STATE)
        xs = conv(x_ref, xe_ref, wx_ref, bx_ref, xsl)
        bs = conv(b_ref, be_ref, wb_ref, bb_ref, nsl)
        cs = conv(c_ref, ce_ref, wc_ref, bc_ref, nsl)
        y, (h_new,) = _ssd_chunk(xs, bs, cs, z_ref[:, xsl], dt_ref[g], dtb_ref[g], al_ref[g], ds_ref[g],
                                 ng_ref[:, xsl], [h_ref[g]], CHUNK, expand_mat)
        h_ref[g] = h_new
        y_ref[:, xsl] = y.astype(y_ref.dtype)

        @pl.when(c == pl.num_programs(1) - 1)
        def _():
            hl_ref[g] = h_new


SSD_SAMPLE_GROUPS = 8


def _ssd_param_arrays(conv_w, conv_b, dt_bias, a_log, d_skip, norm_g):
    g, hp = SSD_GROUPS, SSD_HPG
    return [conv_w, conv_w, conv_w, conv_b.reshape(1, 1, -1), conv_b.reshape(1, 1, -1), conv_b.reshape(1, 1, -1),
            dt_bias.reshape(g, 1, hp), a_log.reshape(g, 1, hp), d_skip.reshape(g, 1, hp), norm_g.reshape(1, SSD_INNER)]


def _ssd_prompt(zx, dtg, params, bp, tp):
    m = zx.shape[0]
    nc = tp // CHUNK
    gn = SSD_GROUPS * SSD_STATE
    b_blk = 2 * SSD_INNER // gn
    wb_blk = SSD_INNER // gn
    hp3 = (SSD_GROUPS, 1, SSD_HPG)
    zero3 = lambda b, c: (0, 0, 0)
    return pl.pallas_call(
        _ssd_prompt_body,
        out_shape=(jax.ShapeDtypeStruct((m, SSD_INNER), BF16),
                   jax.ShapeDtypeStruct((bp, SSD_GROUPS, SSD_GW, SSD_STATE), F32)),
        grid=(bp, nc),
        in_specs=[pl.BlockSpec((CHUNK, SSD_INNER), lambda b, c: (b * nc + c, 0)),
                  pl.BlockSpec((CHUNK, SSD_INNER), lambda b, c: (b * nc + c, 1)),
                  pl.BlockSpec((CHUNK, gn), lambda b, c: (b * nc + c, b_blk)),
                  pl.BlockSpec((CHUNK, gn), lambda b, c: (b * nc + c, b_blk + 1)),
                  pl.BlockSpec((SSD_GROUPS, CHUNK, SSD_HPG), lambda b, c: (0, b * nc + c, 0)),
                  pl.BlockSpec((None, CONV_K, SSD_INNER), zero3),
                  pl.BlockSpec((None, CONV_K, gn), lambda b, c: (0, 0, wb_blk)),
                  pl.BlockSpec((None, CONV_K, gn), lambda b, c: (0, 0, wb_blk + 1)),
                  pl.BlockSpec((None, 1, SSD_INNER), zero3),
                  pl.BlockSpec((None, 1, gn), lambda b, c: (0, 0, wb_blk)),
                  pl.BlockSpec((None, 1, gn), lambda b, c: (0, 0, wb_blk + 1)),
                  pl.BlockSpec(hp3, zero3), pl.BlockSpec(hp3, zero3), pl.BlockSpec(hp3, zero3),
                  pl.BlockSpec((1, SSD_INNER), lambda b, c: (0, 0))],
        out_specs=(pl.BlockSpec((CHUNK, SSD_INNER), lambda b, c: (b * nc + c, 0)),
                   pl.BlockSpec((None, SSD_GROUPS, SSD_GW, SSD_STATE), lambda b, c: (b, 0, 0, 0))),
        scratch_shapes=[pltpu.VMEM((CHUNK + 8, SSD_INNER), F32), pltpu.VMEM((CHUNK + 8, gn), F32),
                        pltpu.VMEM((CHUNK + 8, gn), F32), pltpu.VMEM((SSD_GROUPS, SSD_GW, SSD_STATE), F32)],
        compiler_params=_cparams(2),
    )(zx, zx, zx, zx, dtg, *params)


def _ssd_sample_body(z_ref, x_ref, b_ref, c_ref, dt_ref, px_ref, pb_ref, pc_ref, wx_ref, wb_ref, wc_ref,
                     bx_ref, bb_ref, bc_ref, dtb_ref, al_ref, ds_ref, ng_ref, h0_ref, yin_ref,
                     y_ref, h_ref, nx_ref, nb_ref, nc_ref):
    del yin_ref

    def conv(raw_ref, buf_ref, new_ref, w_ref, bias_ref):
        raw = raw_ref[...]
        rows, width = raw.shape
        t = _row_pos(rows, SAMPLE_T, width)
        acc = w_ref[3:4, :] * raw + bias_ref[...]
        for k in range(1, CONV_K):
            prev = jnp.concatenate([piece for i in range(SAMPLE_NB) for piece in
                                    (buf_ref[i, CONV_K - 1 - k:CONV_K - 1, :], jnp.zeros((SAMPLE_T - k, width), F32))],
                                   axis=0)
            acc = acc + w_ref[3 - k:4 - k, :] * jnp.where(t >= k, pltpu.roll(raw, k, 0), prev)
        for i in range(SAMPLE_NB):
            new_ref[i] = raw[i * SAMPLE_T + SAMPLE_T - (CONV_K - 1):(i + 1) * SAMPLE_T, :]
        return jax.nn.silu(acc)

    xs_all = conv(x_ref, px_ref, nx_ref, wx_ref, bx_ref)
    bs_all = conv(b_ref, pb_ref, nb_ref, wb_ref, bb_ref)
    cs_all = conv(c_ref, pc_ref, nc_ref, wc_ref, bc_ref)
    expand_mat = _head_expand_matrix()
    for g in range(SSD_SAMPLE_GROUPS):
        xsl = slice(g * SSD_GW, (g + 1) * SSD_GW)
        nsl = slice(g * SSD_STATE, (g + 1) * SSD_STATE)
        y, h_new = _ssd_chunk(xs_all[:, xsl], bs_all[:, nsl], cs_all[:, nsl], z_ref[:, xsl], dt_ref[g], dtb_ref[g],
                              al_ref[g], ds_ref[g], ng_ref[:, xsl], [h0_ref[i, g] for i in range(SAMPLE_NB)],
                              SAMPLE_T, expand_mat)
        for i in range(SAMPLE_NB):
            h_ref[i, g] = h_new[i]
        y_ref[:, xsl] = y.astype(y_ref.dtype)


def _ssd_sample(zx, dtg, conv_state, params, state, y_all, rows_p, bs):
    rows = SAMPLE_NB * SAMPLE_T
    rb0 = rows_p // rows
    ng = SSD_SAMPLE_GROUPS
    xw, nw = ng * SSD_GW, ng * SSD_STATE
    gn = SSD_GROUPS * SSD_STATE
    nk = CONV_K - 1
    x_blk = SSD_INNER // xw
    b_blk, c_blk = 2 * SSD_INNER // nw, (2 * SSD_INNER + gn) // nw
    wb_blk, wc_blk = SSD_INNER // nw, (SSD_INNER + gn) // nw
    hp3 = (ng, 1, SSD_HPG)
    st_spec = pl.BlockSpec((None, SAMPLE_NB, ng, SSD_GW, SSD_STATE), lambda bb, q: (0, bb, q, 0, 0))
    new_x_spec = pl.BlockSpec((SAMPLE_NB, nk, xw), lambda bb, q: (bb, 0, q))
    new_n_spec = pl.BlockSpec((SAMPLE_NB, nk, nw), lambda bb, q: (bb, 0, q))
    return pl.pallas_call(
        _ssd_sample_body,
        out_shape=(jax.ShapeDtypeStruct(y_all.shape, y_all.dtype), jax.ShapeDtypeStruct(state.shape, F32),
                   jax.ShapeDtypeStruct((bs, nk, SSD_INNER), F32), jax.ShapeDtypeStruct((bs, nk, gn), F32),
                   jax.ShapeDtypeStruct((bs, nk, gn), F32)),
        grid=(bs // SAMPLE_NB, SSD_GROUPS // ng),
        in_specs=[pl.BlockSpec((rows, xw), lambda bb, q: (rb0 + bb, q)),
                  pl.BlockSpec((rows, xw), lambda bb, q: (rb0 + bb, x_blk + q)),
                  pl.BlockSpec((rows, nw), lambda bb, q: (rb0 + bb, b_blk + q)),
                  pl.BlockSpec((rows, nw), lambda bb, q: (rb0 + bb, c_blk + q)),
                  pl.BlockSpec((ng, rows, SSD_HPG), lambda bb, q: (q, rb0 + bb, 0)),
                  pl.BlockSpec((None, SAMPLE_NB, nk, xw), lambda bb, q: (0, bb, 0, q)),
                  pl.BlockSpec((None, SAMPLE_NB, nk, nw), lambda bb, q: (0, bb, 0, wb_blk + q)),
                  pl.BlockSpec((None, SAMPLE_NB, nk, nw), lambda bb, q: (0, bb, 0, wc_blk + q)),
                  pl.BlockSpec((None, CONV_K, xw), lambda bb, q: (0, 0, q)),
                  pl.BlockSpec((None, CONV_K, nw), lambda bb, q: (0, 0, wb_blk + q)),
                  pl.BlockSpec((None, CONV_K, nw), lambda bb, q: (0, 0, wc_blk + q)),
                  pl.BlockSpec((None, 1, xw), lambda bb, q: (0, 0, q)),
                  pl.BlockSpec((None, 1, nw), lambda bb, q: (0, 0, wb_blk + q)),
                  pl.BlockSpec((None, 1, nw), lambda bb, q: (0, 0, wc_blk + q)),
                  pl.BlockSpec(hp3, lambda bb, q: (q, 0, 0)),
                  pl.BlockSpec(hp3, lambda bb, q: (q, 0, 0)),
                  pl.BlockSpec(hp3, lambda bb, q: (q, 0, 0)),
                  pl.BlockSpec((1, xw), lambda bb, q: (0, q)),
                  st_spec, pl.BlockSpec(memory_space=pl.ANY)],
        out_specs=(pl.BlockSpec((rows, xw), lambda bb, q: (rb0 + bb, q)), st_spec,
                   new_x_spec, new_n_spec, new_n_spec),
        input_output_aliases={19: 0},
        compiler_params=_cparams(2),
    )(zx, zx, zx, zx, dtg, conv_state, conv_state, conv_state, *params, state, y_all)


def _s5_body(*refs, sub, nsub, n_scan, has_h0, has_alias):
    it = iter(refs)
    x_ref, wdc_ref, kdc_ref, vtc_ref = next(it), next(it), next(it), next(it)
    are_ref, aim_ref, sre_ref, sim_ref, dsk_ref = (next(it) for _ in range(5))
    h0_ref = next(it) if has_h0 else None
    if has_alias:
        next(it)
    s_ref, hl_ref = next(it), next(it)
    m1_ref, vt_ref = next(it), next(it)
    hw2 = 2 * S5_HW

    @pl.when(pl.program_id(1) == 0)
    def _():
        def iota(shape, axis):
            return lax.broadcasted_iota(jnp.int32, shape, axis)

        def expand(compact, spread, keep):
            return jnp.where(keep, _dot(compact.astype(BF16), spread), 0.0).astype(BF16)

        q, col = iota((128, hw2), 0), iota((128, hw2), 1)
        spread = jnp.where(((q >> 6) == (col >> 9)) & ((q & 63) == (col & 63)), 1.0, 0.0).astype(BF16)
        row, col = iota((sub * 128, hw2), 0), iota((sub * 128, hw2), 1)
        m1_ref[:, 0:hw2] = expand(wdc_ref[...], spread, ((row >> 4) & 7) == ((col >> 6) & 7))
        q, col = iota((128, 128), 0), iota((128, 128), 1)
        spread = jnp.where((q < S5_GROUP) & (q == (col & 15)), 1.0, 0.0).astype(BF16)
        keep = (q >> 4) == (col >> 4)
        kd = [expand(kdc_ref[d], spread, keep) for d in range(sub)]
        for s in range(sub):
            for t in range(sub):
                blk = kd[t - s] if t >= s else jnp.zeros((128, 128), BF16)
                m1_ref[s * 128:(s + 1) * 128, hw2 + t * 128:hw2 + (t + 1) * 128] = blk
        q, col = iota((128, sub * 128), 0), iota((128, sub * 128), 1)
        spread = jnp.where(((q >> 4) == (col >> 7)) & ((q & 15) == (col & 15)), 1.0, 0.0).astype(BF16)
        row, col = iota((hw2, sub * 128), 0), iota((hw2, sub * 128), 1)
        vt_ref[...] = expand(vtc_ref[...], spread, ((row >> 6) & 7) == ((col >> 4) & 7))

    xs = [x_ref[pl.ds(s, nsub, stride=sub), :] for s in range(sub)]
    r = _dot(jnp.concatenate(xs, axis=1).astype(BF16), m1_ref[...])
    cb, y = r[:, :2 * S5_HW], r[:, 2 * S5_HW:]

    def cmul(h, a_re, a_im):
        return h * a_re + pltpu.roll(h, S5_HW, 1) * a_im

    if has_h0:
        hprev = h0_ref[...]
        h = cb + cmul(hprev, are_ref[...], aim_ref[...])
    else:
        h = cb
        j = lax.broadcasted_iota(jnp.int32, h.shape, 0)
        for step in range(n_scan):
            shift = 1 << step
            prev = jnp.where(j >= shift, pltpu.roll(h, shift, 0), 0.0)
            h = h + cmul(prev, sre_ref[step], sim_ref[step])
        hprev = jnp.where(j >= 1, pltpu.roll(h, 1, 0), 0.0)
    hl_ref[...] = h[nsub - 1:nsub, :] if not has_h0 else h
    y = y + _dot(hprev.astype(BF16), vt_ref[...])
    for t in range(sub):
        s_ref[pl.ds(t, nsub, stride=sub), :] = y[:, t * 128:(t + 1) * 128] + dsk_ref[...] * xs[t]


def _s5_tables(lam_re, lam_im, b_re, b_im, c_re, c_im, log_dt, sub, n_scan):
    nb, g8, gc, p = S5_NBLK, S5_G8, S5_GROUP, S5_STATE
    dt = jnp.exp(log_dt)[:, None]

    def apow(ds):
        d = jnp.asarray(ds, F32).reshape(-1, 1, 1)
        mag = jnp.exp(lam_re * dt * d)
        return mag * jnp.cos(lam_im * dt * d), mag * jnp.sin(lam_im * dt * d)

    a_re, a_im = (v[0] for v in apow([1.0]))
    den = lam_re * lam_re + lam_im * lam_im
    f_re = ((a_re - 1.0) * lam_re + a_im * lam_im) / den
    f_im = (a_im * lam_re - (a_re - 1.0) * lam_im) / den
    bb_re = f_re[..., None] * b_re - f_im[..., None] * b_im
    bb_im = f_re[..., None] * b_im + f_im[..., None] * b_re

    def lane_pad(m):
        return jnp.pad(m, [(0, 0)] * (m.ndim - 1) + [(0, 128 - m.shape[-1])])

    p_re, p_im = apow(range(sub))
    ab_re = p_re[..., None] * bb_re - p_im[..., None] * bb_im
    ab_im = p_re[..., None] * bb_im + p_im[..., None] * bb_re
    kd = jnp.einsum('gcp,dgpk->dgkc', c_re, ab_re) - jnp.einsum('gcp,dgpk->dgkc', c_im, ab_im)
    kdc = lane_pad(kd.reshape(sub, nb, g8 * gc, gc).transpose(1, 0, 2, 3))
    w = jnp.concatenate([jnp.swapaxes(ab_re, 2, 3), jnp.swapaxes(ab_im, 2, 3)], axis=3)[::-1]
    wdc = w.reshape(sub, nb, g8 * gc, 2 * p).transpose(1, 0, 2, 3).reshape(nb, sub * g8 * gc, 2 * p)
    q_re, q_im = apow(range(1, sub + 1))
    m_re = c_re[None] * q_re[:, :, None, :] - c_im[None] * q_im[:, :, None, :]
    m_im = c_re[None] * q_im[:, :, None, :] + c_im[None] * q_re[:, :, None, :]

    def rows(m):
        return m.reshape(sub, nb, g8, gc, p).transpose(1, 2, 4, 0, 3).reshape(nb, g8 * p, sub * gc)
    vtc = lane_pad(jnp.concatenate([rows(m_re), -rows(m_im)], axis=1))

    def lanes(re, im):
        n = re.shape[0]
        re = re.reshape(n, nb, 1, S5_HW).transpose(1, 0, 2, 3)
        im = im.reshape(n, nb, 1, S5_HW).transpose(1, 0, 2, 3)
        return jnp.concatenate([re, re], axis=3), jnp.concatenate([-im, im], axis=3)

    are, aim = (v[:, 0] for v in lanes(*apow([float(sub)])))
    sre, sim = lanes(*apow([float(sub * (1 << k)) for k in range(max(n_scan, 1))]))
    return wdc, kdc, vtc, are, aim, sre, sim


def _s5_core(xz, tables, d_skip, s_all, h0, *, sub, nsub, n_seq, row_block0, n_scan):
    has_h0 = h0 is not None
    rows = sub * nsub
    ns = tables[5].shape[1]
    hw2 = 2 * S5_HW
    sw = sub * 128

    def blk(shape, f):
        return pl.BlockSpec(shape, lambda gb, b: f(gb, b))
    in_specs = [blk((rows, 128), lambda gb, b: (row_block0 + b, gb)),
                blk((None, sw, 128), lambda gb, b: (gb, 0, 0)),
                blk((None, sub, 128, 128), lambda gb, b: (gb, 0, 0, 0)),
                blk((None, hw2, 128), lambda gb, b: (gb, 0, 0)),
                blk((None, 1, hw2), lambda gb, b: (gb, 0, 0)),
                blk((None, 1, hw2), lambda gb, b: (gb, 0, 0)),
                blk((None, ns, 1, hw2), lambda gb, b: (gb, 0, 0, 0)),
                blk((None, ns, 1, hw2), lambda gb, b: (gb, 0, 0, 0)),
                blk((1, 128), lambda gb, b: (0, gb))]
    args = [xz, *tables, d_skip.reshape(1, -1)]
    if has_h0:
        in_specs.append(blk((None, nsub, hw2), lambda gb, b: (gb, 0, 0)))
        args.append(h0)
        hl_shape = (S5_NBLK, nsub, hw2)
        hl_spec = blk((None, nsub, hw2), lambda gb, b: (gb, 0, 0))
    else:
        hl_shape = (S5_NBLK, n_seq, 1, hw2)
        hl_spec = blk((None, None, 1, hw2), lambda gb, b: (gb, b, 0, 0))
    aliases = {}
    if s_all is not None:
        in_specs.append(pl.BlockSpec(memory_space=pl.ANY))
        args.append(s_all)
        aliases = {len(args) - 1: 0}
    body = functools.partial(_s5_body, sub=sub, nsub=nsub, n_scan=n_scan, has_h0=has_h0,
                             has_alias=s_all is not None)
    return pl.pallas_call(
        body,
        out_shape=(jax.ShapeDtypeStruct((xz.shape[0], D_MODEL), F32), jax.ShapeDtypeStruct(hl_shape, F32)),
        grid=(S5_NBLK, n_seq),
        in_specs=in_specs,
        out_specs=(blk((rows, 128), lambda gb, b: (row_block0 + b, gb)), hl_spec),
        scratch_shapes=[pltpu.VMEM((sw, hw2 + sw), BF16), pltpu.VMEM((hw2, sw), BF16)],
        input_output_aliases=aliases,
        compiler_params=_cparams(2),
    )(*args)


def _s5_glu_body(s_ref, st_ref, z_ref, w_ref, b_ref, o_ref, gb_ref, *, tm, row_chunk):
    @pl.when(pl.program_id(1) == 0)
    def _():
        def body(i, carry):
            r0 = pl.multiple_of(i * row_chunk, 16)
            gb_ref[pl.ds(r0, row_chunk), :] = jax.nn.gelu(s_ref[pl.ds(r0, row_chunk), :]).astype(BF16)
            return carry
        lax.fori_loop(0, tm // row_chunk, body, 0)
    g = jax.nn.gelu(st_ref[...])
    o = g * jax.nn.sigmoid(_dot(gb_ref[...], w_ref[...].astype(BF16)) + b_ref[...])
    o_ref[...] = (o * jax.nn.silu(z_ref[...])).astype(o_ref.dtype)


def _s5_glu(s_all, xz, glu_w, glu_b):
    m, k = s_all.shape
    tn = 512
    tm = _largest_divisor(m, 1088, 16)
    row_chunk = _largest_divisor(tm, 272, 16)
    zb0 = k // tn
    body = functools.partial(_s5_glu_body, tm=tm, row_chunk=row_chunk)
    return pl.pallas_call(
        body,
        out_shape=jax.ShapeDtypeStruct((m, k), BF16),
        grid=(m // tm, k // tn),
        in_specs=[pl.BlockSpec((tm, k), lambda i, j: (i, 0)),
                  pl.BlockSpec((tm, tn), lambda i, j: (i, j)),
                  pl.BlockSpec((tm, tn), lambda i, j: (i, zb0 + j)),
                  pl.BlockSpec((None, k, tn), lambda i, j: (0, 0, j)),
                  pl.BlockSpec((1, tn), lambda i, j: (0, j))],
        out_specs=pl.BlockSpec((tm, tn), lambda i, j: (i, j)),
        scratch_shapes=[pltpu.VMEM((tm, k), BF16)],
        compiler_params=_cparams(2),
    )(s_all, s_all, xz, glu_w, glu_b.reshape(1, k))


def _last_conv_inputs(a, bp, tp, col0, n_cols):
    return jnp.stack([a[(b + 1) * tp - (CONV_K - 1):(b + 1) * tp, col0:col0 + n_cols] for b in range(bp)])


def _lru_layer(in_proj, out_proj, j, dims, state_conv, state_h, w_in, conv_w, conv_b, wa, ba, wx, bx, lam, w_out):
    bp, tp, bs = dims
    rows_p = bp * tp
    xz = in_proj(w_in, j)
    params = _lru_params(conv_w[j:j + 1], conv_b[j], wa[j:j + 1], wx[j:j + 1], ba[j], bx[j], lam[j])
    y, hl_p = _lru_prompt(xz, params, bp, tp)
    prevs = _conv_prev_rows(state_conv[j])
    h0_rows = jnp.repeat(state_h[j], SAMPLE_T, axis=0)
    y, h_rows = _lru_sample(xz, prevs, h0_rows, params, y, rows_p)
    x_s = xz[rows_p:, :LRU_WIDTH].reshape(bs, SAMPLE_T, LRU_WIDTH)
    outs = (_last_conv_inputs(xz, bp, tp, 0, LRU_WIDTH), x_s[:, SAMPLE_T - (CONV_K - 1):],
            hl_p.reshape(bp, LRU_WIDTH), h_rows.reshape(bs, SAMPLE_T, LRU_WIDTH)[:, SAMPLE_T - 1])
    return out_proj(y, w_out, j), outs


def _ret_layer(in_proj, out_proj, j, dims, state, w_in, gn, w_out):
    bp, tp, bs = dims
    rows_p = bp * tp
    qkvz = in_proj(w_in, j)
    y, s_p = _ret_prompt(qkvz, gn[j], bp, tp)
    y, s_s = _ret_sample(qkvz, gn[j], state[j:j + 1], y, rows_p, bs)
    return out_proj(y, w_out, j, tn=256), (s_p, s_s[0])


def _ssd_layer(in_proj, out_proj, j, dims, state_conv, state, w_in, conv_w, conv_b, dt_bias, a_log, d_skip,
               norm_g, w_out):
    bp, tp, bs = dims
    rows_p = bp * tp
    m = rows_p + bs * SAMPLE_T
    n_main = SSD_INNER + SSD_CONV_DIM
    w_t = jnp.swapaxes(w_in, 1, 2)
    zx = in_proj(w_t, j, n_cols=n_main, w_transposed=True)
    dt = in_proj(w_t[j:j + 1, n_main:, :], 0, w_transposed=True)
    dtg = dt.reshape(m, SSD_GROUPS, SSD_HPG).transpose(1, 0, 2)
    params = _ssd_param_arrays(conv_w[j:j + 1], conv_b[j], dt_bias[j], a_log[j], d_skip[j], norm_g[j])
    y, hl_p = _ssd_prompt(zx, dtg, params, bp, tp)
    st = state[j:j + 1].reshape(1, bs, SSD_GROUPS, SSD_GW, SSD_STATE)
    y, hl_s, *new_conv = _ssd_sample(zx, dtg, state_conv[j:j + 1], params, st, y, rows_p, bs)
    shape = (SSD_HEADS, SSD_HEADDIM, SSD_STATE)
    outs = (_last_conv_inputs(zx, bp, tp, SSD_INNER, SSD_CONV_DIM), jnp.concatenate(new_conv, axis=2),
            hl_p.reshape((bp,) + shape), hl_s.reshape((bs,) + shape))
    return out_proj(y, w_out, j, tn=256), outs


S5_SUB = 8


def _s5_layer(in_proj, out_proj, j, dims, h0_re, h0_im, w_in, lam_re, lam_im, b_re, b_im, c_re, c_im, d_skip,
              log_dt, glu_w, glu_b, w_out):
    bp, tp, bs = dims
    rows_p = bp * tp
    xz = in_proj(w_in, j)
    nsub_p = tp // S5_SUB
    n_scan = max(nsub_p - 1, 0).bit_length()
    par = (lam_re[j], lam_im[j], b_re[j], b_im[j], c_re[j], c_im[j], log_dt[j])
    s_all, hl_p = _s5_core(xz, _s5_tables(*par, S5_SUB, n_scan), d_skip[j], None, None,
                           sub=S5_SUB, nsub=nsub_p, n_seq=bp, row_block0=0, n_scan=n_scan)

    def to_lanes(v):
        return v.reshape(bs, S5_NBLK, S5_HW).transpose(1, 0, 2)
    h0 = jnp.concatenate([to_lanes(h0_re[j]), to_lanes(h0_im[j])], axis=2)
    s_all, hl_s = _s5_core(xz, _s5_tables(*par, SAMPLE_T, 0), d_skip[j], s_all, h0,
                           sub=SAMPLE_T, nsub=bs, n_seq=1, row_block0=rows_p // (bs * SAMPLE_T), n_scan=0)
    y = _s5_glu(s_all, xz, glu_w[j:j + 1], glu_b[j])

    def from_lanes(v, nb):
        return v.transpose(1, 0, 2).reshape(nb, S5_GROUPS, S5_STATE)
    hl_p = hl_p.reshape(S5_NBLK, bp, 2 * S5_HW)
    outs = (from_lanes(hl_p[..., :S5_HW], bp), from_lanes(hl_s[..., :S5_HW], bs),
            from_lanes(hl_p[..., S5_HW:], bp), from_lanes(hl_s[..., S5_HW:], bs))
    return out_proj(y, w_out, j), outs


def kernel(x_prompt, x_sample, state_lru_conv, state_lru_h, state_ret, state_ssd_conv, state_ssd, state_s5_re, state_s5_im, cache_mem_k, cache_mem_v, mem_prompt, mix_norm, xa_norm, xa_mem_norm, xa_wq, xa_wkv, xa_wo, final_norm, lru_w_in, lru_conv_w, lru_conv_b, lru_wa, lru_ba, lru_wx, lru_bx, lru_lambda, lru_w_out, ret_w_in, ret_gn, ret_w_out, ssd_w_in, ssd_conv_w, ssd_conv_b, ssd_dt_bias, ssd_a_log, ssd_d, ssd_norm, ssd_w_out, s5_w_in, s5_lambda_re, s5_lambda_im, s5_b_re, s5_b_im, s5_c_re, s5_c_im, s5_d, s5_log_dt, s5_glu_w, s5_glu_b, s5_w_out):
    bp, tp, d = x_prompt.shape
    bs, ts, _ = x_sample.shape
    depth = mix_norm.shape[0]
    assert d == D_MODEL and ts == SAMPLE_T and tp % LRU_TC == 0 and bs % SAMPLE_NB == 0
    rows_p, rows_s = bp * tp, bs * ts
    assert rows_p % rows_s == 0
    dims = (bp, tp, bs)
    h = jnp.concatenate([x_prompt.reshape(rows_p, d), x_sample.reshape(rows_s, d)], axis=0)
    mem = mem_prompt.reshape(bp * MEM_LEN, d)
    mix_g = mix_norm.reshape(depth, 1, d)
    xa_g = xa_norm.reshape(depth, 1, d)
    mem_g = xa_mem_norm.reshape(depth, 1, d)

    outs = {k: [] for k in ("lru", "ret", "ssd", "s5")}
    kv = None
    xn = None
    for i in range(depth):
        kind, j = i % 4, i // 4

        def normed_proj(gain, w, idx, h=h, xn=xn, i=i, **kw):
            if xn is not None:
                return _matmul(xn, w, idx, **kw)
            return _matmul(h, w, idx, norm_g=gain, g_idx=i, **kw)

        def out_proj(y, w, idx, h=h, i=i, **kw):
            if w.shape[1] == D_MODEL:
                return _matmul_res_norm(y, w, idx, h, xa_g, i)
            return _matmul(y, w, idx, residual=h, **kw), None

        in_proj = functools.partial(normed_proj, mix_g)
        if kind == 0:
            (h, xn), o = _lru_layer(in_proj, out_proj, j, dims, state_lru_conv, state_lru_h, lru_w_in,
                                    lru_conv_w, lru_conv_b, lru_wa, lru_ba, lru_wx, lru_bx, lru_lambda, lru_w_out)
            outs["lru"].append(o)
        elif kind == 1:
            (h, xn), o = _ret_layer(in_proj, out_proj, j, dims, state_ret, ret_w_in, ret_gn, ret_w_out)
            outs["ret"].append(o)
        elif kind == 2:
            (h, xn), o = _ssd_layer(in_proj, out_proj, j, dims, state_ssd_conv, state_ssd, ssd_w_in,
                                    ssd_conv_w, ssd_conv_b, ssd_dt_bias, ssd_a_log, ssd_d, ssd_norm, ssd_w_out)
            outs["ssd"].append(o)
        else:
            (h, xn), o = _s5_layer(in_proj, out_proj, j, dims, state_s5_re, state_s5_im, s5_w_in,
                                   s5_lambda_re, s5_lambda_im, s5_b_re, s5_b_im, s5_c_re, s5_c_im, s5_d, s5_log_dt,
                                   s5_glu_w, s5_glu_b, s5_w_out)
            outs["s5"].append(o)
        kv = _matmul(mem, xa_wkv, i, norm_g=mem_g, g_idx=i, stack=(depth, i, kv))
        q = normed_proj(xa_g, xa_wq, i, h=h, xn=xn, out_dtype=BF16)
        o_att = _xattn_prompt(q, kv, i, bp, tp)
        o_att = _xattn_sample(q, cache_mem_k, cache_mem_v, i, o_att, rows_p, bs)
        if i + 1 < depth:
            h, xn = _matmul_res_norm(o_att, xa_wo, i, h, mix_g, i + 1)
        else:
            h, xn = _matmul(o_att, xa_wo, i, residual=h), None
    y_p = _rmsnorm(h, final_norm, 0, rows_p)
    y_s = _rmsnorm(h, final_norm, rows_p, rows_s)

    def stack(kind, idx):
        return jnp.stack([o[idx] for o in outs[kind]])
    return (y_p.reshape(bp, tp, d), y_s.reshape(bs, ts, d),
            stack("lru", 0), stack("lru", 1), stack("lru", 2), stack("lru", 3),
            stack("ret", 0), stack("ret", 1),
            stack("ssd", 0), stack("ssd", 1), stack("ssd", 2), stack("ssd", 3),
            stack("s5", 0), stack("s5", 1), stack("s5", 2), stack("s5", 3),
            kv[:, :, :d].reshape(depth, bp, MEM_LEN, XA_HEADS, XA_HD),
            kv[:, :, d:].reshape(depth, bp, MEM_LEN, XA_HEADS, XA_HD))
```
